```python
import math
import jax, jax.numpy as jnp
from jax import lax
import numpy as np

D_MODEL = 2048
BATCH = 4
SEQ = 4096
DEPTH = 1

GRID_W = 64
NA_HEADS = 16
NA_HEAD_DIM = 64
D_NA = NA_HEADS * NA_HEAD_DIM
D_HY = D_MODEL - D_NA
NA_WIN_ROWS_MAX = 8
NA_WIN_COLS = 16
SHORT_CONV_W = 3
FILTER_EMB = 33
FILTER_ORDER = 64
DECAY_TARGET = 1e-2
FAST_DECAY_PCT = 0.3
SLOW_DECAY_PCT = 1.5
D_FF = 5632
FFN_CONV_W = 3
D_IN = 3 * D_NA + 3 * D_HY
N_MOD = 6
EPS = 1e-6

kernel_name = 'hymba_na_hyena_convglu_block'


def rmsnorm(x, g):
    xf = x.astype(jnp.float32)
    y = xf * lax.rsqrt(jnp.mean(xf * xf, axis=-1, keepdims=True) + EPS)
    return (y * g.astype(jnp.float32)).astype(x.dtype)


def dwconv_centred(u, w, b):
    K = w.shape[0]
    pad = K // 2
    L = u.shape[1]
    up = jnp.pad(u, ((0, 0), (pad, pad), (0, 0)))
    out = up[:, 0:L] * w[0]
    for j in range(1, K):
        out = out + up[:, j:j + L] * w[j]
    return out + b


def neighbourhood_attention(q, k, v, rpb):
    B, L, H, Dh = q.shape
    rows = L // GRID_W
    kh = min(NA_WIN_ROWS_MAX, rows)
    q = q.reshape(B, rows, GRID_W, H, Dh)
    k = k.reshape(B, rows, GRID_W, H, Dh)
    v = v.reshape(B, rows, GRID_W, H, Dh)
    r = jnp.arange(rows)
    row_start = jnp.clip(r - kh // 2, 0, rows - kh)
    key_rows = row_start[:, None] + jnp.arange(kh)[None, :]
    kb = k[:, key_rows]
    vb = v[:, key_rows]
    s = jnp.einsum('brqhd,brikhd->bhrqik', q, kb).astype(jnp.float32) * (NA_HEAD_DIM ** -0.5)
    cq = jnp.arange(GRID_W)
    ck = jnp.arange(GRID_W)
    col_start = jnp.clip(cq - NA_WIN_COLS // 2, 0, GRID_W - NA_WIN_COLS)
    col_mask = (ck[None, :] >= col_start[:, None]) & (ck[None, :] < col_start[:, None] + NA_WIN_COLS)
    row_off = key_rows - r[:, None] + (NA_WIN_ROWS_MAX - 1)
    col_off = jnp.clip(ck[None, :] - cq[:, None], -(NA_WIN_COLS - 1), NA_WIN_COLS - 1) + (NA_WIN_COLS - 1)
    bias = rpb[:, row_off[:, None, :, None], col_off[None, :, None, :]]
    s = s + bias.astype(jnp.float32)[None]
    s = jnp.where(col_mask[:, None, :], s, -jnp.inf)
    p = jax.nn.softmax(s.reshape(B, H, rows, GRID_W, kh * GRID_W), axis=-1)
    p = p.reshape(B, H, rows, GRID_W, kh, GRID_W).astype(vb.dtype)
    o = jnp.einsum('bhrqik,brikhd->brqhd', p, vb)
    return o.reshape(B, L, H * Dh)


def hyena_filters(L, w1, b1, w2, b2, w3, b3, w4, freq):
    f32 = jnp.float32
    t = jnp.linspace(0.0, 1.0, L, dtype=f32)[:, None]
    bands = (FILTER_EMB - 1) // 2
    w = 2.0 * math.pi * jnp.arange(L, dtype=f32)[:, None] / L
    fr = jnp.linspace(1e-4, bands - 1, bands, dtype=f32)[None, :]
    z = jnp.concatenate([t, jnp.cos(fr * w), -jnp.sin(fr * w)], axis=-1)
    fq = freq.astype(f32)
    h = jnp.sin(fq * (z @ w1.astype(f32) + b1.astype(f32)))
    h = jnp.sin(fq * (h @ w2.astype(f32) + b2.astype(f32)))
    h = jnp.sin(fq * (h @ w3.astype(f32) + b3.astype(f32)))
    h = (h @ w4.astype(f32)).reshape(L, 2, D_HY)
    max_decay = math.log(DECAY_TARGET) / FAST_DECAY_PCT
    min_decay = math.log(DECAY_TARGET) / SLOW_DECAY_PCT
    deltas = jnp.linspace(min_decay, max_decay, D_HY, dtype=f32)[None, :]
    decay = jnp.exp(-t * jnp.abs(deltas))
    h = h * decay[:, None, :]
    return h[:, 0], h[:, 1]


def bidirectional_fftconv(u, h_fwd, h_bwd):
    L = u.shape[1]
    C = u.shape[2]
    two_sided = jnp.concatenate([h_fwd, jnp.zeros((1, C), jnp.float32), h_bwd[:0:-1]], axis=0)
    k_f = jnp.fft.rfft(two_sided, n=2 * L, axis=0)
    u_f = jnp.fft.rfft(u.astype(jnp.float32), n=2 * L, axis=1)
    y = jnp.fft.irfft(u_f * k_f[None], n=2 * L, axis=1)[:, :L]
    return y.astype(u.dtype)


def hyena_mixer(hy_in, short_w, short_b, w1, b1, w2, b2, w3, b3, w4, freq, d_bias):
    L = hy_in.shape[1]
    uc = dwconv_centred(hy_in, short_w, short_b)
    x0, x1, v = jnp.split(uc, 3, axis=-1)
    h_fwd, h_bwd = hyena_filters(L, w1, b1, w2, b2, w3, b3, w4, freq)
    v = v * x1
    v = bidirectional_fftconv(v, h_fwd, h_bwd) + v * d_bias
    return v * x0


def setup_inputs(seed: int = 0) -> dict:
    key = jax.random.key(seed)
    ks = jax.random.split(key, 32)
    f32 = jnp.float32

    def nrm(k, shape, scale):
        return jax.random.normal(k, shape, f32) * scale

    Lr = DEPTH
    return {
        'x': nrm(ks[0], (BATCH, SEQ, D_MODEL), 1.0),
        'c': nrm(ks[1], (BATCH, D_MODEL), 1.0),
        'w_ada': nrm(ks[2], (Lr, D_MODEL, N_MOD * D_MODEL), 0.5 * D_MODEL ** -0.5),
        'b_ada': nrm(ks[3], (Lr, N_MOD * D_MODEL), 0.02),
        'g_mix': 1.0 + nrm(ks[4], (Lr, D_MODEL), 0.02),
        'w_in': nrm(ks[5], (Lr, D_MODEL, D_IN), D_MODEL ** -0.5),
        'na_rpb': nrm(ks[6], (Lr, NA_HEADS, 2 * NA_WIN_ROWS_MAX - 1, 2 * NA_WIN_COLS - 1), 0.1),
        'hy_short_w': nrm(ks[7], (Lr, SHORT_CONV_W, 3 * D_HY), SHORT_CONV_W ** -0.5),
        'hy_short_b': nrm(ks[8], (Lr, 3 * D_HY), 0.02),
        'hy_filt_w1': nrm(ks[9], (Lr, FILTER_EMB, FILTER_ORDER), FILTER_EMB ** -0.5),
        'hy_filt_b1': nrm(ks[10], (Lr, FILTER_ORDER), 0.02),
        'hy_filt_w2': nrm(ks[11], (Lr, FILTER_ORDER, FILTER_ORDER), FILTER_ORDER ** -0.5),
        'hy_filt_b2': nrm(ks[12], (Lr, FILTER_ORDER), 0.02),
        'hy_filt_w3': nrm(ks[13], (Lr, FILTER_ORDER, FILTER_ORDER), FILTER_ORDER ** -0.5),
        'hy_filt_b3': nrm(ks[14], (Lr, FILTER_ORDER), 0.02),
        'hy_filt_w4': nrm(ks[15], (Lr, FILTER_ORDER, 2 * D_HY), FILTER_ORDER ** -0.5),
        'hy_filt_freq': 1.0 + nrm(ks[16], (Lr, FILTER_ORDER), 0.01),
        'hy_bias': nrm(ks[17], (Lr, D_HY), 0.1),
        'beta_na': 1.0 + nrm(ks[18], (Lr, D_NA), 0.02),
        'beta_hy': 1.0 + nrm(ks[19], (Lr, D_HY), 0.02),
        'w_out': nrm(ks[20], (Lr, D_MODEL, D_MODEL), D_MODEL ** -0.5),
        'g_ffn': 1.0 + nrm(ks[21], (Lr, D_MODEL), 0.02),
        'w_up': nrm(ks[22], (Lr, D_MODEL, 2 * D_FF), D_MODEL ** -0.5),
        'ffn_conv_w': nrm(ks[23], (Lr, FFN_CONV_W, D_FF), FFN_CONV_W ** -0.5),
        'ffn_conv_b': nrm(ks[24], (Lr, D_FF), 0.02),
        'w_down': nrm(ks[25], (Lr, D_FF, D_MODEL), D_FF ** -0.5),
        'g_final': 1.0 + nrm(ks[26], (D_MODEL,), 0.02),
    }


def reference(x, c, w_ada, b_ada, g_mix, w_in, na_rpb, hy_short_w, hy_short_b,
              hy_filt_w1, hy_filt_b1, hy_filt_w2, hy_filt_b2, hy_filt_w3, hy_filt_b3,
              hy_filt_w4, hy_filt_freq, hy_bias, beta_na, beta_hy, w_out, g_ffn,
              w_up, ffn_conv_w, ffn_conv_b, w_down, g_final):
    B, L, _ = x.shape
    cond = jax.nn.silu(c)
    for l in range(DEPTH):
        mod = cond @ w_ada[l] + b_ada[l]
        sh1, sc1, gt1, sh2, sc2, gt2 = jnp.split(mod[:, None, :], N_MOD, axis=-1)
        h = rmsnorm(x, g_mix[l]) * (1.0 + sc1) + sh1
        proj = h @ w_in[l]
        q = proj[..., 0:D_NA].reshape(B, L, NA_HEADS, NA_HEAD_DIM)
        k = proj[..., D_NA:2 * D_NA].reshape(B, L, NA_HEADS, NA_HEAD_DIM)
        v = proj[..., 2 * D_NA:3 * D_NA].reshape(B, L, NA_HEADS, NA_HEAD_DIM)
        hy_in = proj[..., 3 * D_NA:]
        na_out = neighbourhood_attention(q, k, v, na_rpb[l])
        hy_out = hyena_mixer(hy_in, hy_short_w[l], hy_short_b[l],
                             hy_filt_w1[l], hy_filt_b1[l], hy_filt_w2[l], hy_filt_b2[l],
                             hy_filt_w3[l], hy_filt_b3[l], hy_filt_w4[l], hy_filt_freq[l],
                             hy_bias[l])
        merged = jnp.concatenate([rmsnorm(na_out, beta_na[l]), rmsnorm(hy_out, beta_hy[l])], axis=-1)
        x = x + gt1 * (merged @ w_out[l])
        h = rmsnorm(x, g_ffn[l]) * (1.0 + sc2) + sh2
        a, b = jnp.split(h @ w_up[l], 2, axis=-1)
        a = dwconv_centred(a, ffn_conv_w[l], ffn_conv_b[l])
        x = x + gt2 * ((jax.nn.gelu(a, approximate=False) * b) @ w_down[l])
    return rmsnorm(x, g_final)
```

```python
import functools
import math

import numpy as np
import jax
import jax.numpy as jnp
from jax import lax
from jax.experimental import pallas as pl
from jax.experimental.pallas import tpu as pltpu

F32 = jnp.float32
BF16 = jnp.bfloat16

GRID_W = 64
NA_HEADS = 16
NA_HEAD_DIM = 64
D_NA = NA_HEADS * NA_HEAD_DIM
NA_WIN_ROWS = 8
NA_WIN_COLS = 16
FILTER_EMB = 33
DECAY_TARGET = 1e-2
FAST_DECAY_PCT = 0.3
SLOW_DECAY_PCT = 1.5
EPS = 1e-6
NEG_BIG = -1e30

V7X_LANES = 128
V7X_VMEM_BYTES = 64 * 1024 * 1024
VMEM_LIMIT = 56 * 1024 * 1024

FFT_R = 128
FFT_S = 64

ROW_CHUNK = 64


def _cparams(sem, vmem=VMEM_LIMIT):
    return pltpu.CompilerParams(dimension_semantics=sem, vmem_limit_bytes=vmem)


def _for_row_chunks(n_rows, chunk, fn):
    def body(i, carry):
        fn(pl.ds(pl.multiple_of(i * chunk, chunk), chunk))
        return carry
    lax.fori_loop(0, n_rows // chunk, body, 0)


def _adaln_kernel(c_ref, w_ref, b_ref, o_ref):
    c = c_ref[...]
    cond = c / (1.0 + jnp.exp(-c))
    o_ref[...] = jnp.dot(cond.astype(BF16), w_ref[...].astype(BF16),
                         preferred_element_type=F32) + b_ref[...]


def _adaln(c_pad, w_ada, b_ada, bn=1024):
    rows, d = c_pad.shape
    n = w_ada.shape[1]
    return pl.pallas_call(
        _adaln_kernel,
        grid=(n // bn,),
        in_specs=[pl.BlockSpec((rows, d), lambda j: (0, 0)),
                  pl.BlockSpec((d, bn), lambda j: (0, j)),
                  pl.BlockSpec((1, bn), lambda j: (0, j))],
        out_specs=pl.BlockSpec((rows, bn), lambda j: (0, j)),
        out_shape=jax.ShapeDtypeStruct((rows, n), F32),
        compiler_params=_cparams(("parallel",)),
        name="adaln",
    )(c_pad, w_ada, b_ada)


def _mm_norm_kernel(x_ref, g_ref, sc_ref, sh_ref, w_ref, o_ref, h_ref):
    @pl.when(pl.program_id(1) == 0)
    def _():
        def chunk(rs):
            x = x_ref[rs, :]
            ms = jnp.mean(x * x, axis=-1, keepdims=True)
            y = x * lax.rsqrt(ms + EPS) * g_ref[...]
            h = y * (1.0 + sc_ref[0]) + sh_ref[0]
            h_ref[rs, :] = h.astype(BF16)

        _for_row_chunks(x_ref.shape[0], ROW_CHUNK, chunk)

    o_ref[...] = jnp.dot(h_ref[...], w_ref[...],
                         preferred_element_type=F32).astype(o_ref.dtype)


def _mm_norm(x2d, g, sc, sh, w, seq_len, bm=1024, bn=1024):
    m, k = x2d.shape
    n = w.shape[1]
    blocks_per_seq = seq_len // bm
    return pl.pallas_call(
        _mm_norm_kernel,
        grid=(m // bm, n // bn),
        in_specs=[pl.BlockSpec((bm, k), lambda i, j: (i, 0)),
                  pl.BlockSpec((1, k), lambda i, j: (0, 0)),
                  pl.BlockSpec((1, 1, k), lambda i, j: (i // blocks_per_seq, 0, 0)),
                  pl.BlockSpec((1, 1, k), lambda i, j: (i // blocks_per_seq, 0, 0)),
                  pl.BlockSpec((k, bn), lambda i, j: (0, j))],
        out_specs=pl.BlockSpec((bm, bn), lambda i, j: (i, j)),
        out_shape=jax.ShapeDtypeStruct((m, n), BF16),
        scratch_shapes=[pltpu.VMEM((bm, k), BF16)],
        compiler_params=_cparams(("parallel", "arbitrary")),
        name="mm_norm",
    )(x2d, g, sc, sh, w)


def _bias_kernel(rpb_ref, o_ref):
    h = pl.program_id(0)
    w = pl.program_id(1)
    n_rows = 2 * NA_WIN_ROWS - 1
    n_cols = 2 * NA_WIN_COLS - 1
    shape = (GRID_W, 2 * GRID_W)
    lane = lax.broadcasted_iota(jnp.int32, shape, 1)
    cq = lax.broadcasted_iota(jnp.int32, shape, 0)
    ck = lane & (GRID_W - 1)
    first = lane < GRID_W
    cs = jnp.clip(cq - NA_WIN_COLS // 2, 0, GRID_W - NA_WIN_COLS)
    valid = (ck >= cs) & (ck < cs + NA_WIN_COLS)
    d = jnp.clip(ck - cq, -(NA_WIN_COLS - 1), NA_WIN_COLS - 1) + (NA_WIN_COLS - 1)
    tiles = []
    for ip in range(NA_WIN_ROWS // 2):
        base0 = (h * n_rows + w + 2 * ip) * n_cols
        base1 = base0 + n_cols
        acc = jnp.zeros(shape, F32)
        for dd in range(n_cols):
            val = jnp.where(first, rpb_ref[base0 + dd], rpb_ref[base1 + dd])
            acc = jnp.where(d == dd, val, acc)
        tiles.append(jnp.where(valid, acc, NEG_BIG))
    o_ref[...] = jnp.concatenate(tiles, axis=1)


def _bias_table(rpb_flat):
    return pl.pallas_call(
        _bias_kernel,
        grid=(NA_HEADS, NA_WIN_ROWS),
        in_specs=[pl.BlockSpec(memory_space=pltpu.SMEM)],
        out_specs=pl.BlockSpec((None, None, GRID_W, NA_WIN_ROWS * GRID_W),
                               lambda h, w: (h, w, 0, 0)),
        out_shape=jax.ShapeDtypeStruct(
            (NA_HEADS, NA_WIN_ROWS, GRID_W, NA_WIN_ROWS * GRID_W), F32),
        compiler_params=_cparams(("parallel", "parallel")),
        name="na_bias",
    )(rpb_flat)


def _na_kernel(q_ref, k_ref, v_ref, bias_ref, o_ref):
    scale = NA_HEAD_DIM ** -0.5
    n_keys = NA_WIN_ROWS * GRID_W
    lane = lax.broadcasted_iota(jnp.int32, (GRID_W, 2 * NA_HEAD_DIM), 1)
    even = lane < NA_HEAD_DIM

    def head(qh, kp, vp, h):
        s = lax.dot_general(qh, kp, (((1,), (1,)), ((), ())),
                            preferred_element_type=F32)
        s = s * scale + bias_ref[h]
        m = jnp.max(s, axis=-1, keepdims=True)
        p = jnp.exp(s - m)
        l = jnp.sum(p, axis=-1, keepdims=True)
        o = jnp.dot(p.astype(BF16), vp, preferred_element_type=F32)
        return o / l

    def body(hp, carry):
        off = pl.multiple_of(hp * (2 * NA_HEAD_DIM), 2 * NA_HEAD_DIM)
        qp = q_ref[:, pl.ds(off, 2 * NA_HEAD_DIM)]
        kp = k_ref[:, :, pl.ds(off, 2 * NA_HEAD_DIM)].reshape(n_keys, 2 * NA_HEAD_DIM)
        vp = v_ref[:, :, pl.ds(off, 2 * NA_HEAD_DIM)].reshape(n_keys, 2 * NA_HEAD_DIM)
        zero = jnp.zeros_like(qp)
        o_e = head(jnp.where(even, qp, zero), kp, vp, 2 * hp)
        o_o = head(jnp.where(even, zero, qp), kp, vp, 2 * hp + 1)
        o_ref[:, pl.ds(off, 2 * NA_HEAD_DIM)] = jnp.where(even, o_e, o_o).astype(o_ref.dtype)
        return carry

    lax.fori_loop(0, NA_HEADS // 2, body, 0)


def _na_attention(proj4, bias_tab):
    b, rows, width, _ = proj4.shape
    kh = NA_WIN_ROWS

    def row_start(r):
        return jnp.clip(r - kh // 2, 0, rows - kh)

    return pl.pallas_call(
        _na_kernel,
        grid=(b, rows),
        in_specs=[
            pl.BlockSpec((None, None, width, D_NA), lambda bi, r: (bi, r, 0, 0)),
            pl.BlockSpec((None, pl.Element(kh), pl.Element(width), pl.Element(D_NA)),
                         lambda bi, r: (bi, row_start(r), 0, D_NA)),
            pl.BlockSpec((None, pl.Element(kh), pl.Element(width), pl.Element(D_NA)),
                         lambda bi, r: (bi, row_start(r), 0, 2 * D_NA)),
            pl.BlockSpec((NA_HEADS, None, width, kh * width),
                         lambda bi, r: (0, row_start(r) - r + (kh - 1), 0, 0)),
        ],
        out_specs=pl.BlockSpec((None, None, width, D_NA), lambda bi, r: (bi, r, 0, 0)),
        out_shape=jax.ShapeDtypeStruct((b, rows, width, D_NA), BF16),
        compiler_params=_cparams(("parallel", "arbitrary")),
        name="na_attn",
    )(proj4, proj4, proj4, bias_tab)


def _conv3(u, w, b):
    n = u.shape[0]
    row = lax.broadcasted_iota(jnp.int32, u.shape, 0)
    prev = jnp.where(row == 0, 0.0, pltpu.roll(u, 1, axis=0))
    nxt = jnp.where(row == n - 1, 0.0, pltpu.roll(u, n - 1, axis=0))
    return prev * w[0:1] + u * w[1:2] + nxt * w[2:3] + b


def _hy_pre_kernel(u0_ref, u1_ref, u2_ref, w0_ref, w1_ref, w2_ref,
                   b0_ref, b1_ref, b2_ref, x0_ref, vx_ref):
    x0 = _conv3(u0_ref[...].astype(F32), w0_ref[...], b0_ref[...])
    x1 = _conv3(u1_ref[...].astype(F32), w1_ref[...], b1_ref[...])
    v = _conv3(u2_ref[...].astype(F32), w2_ref[...], b2_ref[...])
    x0_ref[...] = x0.astype(x0_ref.dtype)
    vx_ref[...] = (v * x1).astype(vx_ref.dtype)


def _hy_pre(proj3, short_w, short_b, d_hy, cb=128):
    b, l, _ = proj3.shape
    base = 3 * D_NA // cb
    per = d_hy // cb

    def u_spec(g):
        return pl.BlockSpec((None, l, cb), lambda bi, j: (bi, 0, base + g * per + j))

    def w_spec(g):
        return pl.BlockSpec((3, cb), lambda bi, j: (0, g * per + j))

    def b_spec(g):
        return pl.BlockSpec((1, cb), lambda bi, j: (0, g * per + j))

    out_spec = pl.BlockSpec((None, l, cb), lambda bi, j: (bi, 0, j))
    return pl.pallas_call(
        _hy_pre_kernel,
        grid=(b, per),
        in_specs=[u_spec(0), u_spec(1), u_spec(2), w_spec(0), w_spec(1), w_spec(2),
                  b_spec(0), b_spec(1), b_spec(2)],
        out_specs=[out_spec, out_spec],
        out_shape=[jax.ShapeDtypeStruct((b, l, d_hy), BF16),
                   jax.ShapeDtypeStruct((b, l, d_hy), BF16)],
        compiler_params=_cparams(("parallel", "parallel")),
        name="hy_pre",
    )(proj3, proj3, proj3, short_w, short_w, short_w, short_b, short_b, short_b)


def _filter_kernel(z_ref, zr_ref, w1_ref, b1_ref, w2_ref, b2_ref, w3_ref, b3_ref,
                   fq_ref, w4f_ref, w4b_ref, dl_ref, o_ref, hf_ref, hb_ref):
    hi = lax.Precision.HIGHEST
    seq = z_ref.shape[0]

    @pl.when(pl.program_id(0) == 0)
    def _():
        fq = fq_ref[...]

        def mlp(z):
            h = jnp.sin(fq * (jnp.dot(z, w1_ref[...], precision=hi,
                                      preferred_element_type=F32) + b1_ref[...]))
            h = jnp.sin(fq * (jnp.dot(h, w2_ref[...], precision=hi,
                                      preferred_element_type=F32) + b2_ref[...]))
            h = jnp.sin(fq * (jnp.dot(h, w3_ref[...], precision=hi,
                                      preferred_element_type=F32) + b3_ref[...]))
            return h

        def chunk(rs):
            hf_ref[rs, :] = mlp(z_ref[rs, :])
            hb_ref[rs, :] = mlp(zr_ref[rs, :])

        _for_row_chunks(seq, 4 * ROW_CHUNK, chunk)

    dl = dl_ref[...]
    t_f = z_ref[:, 0:1]
    t_b = zr_ref[:, 0:1]
    fwd = jnp.dot(hf_ref[...], w4f_ref[...], precision=hi, preferred_element_type=F32)
    fwd = fwd * jnp.exp(-t_f * dl)
    bwd = jnp.dot(hb_ref[...], w4b_ref[...], precision=hi, preferred_element_type=F32)
    bwd = bwd * jnp.exp(-t_b * dl)
    row = lax.broadcasted_iota(jnp.int32, bwd.shape, 0)
    bwd = jnp.where(row == 0, 0.0, bwd)
    o_ref[0:seq, :] = fwd.astype(o_ref.dtype)
    o_ref[seq:2 * seq, :] = bwd.astype(o_ref.dtype)


def _filter_taps(z, z_rev, w1p, b1, w2, b2, w3, b3, freq, w4, deltas_abs, d_hy, cb=256):
    seq, kpad = z.shape
    order = w2.shape[0]
    per = d_hy // cb
    full = lambda shape: pl.BlockSpec(shape, lambda j: tuple(0 for _ in shape))
    return pl.pallas_call(
        _filter_kernel,
        grid=(per,),
        in_specs=[full((seq, kpad)), full((seq, kpad)),
                  full((kpad, order)), full((1, order)),
                  full((order, order)), full((1, order)),
                  full((order, order)), full((1, order)),
                  full((1, order)),
                  pl.BlockSpec((order, cb), lambda j: (0, j)),
                  pl.BlockSpec((order, cb), lambda j: (0, per + j)),
                  pl.BlockSpec((1, cb), lambda j: (0, j))],
        out_specs=pl.BlockSpec((2 * seq, cb), lambda j: (0, j)),
        out_shape=jax.ShapeDtypeStruct((2 * seq, d_hy), BF16),
        scratch_shapes=[pltpu.VMEM((seq, order), F32), pltpu.VMEM((seq, order), F32)],
        compiler_params=_cparams(("arbitrary",)),
        name="hy_filter",
    )(z, z_rev, w1p, b1, w2, b2, w3, b3, freq, w4, w4, deltas_abs)


def _colmm_kernel(w_ref, x_ref, o_ref):
    o_ref[...] = jnp.dot(w_ref[...], x_ref[...],
                         preferred_element_type=F32).astype(o_ref.dtype)


def _colmm(w, x3, out_dtype, bc=8192):
    g, k, cols = x3.shape
    m = w.shape[0]
    return pl.pallas_call(
        _colmm_kernel,
        grid=(g, cols // bc),
        in_specs=[pl.BlockSpec((m, k), lambda gi, j: (0, 0)),
                  pl.BlockSpec((None, k, bc), lambda gi, j: (gi, 0, j))],
        out_specs=pl.BlockSpec((None, m, bc), lambda gi, j: (gi, 0, j)),
        out_shape=jax.ShapeDtypeStruct((g, m, cols), out_dtype),
        compiler_params=_cparams(("parallel", "parallel")),
        name="fft_slow",
    )(w, x3)


def _spec_filter_kernel(e_ref, a_ref, o_ref):
    two, r, c = a_ref.shape
    a2 = a_ref[...].reshape(two * r, c)
    k2 = jnp.dot(e_ref[...], a2, preferred_element_type=F32)
    o_ref[...] = k2.reshape(two, r, c)


def _spec_filter(e_fwd, a_filt):
    _, two, s, r, c = a_filt.shape
    return pl.pallas_call(
        _spec_filter_kernel,
        grid=(s,),
        in_specs=[pl.BlockSpec((None, two * r, two * r), lambda ks: (ks, 0, 0)),
                  pl.BlockSpec((None, two, None, r, c), lambda ks: (0, 0, ks, 0, 0))],
        out_specs=pl.BlockSpec((None, two, r, c), lambda ks: (ks, 0, 0, 0)),
        out_shape=jax.ShapeDtypeStruct((s, two, r, c), F32),
        compiler_params=_cparams(("parallel",)),
        name="fft_filter_spec",
    )(e_fwd, a_filt)


def _spec_conv_kernel(e_ref, ei_ref, a_ref, k_ref, o_ref):
    two, r, c = a_ref.shape
    a2 = a_ref[...].reshape(two * r, c)
    b2 = jnp.dot(e_ref[...], a2, preferred_element_type=F32)
    br, bi = b2[0:r], b2[r:2 * r]
    kr, ki = k_ref[0], k_ref[1]
    yr = br * kr - bi * ki
    yi = br * ki + bi * kr
    y2 = jnp.concatenate([yr, yi], axis=0).astype(BF16)
    g2 = jnp.dot(ei_ref[...], y2, preferred_element_type=F32)
    o_ref[...] = g2.reshape(two, r, c).astype(o_ref.dtype)


def _spec_conv(e_fwd, e_inv, a_data, khat):
    pairs, two, s, r, c = a_data.shape
    return pl.pallas_call(
        _spec_conv_kernel,
        grid=(s, pairs),
        in_specs=[pl.BlockSpec((None, two * r, two * r), lambda ks, p: (ks, 0, 0)),
                  pl.BlockSpec((None, two * r, two * r), lambda ks, p: (ks, 0, 0)),
                  pl.BlockSpec((None, two, None, r, c), lambda ks, p: (p, 0, ks, 0, 0)),
                  pl.BlockSpec((None, two, r, c), lambda ks, p: (ks, 0, 0, 0))],
        out_specs=pl.BlockSpec((None, two, None, r, c), lambda ks, p: (p, 0, ks, 0, 0)),
        out_shape=jax.ShapeDtypeStruct((pairs, two, s, r, c), BF16),
        compiler_params=_cparams(("parallel", "arbitrary")),
        name="fft_conv_spec",
    )(e_fwd, e_inv, a_data, khat)


def _fft_constants(seq_len):
    n = 2 * seq_len
    r_, s_ = FFT_R, FFT_S
    assert r_ * s_ == n
    half = s_ // 2
    ks = np.arange(s_)[:, None]
    s = np.arange(s_)[None, :]
    ang = 2.0 * np.pi * ((ks * s) % s_) / s_
    fr, fi = np.cos(ang), -np.sin(ang)
    w1_data = np.block([[fr[:, :half], -fi[:, :half]], [fi[:, :half], fr[:, :half]]])
    w1_filt = np.block([[fr], [fi]])
    so = np.arange(half)[:, None]
    ko = np.arange(s_)[None, :]
    ang_i = 2.0 * np.pi * ((so * ko) % s_) / s_
    cr, ci = np.cos(ang_i), np.sin(ang_i)
    w3 = np.block([[cr, -ci], [ci, cr]]) / n
    ksv = np.arange(s_)[:, None, None]
    kr = np.arange(r_)[None, :, None]
    r = np.arange(r_)[None, None, :]
    ang_e = 2.0 * np.pi * ((r * (ksv + s_ * kr)) % n) / n
    er, ei = np.cos(ang_e), -np.sin(ang_e)
    e_fwd = np.concatenate([np.concatenate([er, -ei], axis=2),
                            np.concatenate([ei, er], axis=2)], axis=1)
    e_inv = np.transpose(e_fwd, (0, 2, 1))
    as_bf16 = lambda a: jnp.asarray(a.astype(np.float32)).astype(BF16)
    return as_bf16(w1_data), as_bf16(w1_filt), as_bf16(w3), as_bf16(e_fwd), as_bf16(e_inv)


def _mm_out_kernel(na_ref, y_ref, vx_ref, x0_ref, db_ref, bna_ref, bhy_ref,
                   w_ref, x_ref, gt_ref, o_ref, m_ref):
    d_na = na_ref.shape[1]

    @pl.when(pl.program_id(1) == 0)
    def _():
        def chunk(rs):
            na = na_ref[rs, :].astype(F32)
            na_n = na * lax.rsqrt(jnp.mean(na * na, axis=-1, keepdims=True) + EPS) * bna_ref[...]
            vx = vx_ref[rs, :].astype(F32)
            hy = (y_ref[rs, :] + vx * db_ref[...]) * x0_ref[rs, :].astype(F32)
            hy_n = hy * lax.rsqrt(jnp.mean(hy * hy, axis=-1, keepdims=True) + EPS) * bhy_ref[...]
            m_ref[rs, 0:d_na] = na_n.astype(BF16)
            m_ref[rs, d_na:] = hy_n.astype(BF16)

        _for_row_chunks(na_ref.shape[0], ROW_CHUNK, chunk)

    acc = jnp.dot(m_ref[...], w_ref[...], preferred_element_type=F32)
    o_ref[...] = x_ref[...] + gt_ref[0] * acc


def _mm_out(na2d, y2d, vx2d, x02d, d_bias, beta_na, beta_hy, w, x2d, gt, seq_len,
            bm=512, bn=1024):
    m, d_na = na2d.shape
    d_hy = y2d.shape[1]
    k, n = w.shape
    blocks_per_seq = seq_len // bm
    return pl.pallas_call(
        _mm_out_kernel,
        grid=(m // bm, n // bn),
        in_specs=[pl.BlockSpec((bm, d_na), lambda i, j: (i, 0)),
                  pl.BlockSpec((bm, d_hy), lambda i, j: (i, 0)),
                  pl.BlockSpec((bm, d_hy), lambda i, j: (i, 0)),
                  pl.BlockSpec((bm, d_hy), lambda i, j: (i, 0)),
                  pl.BlockSpec((1, d_hy), lambda i, j: (0, 0)),
                  pl.BlockSpec((1, d_na), lambda i, j: (0, 0)),
                  pl.BlockSpec((1, d_hy), lambda i, j: (0, 0)),
                  pl.BlockSpec((k, bn), lambda i, j: (0, j)),
                  pl.BlockSpec((bm, bn), lambda i, j: (i, j)),
                  pl.BlockSpec((1, 1, bn), lambda i, j: (i // blocks_per_seq, 0, j))],
        out_specs=pl.BlockSpec((bm, bn), lambda i, j: (i, j)),
        out_shape=jax.ShapeDtypeStruct((m, n), F32),
        scratch_shapes=[pltpu.VMEM((bm, k), BF16)],
        compiler_params=_cparams(("parallel", "arbitrary")),
        name="mm_out",
    )(na2d, y2d, vx2d, x02d, d_bias, beta_na, beta_hy, w, x2d, gt)


def _glu_kernel(a_ref, b_ref, w_ref, cb_ref, o_ref):
    a = _conv3(a_ref[...].astype(F32), w_ref[...], cb_ref[...])
    gelu = 0.5 * a * (1.0 + lax.erf(a * (1.0 / math.sqrt(2.0))))
    o_ref[...] = (gelu * b_ref[...].astype(F32)).astype(o_ref.dtype)


def _glu(u3, conv_w, conv_b, d_ff, cb=256):
    b, l, _ = u3.shape
    per = d_ff // cb
    return pl.pallas_call(
        _glu_kernel,
        grid=(b, per),
        in_specs=[pl.BlockSpec((None, l, cb), lambda bi, j: (bi, 0, j)),
                  pl.BlockSpec((None, l, cb), lambda bi, j: (bi, 0, per + j)),
                  pl.BlockSpec((3, cb), lambda bi, j: (0, j)),
                  pl.BlockSpec((1, cb), lambda bi, j: (0, j))],
        out_specs=pl.BlockSpec((None, l, cb), lambda bi, j: (bi, 0, j)),
        out_shape=jax.ShapeDtypeStruct((b, l, d_ff), BF16),
        compiler_params=_cparams(("parallel", "parallel")),
        name="glu",
    )(u3, u3, conv_w, conv_b)


def _mm_down_kernel(g_ref, w_ref, x_ref, gt_ref, gf_ref, o_ref, acc_ref):
    kk = pl.program_id(1)

    @pl.when(kk == 0)
    def _():
        acc_ref[...] = jnp.zeros_like(acc_ref)

    acc_ref[...] += jnp.dot(g_ref[...], w_ref[...], preferred_element_type=F32)

    @pl.when(kk == pl.num_programs(1) - 1)
    def _():
        def chunk(rs):
            x = x_ref[rs, :] + gt_ref[0] * acc_ref[rs, :]
            ms = jnp.mean(x * x, axis=-1, keepdims=True)
            o_ref[rs, :] = x * lax.rsqrt(ms + EPS) * gf_ref[...]

        _for_row_chunks(x_ref.shape[0], ROW_CHUNK, chunk)


def _mm_down(g2d, w, x2d, gt, g_final, seq_len, bm=1024, bk=512):
    m, k = g2d.shape
    n = w.shape[1]
    blocks_per_seq = seq_len // bm
    return pl.pallas_call(
        _mm_down_kernel,
        grid=(m // bm, k // bk),
        in_specs=[pl.BlockSpec((bm, bk), lambda i, kk: (i, kk)),
                  pl.BlockSpec((bk, n), lambda i, kk: (kk, 0)),
                  pl.BlockSpec((bm, n), lambda i, kk: (i, 0)),
                  pl.BlockSpec((1, 1, n), lambda i, kk: (i // blocks_per_seq, 0, 0)),
                  pl.BlockSpec((1, n), lambda i, kk: (0, 0))],
        out_specs=pl.BlockSpec((bm, n), lambda i, kk: (i, 0)),
        out_shape=jax.ShapeDtypeStruct((m, n), F32),
        scratch_shapes=[pltpu.VMEM((bm, n), F32)],
        compiler_params=_cparams(("parallel", "arbitrary")),
        name="mm_down",
    )(g2d, w, x2d, gt, g_final)


def _position_features(seq_len, kpad):
    t = np.linspace(0.0, 1.0, seq_len)[:, None]
    bands = (FILTER_EMB - 1) // 2
    w = 2.0 * np.pi * np.arange(seq_len)[:, None] / seq_len
    fr = np.linspace(1e-4, bands - 1, bands)[None, :]
    z = np.concatenate([t, np.cos(fr * w), -np.sin(fr * w)], axis=-1)
    z = np.pad(z, ((0, 0), (0, kpad - z.shape[1])))
    idx = (seq_len - np.arange(seq_len)) % seq_len
    return z.astype(np.float32), z[idx].astype(np.float32)


def kernel(x, c, w_ada, b_ada, g_mix, w_in, na_rpb, hy_short_w, hy_short_b,
           hy_filt_w1, hy_filt_b1, hy_filt_w2, hy_filt_b2, hy_filt_w3, hy_filt_b3,
           hy_filt_w4, hy_filt_freq, hy_bias, beta_na, beta_hy, w_out, g_ffn,
           w_up, ffn_conv_w, ffn_conv_b, w_down, g_final):
    b, l, d = x.shape
    depth = w_ada.shape[0]
    d_hy = d - D_NA
    d_ff = w_down.shape[1]
    rows = l // GRID_W
    m = b * l
    assert depth == 1 and b % 2 == 0 and 2 * l == FFT_R * FFT_S

    c_pad = jnp.pad(c, ((0, 8 - b), (0, 0)))
    mod = _adaln(c_pad, w_ada[0], b_ada[0][None, :])[:b]
    sh1, sc1, gt1, sh2, sc2, gt2 = [t[:, None, :] for t in jnp.split(mod, 6, axis=-1)]

    x2d = x.reshape(m, d)

    proj = _mm_norm(x2d, g_mix, sc1, sh1, w_in[0].astype(BF16), l)

    bias_tab = _bias_table(na_rpb[0].reshape(-1))
    na = _na_attention(proj.reshape(b, rows, GRID_W, proj.shape[1]), bias_tab)
    na2d = na.reshape(m, D_NA)

    x0, vx = _hy_pre(proj.reshape(b, l, proj.shape[1]), hy_short_w[0], hy_short_b[0][None, :], d_hy)

    kpad = V7X_LANES
    z_np, zr_np = _position_features(l, kpad)
    w1p = jnp.pad(hy_filt_w1[0], ((0, kpad - FILTER_EMB), (0, 0)))
    max_decay = math.log(DECAY_TARGET) / FAST_DECAY_PCT
    min_decay = math.log(DECAY_TARGET) / SLOW_DECAY_PCT
    deltas_abs = np.abs(np.linspace(min_decay, max_decay, d_hy))[None, :].astype(np.float32)
    taps = _filter_taps(jnp.asarray(z_np), jnp.asarray(zr_np), w1p, hy_filt_b1,
                        hy_filt_w2[0], hy_filt_b2, hy_filt_w3[0], hy_filt_b3,
                        hy_filt_freq, hy_filt_w4[0], jnp.asarray(deltas_abs), d_hy)

    w1_data, w1_filt, w3, e_fwd, e_inv = _fft_constants(l)
    half = FFT_S // 2
    cols = FFT_R * d_hy
    a_filt = _colmm(w1_filt, taps.reshape(1, FFT_S, cols), BF16)
    khat = _spec_filter(e_fwd, a_filt.reshape(1, 2, FFT_S, FFT_R, d_hy))
    a_data = _colmm(w1_data, vx.reshape(b // 2, 2 * half, cols), BF16)
    g_spec = _spec_conv(e_fwd, e_inv, a_data.reshape(b // 2, 2, FFT_S, FFT_R, d_hy), khat)
    y = _colmm(w3, g_spec.reshape(b // 2, 2 * FFT_S, cols), F32)
    y2d = y.reshape(m, d_hy)

    x1 = _mm_out(na2d, y2d, vx.reshape(m, d_hy), x0.reshape(m, d_hy), hy_bias,
                 beta_na, beta_hy, w_out[0].astype(BF16), x2d, gt1, l)

    u = _mm_norm(x1, g_ffn, sc2, sh2, w_up[0].astype(BF16), l)
    gl = _glu(u.reshape(b, l, 2 * d_ff), ffn_conv_w[0], ffn_conv_b[0][None, :], d_ff)
    out = _mm_down(gl.reshape(m, d_ff), w_down[0].astype(BF16), x1, gt2,
                   g_final[None, :], l)
    return out.reshape(b, l, d)
```

```python
import functools
import math

import numpy as np
import jax
import jax.numpy as jnp
from jax import lax
from jax.experimental import pallas as pl
from jax.experimental.pallas import tpu as pltpu

F32 = jnp.float32
BF16 = jnp.bfloat16

GRID_W = 64
NA_HEADS = 16
NA_HEAD_DIM = 64
D_NA = NA_HEADS * NA_HEAD_DIM
NA_WIN_ROWS = 8
NA_WIN_COLS = 16
NA_GROUP = 4
FILTER_EMB = 33
DECAY_TARGET = 1e-2
FAST_DECAY_PCT = 0.3
SLOW_DECAY_PCT = 1.5
EPS = 1e-6
NEG_BIG = -1e30

V7X_LANES = 128
V7X_VMEM_BYTES = 64 * 1024 * 1024
VMEM_LIMIT = 56 * 1024 * 1024

FFT_R = 128
FFT_S = 64

ROW_CHUNK = 64


def _cparams(sem, vmem=VMEM_LIMIT):
    return pltpu.CompilerParams(dimension_semantics=sem, vmem_limit_bytes=vmem)


def _for_row_chunks(n_rows, chunk, fn):
    def body(i, carry):
        fn(pl.ds(pl.multiple_of(i * chunk, chunk), chunk))
        return carry
    lax.fori_loop(0, n_rows // chunk, body, 0)


def _adaln_kernel(c_ref, w_ref, b_ref, o_ref):
    c = c_ref[...]
    cond = c / (1.0 + jnp.exp(-c))
    o_ref[...] = jnp.dot(cond.astype(BF16), w_ref[...].astype(BF16),
                         preferred_element_type=F32) + b_ref[...]


def _adaln(c_pad, w_ada, b_ada, bn=1024):
    rows, d = c_pad.shape
    n = w_ada.shape[1]
    return pl.pallas_call(
        _adaln_kernel,
        grid=(n // bn,),
        in_specs=[pl.BlockSpec((rows, d), lambda j: (0, 0)),
                  pl.BlockSpec((d, bn), lambda j: (0, j)),
                  pl.BlockSpec((1, bn), lambda j: (0, j))],
        out_specs=pl.BlockSpec((rows, bn), lambda j: (0, j)),
        out_shape=jax.ShapeDtypeStruct((rows, n), F32),
        compiler_params=_cparams(("parallel",)),
        name="adaln",
    )(c_pad, w_ada, b_ada)


def _mm_norm_kernel(x_ref, g_ref, sc_ref, sh_ref, w_ref, o_ref, h_ref):
    @pl.when(pl.program_id(1) == 0)
    def _():
        def chunk(rs):
            x = x_ref[rs, :]
            ms = jnp.mean(x * x, axis=-1, keepdims=True)
            y = x * lax.rsqrt(ms + EPS) * g_ref[...]
            h = y * (1.0 + sc_ref[0]) + sh_ref[0]
            h_ref[rs, :] = h.astype(BF16)

        _for_row_chunks(x_ref.shape[0], ROW_CHUNK, chunk)

    o_ref[...] = jnp.dot(h_ref[...], w_ref[...],
                         preferred_element_type=F32).astype(o_ref.dtype)


def _mm_norm(x2d, g, sc, sh, w, seq_len, bm=1024, bn=1024):
    m, k = x2d.shape
    n = w.shape[1]
    blocks_per_seq = seq_len // bm
    return pl.pallas_call(
        _mm_norm_kernel,
        grid=(m // bm, n // bn),
        in_specs=[pl.BlockSpec((bm, k), lambda i, j: (i, 0)),
                  pl.BlockSpec((1, k), lambda i, j: (0, 0)),
                  pl.BlockSpec((1, 1, k), lambda i, j: (i // blocks_per_seq, 0, 0)),
                  pl.BlockSpec((1, 1, k), lambda i, j: (i // blocks_per_seq, 0, 0)),
                  pl.BlockSpec((k, bn), lambda i, j: (0, j))],
        out_specs=pl.BlockSpec((bm, bn), lambda i, j: (i, j)),
        out_shape=jax.ShapeDtypeStruct((m, n), BF16),
        scratch_shapes=[pltpu.VMEM((bm, k), BF16)],
        compiler_params=_cparams(("parallel", "arbitrary")),
        name="mm_norm",
    )(x2d, g, sc, sh, w)


def _bias_kernel(rpb_ref, o_ref):
    h = pl.program_id(0)
    n_rows = 2 * NA_WIN_ROWS - 1
    n_cols = 2 * NA_WIN_COLS - 1
    shape = (GRID_W, 2 * GRID_W)
    lane = lax.broadcasted_iota(jnp.int32, shape, 1)
    cq = lax.broadcasted_iota(jnp.int32, shape, 0)
    ck = lane & (GRID_W - 1)
    first = lane < GRID_W
    cs = jnp.clip(cq - NA_WIN_COLS // 2, 0, GRID_W - NA_WIN_COLS)
    valid = (ck >= cs) & (ck < cs + NA_WIN_COLS)
    d = jnp.clip(ck - cq, -(NA_WIN_COLS - 1), NA_WIN_COLS - 1) + (NA_WIN_COLS - 1)
    pair = []
    for j in range(n_rows - 1):
        base0 = (h * n_rows + j) * n_cols
        base1 = base0 + n_cols
        acc = jnp.zeros(shape, F32)
        for dd in range(n_cols):
            val = jnp.where(first, rpb_ref[base0 + dd], rpb_ref[base1 + dd])
            acc = jnp.where(d == dd, val, acc)
        pair.append(jnp.where(valid, acc, NEG_BIG))
    for w in range(NA_WIN_ROWS):
        for ip in range(NA_WIN_ROWS // 2):
            o_ref[w, :, ip * 2 * GRID_W:(ip + 1) * 2 * GRID_W] = pair[w + 2 * ip]


def _bias_table(rpb_flat):
    return pl.pallas_call(
        _bias_kernel,
        grid=(NA_HEADS,),
        in_specs=[pl.BlockSpec(memory_space=pltpu.SMEM)],
        out_specs=pl.BlockSpec((None, NA_WIN_ROWS, GRID_W, NA_WIN_ROWS * GRID_W),
                               lambda h: (h, 0, 0, 0)),
        out_shape=jax.ShapeDtypeStruct(
            (NA_HEADS, NA_WIN_ROWS, GRID_W, NA_WIN_ROWS * GRID_W), F32),
        compiler_params=_cparams(("parallel",)),
        name="na_bias",
    )(rpb_flat)


def _na_kernel(q_ref, k_ref, v_ref, bias_ref, o_ref):
    scale = NA_HEAD_DIM ** -0.5
    n_keys = k_ref.shape[0]
    gw = NA_GROUP * NA_HEAD_DIM
    lane_head = lax.broadcasted_iota(jnp.int32, (GRID_W, gw), 1) // NA_HEAD_DIM
    for g in range(NA_HEADS // NA_GROUP):
        cs = slice(g * gw, (g + 1) * gw)
        qg = q_ref[:, cs] * scale
        kg = k_ref[:, cs]
        vg = v_ref[:, cs]
        zero = jnp.zeros_like(qg)
        q4 = jnp.concatenate([jnp.where(lane_head == h, qg, zero) for h in range(NA_GROUP)],
                             axis=0)
        s = lax.dot_general(q4, kg, (((1,), (1,)), ((), ())), preferred_element_type=F32)
        s = s + bias_ref[NA_GROUP * g:NA_GROUP * (g + 1)].reshape(NA_GROUP * GRID_W, n_keys)
        m = jnp.max(s, axis=-1, keepdims=True)
        p = jnp.exp(s - m)
        l = jnp.sum(p, axis=-1, keepdims=True)
        o4 = jnp.dot(p.astype(BF16), vg, preferred_element_type=F32) / l
        o = o4[0:GRID_W]
        for h in range(1, NA_GROUP):
            o = jnp.where(lane_head == h, o4[h * GRID_W:(h + 1) * GRID_W], o)
        o_ref[:, cs] = o.astype(o_ref.dtype)


def _na_attention(proj3, bias_tab):
    b, l, _ = proj3.shape
    width = GRID_W
    rows = l // width
    kh = NA_WIN_ROWS
    n_keys = kh * width

    def row_start(r):
        return jnp.clip(r - kh // 2, 0, rows - kh)

    def kv_spec(col0):
        return pl.BlockSpec((None, pl.Element(n_keys), pl.Element(D_NA)),
                            lambda bi, r: (bi, row_start(r) * width, col0))

    return pl.pallas_call(
        _na_kernel,
        grid=(b, rows),
        in_specs=[
            pl.BlockSpec((None, width, D_NA), lambda bi, r: (bi, r, 0)),
            kv_spec(D_NA),
            kv_spec(2 * D_NA),
            pl.BlockSpec((NA_HEADS, None, width, n_keys),
                         lambda bi, r: (0, row_start(r) - r + (kh - 1), 0, 0)),
        ],
        out_specs=pl.BlockSpec((None, width, D_NA), lambda bi, r: (bi, r, 0)),
        out_shape=jax.ShapeDtypeStruct((b, l, D_NA), BF16),
        compiler_params=_cparams(("parallel", "arbitrary")),
        name="na_attn",
    )(proj3, proj3, proj3, bias_tab)


def _conv3(u, w, b):
    n = u.shape[0]
    row = lax.broadcasted_iota(jnp.int32, u.shape, 0)
    prev = jnp.where(row == 0, 0.0, pltpu.roll(u, 1, axis=0))
    nxt = jnp.where(row == n - 1, 0.0, pltpu.roll(u, n - 1, axis=0))
    return prev * w[0:1] + u * w[1:2] + nxt * w[2:3] + b


def _hy_pre_kernel(u0_ref, u1_ref, u2_ref, w0_ref, w1_ref, w2_ref,
                   b0_ref, b1_ref, b2_ref, x0_ref, vx_ref):
    x0 = _conv3(u0_ref[...].astype(F32), w0_ref[...], b0_ref[...])
    x1 = _conv3(u1_ref[...].astype(F32), w1_ref[...], b1_ref[...])
    v = _conv3(u2_ref[...].astype(F32), w2_ref[...], b2_ref[...])
    x0_ref[...] = x0.astype(x0_ref.dtype)
    vx_ref[...] = (v * x1).astype(vx_ref.dtype)


def _hy_pre(proj3, short_w, short_b, d_hy, cb=128):
    b, l, _ = proj3.shape
    base = 3 * D_NA // cb
    per = d_hy // cb

    def u_spec(g):
        return pl.BlockSpec((None, l, cb), lambda bi, j: (bi, 0, base + g * per + j))

    def w_spec(g):
        return pl.BlockSpec((3, cb), lambda bi, j: (0, g * per + j))

    def b_spec(g):
        return pl.BlockSpec((1, cb), lambda bi, j: (0, g * per + j))

    out_spec = pl.BlockSpec((None, l, cb), lambda bi, j: (bi, 0, j))
    return pl.pallas_call(
        _hy_pre_kernel,
        grid=(b, per),
        in_specs=[u_spec(0), u_spec(1), u_spec(2), w_spec(0), w_spec(1), w_spec(2),
                  b_spec(0), b_spec(1), b_spec(2)],
        out_specs=[out_spec, out_spec],
        out_shape=[jax.ShapeDtypeStruct((b, l, d_hy), BF16),
                   jax.ShapeDtypeStruct((b, l, d_hy), BF16)],
        compiler_params=_cparams(("parallel", "parallel")),
        name="hy_pre",
    )(proj3, proj3, proj3, short_w, short_w, short_w, short_b, short_b, short_b)


def _filter_kernel(z_ref, zr_ref, w1_ref, b1_ref, w2_ref, b2_ref, w3_ref, b3_ref,
                   fq_ref, w4f_ref, w4b_ref, dl_ref, o_ref, hf_ref, hb_ref):
    hi = lax.Precision.HIGHEST
    seq = z_ref.shape[0]

    @pl.when(pl.program_id(0) == 0)
    def _():
        fq = fq_ref[...]

        def mlp(z):
            h = jnp.sin(fq * (jnp.dot(z, w1_ref[...], precision=hi,
                                      preferred_element_type=F32) + b1_ref[...]))
            h = jnp.sin(fq * (jnp.dot(h, w2_ref[...], precision=hi,
                                      preferred_element_type=F32) + b2_ref[...]))
            h = jnp.sin(fq * (jnp.dot(h, w3_ref[...], precision=hi,
                                      preferred_element_type=F32) + b3_ref[...]))
            return h

        def chunk(rs):
            hf_ref[rs, :] = mlp(z_ref[rs, :])
            hb_ref[rs, :] = mlp(zr_ref[rs, :])

        _for_row_chunks(seq, 4 * ROW_CHUNK, chunk)

    dl = dl_ref[...]
    t_f = z_ref[:, 0:1]
    t_b = zr_ref[:, 0:1]
    fwd = jnp.dot(hf_ref[...], w4f_ref[...], precision=hi, preferred_element_type=F32)
    fwd = fwd * jnp.exp(-t_f * dl)
    bwd = jnp.dot(hb_ref[...], w4b_ref[...], precision=hi, preferred_element_type=F32)
    bwd = bwd * jnp.exp(-t_b * dl)
    row = lax.broadcasted_iota(jnp.int32, bwd.shape, 0)
    bwd = jnp.where(row == 0, 0.0, bwd)
    o_ref[0:seq, :] = fwd.astype(o_ref.dtype)
    o_ref[seq:2 * seq, :] = bwd.astype(o_ref.dtype)


def _filter_taps(z, z_rev, w1p, b1, w2, b2, w3, b3, freq, w4, deltas_abs, d_hy, cb=256):
    seq, kpad = z.shape
    order = w2.shape[0]
    per = d_hy // cb
    full = lambda shape: pl.BlockSpec(shape, lambda j: tuple(0 for _ in shape))
    return pl.pallas_call(
        _filter_kernel,
        grid=(per,),
        in_specs=[full((seq, kpad)), full((seq, kpad)),
                  full((kpad, order)), full((1, order)),
                  full((order, order)), full((1, order)),
                  full((order, order)), full((1, order)),
                  full((1, order)),
                  pl.BlockSpec((order, cb), lambda j: (0, j)),
                  pl.BlockSpec((order, cb), lambda j: (0, per + j)),
                  pl.BlockSpec((1, cb), lambda j: (0, j))],
        out_specs=pl.BlockSpec((2 * seq, cb), lambda j: (0, j)),
        out_shape=jax.ShapeDtypeStruct((2 * seq, d_hy), BF16),
        scratch_shapes=[pltpu.VMEM((seq, order), F32), pltpu.VMEM((seq, order), F32)],
        compiler_params=_cparams(("arbitrary",)),
        name="hy_filter",
    )(z, z_rev, w1p, b1, w2, b2, w3, b3, freq, w4, w4, deltas_abs)


def _colmm_kernel(w_ref, x_ref, o_ref):
    o_ref[...] = jnp.dot(w_ref[...], x_ref[...],
                         preferred_element_type=F32).astype(o_ref.dtype)


def _colmm(w, x3, out_dtype, bc=8192):
    g, k, cols = x3.shape
    m = w.shape[0]
    return pl.pallas_call(
        _colmm_kernel,
        grid=(g, cols // bc),
        in_specs=[pl.BlockSpec((m, k), lambda gi, j: (0, 0)),
                  pl.BlockSpec((None, k, bc), lambda gi, j: (gi, 0, j))],
        out_specs=pl.BlockSpec((None, m, bc), lambda gi, j: (gi, 0, j)),
        out_shape=jax.ShapeDtypeStruct((g, m, cols), out_dtype),
        compiler_params=_cparams(("parallel", "parallel")),
        name="fft_slow",
    )(w, x3)


def _spec_filter_kernel(e_ref, a_ref, o_ref):
    two, r, c = a_ref.shape
    a2 = a_ref[...].reshape(two * r, c)
    k2 = jnp.dot(e_ref[...], a2, preferred_element_type=F32)
    o_ref[...] = k2.reshape(two, r, c)


def _spec_filter(e_fwd, a_filt):
    _, two, s, r, c = a_filt.shape
    return pl.pallas_call(
        _spec_filter_kernel,
        grid=(s,),
        in_specs=[pl.BlockSpec((None, two * r, two * r), lambda ks: (ks, 0, 0)),
                  pl.BlockSpec((None, two, None, r, c), lambda ks: (0, 0, ks, 0, 0))],
        out_specs=pl.BlockSpec((None, two, r, c), lambda ks: (ks, 0, 0, 0)),
        out_shape=jax.ShapeDtypeStruct((s, two, r, c), F32),
        compiler_params=_cparams(("parallel",)),
        name="fft_filter_spec",
    )(e_fwd, a_filt)


def _spec_conv_kernel(e_ref, ei_ref, a_ref, k_ref, o_ref):
    two, r, c = a_ref.shape
    a2 = a_ref[...].reshape(two * r, c)
    b2 = jnp.dot(e_ref[...], a2, preferred_element_type=F32)
    br, bi = b2[0:r], b2[r:2 * r]
    kr, ki = k_ref[0], k_ref[1]
    yr = br * kr - bi * ki
    yi = br * ki + bi * kr
    y2 = jnp.concatenate([yr, yi], axis=0).astype(BF16)
    g2 = jnp.dot(ei_ref[...], y2, preferred_element_type=F32)
    o_ref[...] = g2.reshape(two, r, c).astype(o_ref.dtype)


def _spec_conv(e_fwd, e_inv, a_data, khat):
    pairs, two, s, r, c = a_data.shape
    return pl.pallas_call(
        _spec_conv_kernel,
        grid=(s, pairs),
        in_specs=[pl.BlockSpec((None, two * r, two * r), lambda ks, p: (ks, 0, 0)),
                  pl.BlockSpec((None, two * r, two * r), lambda ks, p: (ks, 0, 0)),
                  pl.BlockSpec((None, two, None, r, c), lambda ks, p: (p, 0, ks, 0, 0)),
                  pl.BlockSpec((None, two, r, c), lambda ks, p: (ks, 0, 0, 0))],
        out_specs=pl.BlockSpec((None, two, None, r, c), lambda ks, p: (p, 0, ks, 0, 0)),
        out_shape=jax.ShapeDtypeStruct((pairs, two, s, r, c), BF16),
        compiler_params=_cparams(("parallel", "arbitrary")),
        name="fft_conv_spec",
    )(e_fwd, e_inv, a_data, khat)


def _fft_constants(seq_len):
    n = 2 * seq_len
    r_, s_ = FFT_R, FFT_S
    assert r_ * s_ == n
    half = s_ // 2
    ks = np.arange(s_)[:, None]
    s = np.arange(s_)[None, :]
    ang = 2.0 * np.pi * ((ks * s) % s_) / s_
    fr, fi = np.cos(ang), -np.sin(ang)
    w1_data = np.block([[fr[:, :half], -fi[:, :half]], [fi[:, :half], fr[:, :half]]])
    w1_filt = np.block([[fr], [fi]])
    so = np.arange(half)[:, None]
    ko = np.arange(s_)[None, :]
    ang_i = 2.0 * np.pi * ((so * ko) % s_) / s_
    cr, ci = np.cos(ang_i), np.sin(ang_i)
    w3 = np.block([[cr, -ci], [ci, cr]]) / n
    ksv = np.arange(s_)[:, None, None]
    kr = np.arange(r_)[None, :, None]
    r = np.arange(r_)[None, None, :]
    ang_e = 2.0 * np.pi * ((r * (ksv + s_ * kr)) % n) / n
    er, ei = np.cos(ang_e), -np.sin(ang_e)
    e_fwd = np.concatenate([np.concatenate([er, -ei], axis=2),
                            np.concatenate([ei, er], axis=2)], axis=1)
    e_inv = np.transpose(e_fwd, (0, 2, 1))
    as_bf16 = lambda a: jnp.asarray(a.astype(np.float32)).astype(BF16)
    return as_bf16(w1_data), as_bf16(w1_filt), as_bf16(w3), as_bf16(e_fwd), as_bf16(e_inv)


def _mm_out_kernel(na_ref, y_ref, vx_ref, x0_ref, db_ref, bna_ref, bhy_ref,
                   w_ref, x_ref, gt_ref, o_ref, m_ref):
    d_na = na_ref.shape[1]

    @pl.when(pl.program_id(1) == 0)
    def _():
        def chunk(rs):
            na = na_ref[rs, :].astype(F32)
            na_n = na * lax.rsqrt(jnp.mean(na * na, axis=-1, keepdims=True) + EPS) * bna_ref[...]
            vx = vx_ref[rs, :].astype(F32)
            hy = (y_ref[rs, :] + vx * db_ref[...]) * x0_ref[rs, :].astype(F32)
            hy_n = hy * lax.rsqrt(jnp.mean(hy * hy, axis=-1, keepdims=True) + EPS) * bhy_ref[...]
            m_ref[rs, 0:d_na] = na_n.astype(BF16)
            m_ref[rs, d_na:] = hy_n.astype(BF16)

        _for_row_chunks(na_ref.shape[0], ROW_CHUNK, chunk)

    acc = jnp.dot(m_ref[...], w_ref[...], preferred_element_type=F32)
    o_ref[...] = x_ref[...] + gt_ref[0] * acc


def _mm_out(na2d, y2d, vx2d, x02d, d_bias, beta_na, beta_hy, w, x2d, gt, seq_len,
            bm=512, bn=1024):
    m, d_na = na2d.shape
    d_hy = y2d.shape[1]
    k, n = w.shape
    blocks_per_seq = seq_len // bm
    return pl.pallas_call(
        _mm_out_kernel,
        grid=(m // bm, n // bn),
        in_specs=[pl.BlockSpec((bm, d_na), lambda i, j: (i, 0)),
                  pl.BlockSpec((bm, d_hy), lambda i, j: (i, 0)),
                  pl.BlockSpec((bm, d_hy), lambda i, j: (i, 0)),
                  pl.BlockSpec((bm, d_hy), lambda i, j: (i, 0)),
                  pl.BlockSpec((1, d_hy), lambda i, j: (0, 0)),
                  pl.BlockSpec((1, d_na), lambda i, j: (0, 0)),
                  pl.BlockSpec((1, d_hy), lambda i, j: (0, 0)),
                  pl.BlockSpec((k, bn), lambda i, j: (0, j)),
                  pl.BlockSpec((bm, bn), lambda i, j: (i, j)),
                  pl.BlockSpec((1, 1, bn), lambda i, j: (i // blocks_per_seq, 0, j))],
        out_specs=pl.BlockSpec((bm, bn), lambda i, j: (i, j)),
        out_shape=jax.ShapeDtypeStruct((m, n), F32),
        scratch_shapes=[pltpu.VMEM((bm, k), BF16)],
        compiler_params=_cparams(("parallel", "arbitrary")),
        name="mm_out",
    )(na2d, y2d, vx2d, x02d, d_bias, beta_na, beta_hy, w, x2d, gt)


def _glu_kernel(a_ref, b_ref, w_ref, cb_ref, o_ref):
    a = _conv3(a_ref[...].astype(F32), w_ref[...], cb_ref[...])
    gelu = 0.5 * a * (1.0 + lax.erf(a * (1.0 / math.sqrt(2.0))))
    o_ref[...] = (gelu * b_ref[...].astype(F32)).astype(o_ref.dtype)


def _glu(u3, conv_w, conv_b, d_ff, cb=256):
    b, l, _ = u3.shape
    per = d_ff // cb
    return pl.pallas_call(
        _glu_kernel,
        grid=(b, per),
        in_specs=[pl.BlockSpec((None, l, cb), lambda bi, j: (bi, 0, j)),
                  pl.BlockSpec((None, l, cb), lambda bi, j: (bi, 0, per + j)),
                  pl.BlockSpec((3, cb), lambda bi, j: (0, j)),
                  pl.BlockSpec((1, cb), lambda bi, j: (0, j))],
        out_specs=pl.BlockSpec((None, l, cb), lambda bi, j: (bi, 0, j)),
        out_shape=jax.ShapeDtypeStruct((b, l, d_ff), BF16),
        compiler_params=_cparams(("parallel", "parallel")),
        name="glu",
    )(u3, u3, conv_w, conv_b)


def _mm_down_kernel(g_ref, w_ref, x_ref, gt_ref, gf_ref, o_ref, acc_ref):
    kk = pl.program_id(1)

    @pl.when(kk == 0)
    def _():
        acc_ref[...] = jnp.zeros_like(acc_ref)

    acc_ref[...] += jnp.dot(g_ref[...], w_ref[...], preferred_element_type=F32)

    @pl.when(kk == pl.num_programs(1) - 1)
    def _():
        def chunk(rs):
            x = x_ref[rs, :] + gt_ref[0] * acc_ref[rs, :]
            ms = jnp.mean(x * x, axis=-1, keepdims=True)
            o_ref[rs, :] = x * lax.rsqrt(ms + EPS) * gf_ref[...]

        _for_row_chunks(x_ref.shape[0], ROW_CHUNK, chunk)


def _mm_down(g2d, w, x2d, gt, g_final, seq_len, bm=1024, bk=512):
    m, k = g2d.shape
    n = w.shape[1]
    blocks_per_seq = seq_len // bm
    return pl.pallas_call(
        _mm_down_kernel,
        grid=(m // bm, k // bk),
        in_specs=[pl.BlockSpec((bm, bk), lambda i, kk: (i, kk)),
                  pl.BlockSpec((bk, n), lambda i, kk: (kk, 0)),
                  pl.BlockSpec((bm, n), lambda i, kk: (i, 0)),
                  pl.BlockSpec((1, 1, n), lambda i, kk: (i // blocks_per_seq, 0, 0)),
                  pl.BlockSpec((1, n), lambda i, kk: (0, 0))],
        out_specs=pl.BlockSpec((bm, n), lambda i, kk: (i, 0)),
        out_shape=jax.ShapeDtypeStruct((m, n), F32),
        scratch_shapes=[pltpu.VMEM((bm, n), F32)],
        compiler_params=_cparams(("parallel", "arbitrary")),
        name="mm_down",
    )(g2d, w, x2d, gt, g_final)


def _position_features(seq_len, kpad):
    t = np.linspace(0.0, 1.0, seq_len)[:, None]
    bands = (FILTER_EMB - 1) // 2
    w = 2.0 * np.pi * np.arange(seq_len)[:, None] / seq_len
    fr = np.linspace(1e-4, bands - 1, bands)[None, :]
    z = np.concatenate([t, np.cos(fr * w), -np.sin(fr * w)], axis=-1)
    z = np.pad(z, ((0, 0), (0, kpad - z.shape[1])))
    idx = (seq_len - np.arange(seq_len)) % seq_len
    return z.astype(np.float32), z[idx].astype(np.float32)


def kernel(x, c, w_ada, b_ada, g_mix, w_in, na_rpb, hy_short_w, hy_short_b,
           hy_filt_w1, hy_filt_b1, hy_filt_w2, hy_filt_b2, hy_filt_w3, hy_filt_b3,
           hy_filt_w4, hy_filt_freq, hy_bias, beta_na, beta_hy, w_out, g_ffn,
           w_up, ffn_conv_w, ffn_conv_b, w_down, g_final):
    b, l, d = x.shape
    depth = w_ada.shape[0]
    d_hy = d - D_NA
    d_ff = w_down.shape[1]
    rows = l // GRID_W
    m = b * l
    assert depth == 1 and b % 2 == 0 and 2 * l == FFT_R * FFT_S

    c_pad = jnp.pad(c, ((0, 8 - b), (0, 0)))
    mod = _adaln(c_pad, w_ada[0], b_ada[0][None, :])[:b]
    sh1, sc1, gt1, sh2, sc2, gt2 = [t[:, None, :] for t in jnp.split(mod, 6, axis=-1)]

    x2d = x.reshape(m, d)

    proj = _mm_norm(x2d, g_mix, sc1, sh1, w_in[0].astype(BF16), l)

    bias_tab = _bias_table(na_rpb[0].reshape(-1))
    na = _na_attention(proj.reshape(b, l, proj.shape[1]), bias_tab)
    na2d = na.reshape(m, D_NA)

    x0, vx = _hy_pre(proj.reshape(b, l, proj.shape[1]), hy_short_w[0], hy_short_b[0][None, :], d_hy)

    kpad = V7X_LANES
    z_np, zr_np = _position_features(l, kpad)
    w1p = jnp.pad(hy_filt_w1[0], ((0, kpad - FILTER_EMB), (0, 0)))
    max_decay = math.log(DECAY_TARGET) / FAST_DECAY_PCT
    min_decay = math.log(DECAY_TARGET) / SLOW_DECAY_PCT
    deltas_abs = np.abs(np.linspace(min_decay, max_decay, d_hy))[None, :].astype(np.float32)
    taps = _filter_taps(jnp.asarray(z_np), jnp.asarray(zr_np), w1p, hy_filt_b1,
                        hy_filt_w2[0], hy_filt_b2, hy_filt_w3[0], hy_filt_b3,
                        hy_filt_freq, hy_filt_w4[0], jnp.asarray(deltas_abs), d_hy)

    w1_data, w1_filt, w3, e_fwd, e_inv = _fft_constants(l)
    half = FFT_S // 2
    cols = FFT_R * d_hy
    a_filt = _colmm(w1_filt, taps.reshape(1, FFT_S, cols), BF16)
    khat = _spec_filter(e_fwd, a_filt.reshape(1, 2, FFT_S, FFT_R, d_hy))
    a_data = _colmm(w1_data, vx.reshape(b // 2, 2 * half, cols), BF16)
    g_spec = _spec_conv(e_fwd, e_inv, a_data.reshape(b // 2, 2, FFT_S, FFT_R, d_hy), khat)
    y = _colmm(w3, g_spec.reshape(b // 2, 2 * FFT_S, cols), F32)
    y2d = y.reshape(m, d_hy)

    x1 = _mm_out(na2d, y2d, vx.reshape(m, d_hy), x0.reshape(m, d_hy), hy_bias,
                 beta_na, beta_hy, w_out[0].astype(BF16), x2d, gt1, l)

    u = _mm_norm(x1, g_ffn, sc2, sh2, w_up[0].astype(BF16), l)
    gl = _glu(u.reshape(b, l, 2 * d_ff), ffn_conv_w[0], ffn_conv_b[0][None, :], d_ff)
    out = _mm_down(gl.reshape(m, d_ff), w_down[0].astype(BF16), x1, gt2,
                   g_final[None, :], l)
    return out.reshape(b, l, d)
```

```python
import functools
import math

import numpy as np
import jax
import jax.numpy as jnp
from jax import lax
from jax.experimental import pallas as pl
from jax.experimental.pallas import tpu as pltpu

F32 = jnp.float32
BF16 = jnp.bfloat16

GRID_W = 64
NA_HEADS = 16
NA_HEAD_DIM = 64
D_NA = NA_HEADS * NA_HEAD_DIM
NA_WIN_ROWS = 8
NA_WIN_COLS = 16
NA_GROUP = 4
FILTER_EMB = 33
DECAY_TARGET = 1e-2
FAST_DECAY_PCT = 0.3
SLOW_DECAY_PCT = 1.5
EPS = 1e-6
NEG_BIG = -1e30

V7X_LANES = 128
V7X_VMEM_BYTES = 64 * 1024 * 1024
VMEM_LIMIT = 56 * 1024 * 1024

FFT_R = 128
FFT_S = 64

ROW_CHUNK = 64


def _cparams(sem, vmem=VMEM_LIMIT):
    return pltpu.CompilerParams(dimension_semantics=sem, vmem_limit_bytes=vmem)


def _for_row_chunks(n_rows, chunk, fn):
    def body(i, carry):
        fn(pl.ds(pl.multiple_of(i * chunk, chunk), chunk))
        return carry
    lax.fori_loop(0, n_rows // chunk, body, 0)


def _adaln_kernel(c_ref, w_ref, b_ref, o_ref):
    c = c_ref[...]
    cond = c / (1.0 + jnp.exp(-c))
    o_ref[...] = jnp.dot(cond.astype(BF16), w_ref[...].astype(BF16),
                         preferred_element_type=F32) + b_ref[...]


def _adaln(c_pad, w_ada, b_ada, bn=1024):
    rows, d = c_pad.shape
    n = w_ada.shape[1]
    return pl.pallas_call(
        _adaln_kernel,
        grid=(n // bn,),
        in_specs=[pl.BlockSpec((rows, d), lambda j: (0, 0)),
                  pl.BlockSpec((d, bn), lambda j: (0, j)),
                  pl.BlockSpec((1, bn), lambda j: (0, j))],
        out_specs=pl.BlockSpec((rows, bn), lambda j: (0, j)),
        out_shape=jax.ShapeDtypeStruct((rows, n), F32),
        compiler_params=_cparams(("parallel",)),
        name="adaln",
    )(c_pad, w_ada, b_ada)


def _mm_norm_kernel(x_ref, g_ref, sc_ref, sh_ref, w_ref, o_ref, h_ref):
    @pl.when(pl.program_id(1) == 0)
    def _():
        def chunk(rs):
            x = x_ref[rs, :]
            ms = jnp.mean(x * x, axis=-1, keepdims=True)
            y = x * lax.rsqrt(ms + EPS) * g_ref[...]
            h = y * (1.0 + sc_ref[0]) + sh_ref[0]
            h_ref[rs, :] = h.astype(BF16)

        _for_row_chunks(x_ref.shape[0], ROW_CHUNK, chunk)

    o_ref[...] = jnp.dot(h_ref[...], w_ref[...],
                         preferred_element_type=F32).astype(o_ref.dtype)


def _mm_norm(x2d, g, sc, sh, w, seq_len, bm=1024, bn=1024):
    m, k = x2d.shape
    n = w.shape[1]
    blocks_per_seq = seq_len // bm
    return pl.pallas_call(
        _mm_norm_kernel,
        grid=(m // bm, n // bn),
        in_specs=[pl.BlockSpec((bm, k), lambda i, j: (i, 0)),
                  pl.BlockSpec((1, k), lambda i, j: (0, 0)),
                  pl.BlockSpec((1, 1, k), lambda i, j: (i // blocks_per_seq, 0, 0)),
                  pl.BlockSpec((1, 1, k), lambda i, j: (i // blocks_per_seq, 0, 0)),
                  pl.BlockSpec((k, bn), lambda i, j: (0, j))],
        out_specs=pl.BlockSpec((bm, bn), lambda i, j: (i, j)),
        out_shape=jax.ShapeDtypeStruct((m, n), BF16),
        scratch_shapes=[pltpu.VMEM((bm, k), BF16)],
        compiler_params=_cparams(("parallel", "arbitrary")),
        name="mm_norm",
    )(x2d, g, sc, sh, w)


def _bias_kernel(rpb_ref, o_ref):
    h = pl.program_id(0)
    n_rows = 2 * NA_WIN_ROWS - 1
    n_cols = 2 * NA_WIN_COLS - 1
    shape = (GRID_W, 2 * GRID_W)
    lane = lax.broadcasted_iota(jnp.int32, shape, 1)
    cq = lax.broadcasted_iota(jnp.int32, shape, 0)
    ck = lane & (GRID_W - 1)
    first = lane < GRID_W
    cs = jnp.clip(cq - NA_WIN_COLS // 2, 0, GRID_W - NA_WIN_COLS)
    valid = (ck >= cs) & (ck < cs + NA_WIN_COLS)
    d = jnp.clip(ck - cq, -(NA_WIN_COLS - 1), NA_WIN_COLS - 1) + (NA_WIN_COLS - 1)
    pair = []
    for j in range(n_rows - 1):
        base0 = (h * n_rows + j) * n_cols
        base1 = base0 + n_cols
        acc = jnp.zeros(shape, F32)
        for dd in range(n_cols):
            val = jnp.where(first, rpb_ref[base0 + dd], rpb_ref[base1 + dd])
            acc = jnp.where(d == dd, val, acc)
        pair.append(jnp.where(valid, acc, NEG_BIG))
    for w in range(NA_WIN_ROWS):
        for ip in range(NA_WIN_ROWS // 2):
            o_ref[w, :, ip * 2 * GRID_W:(ip + 1) * 2 * GRID_W] = pair[w + 2 * ip]


def _bias_table(rpb_flat):
    return pl.pallas_call(
        _bias_kernel,
        grid=(NA_HEADS,),
        in_specs=[pl.BlockSpec(memory_space=pltpu.SMEM)],
        out_specs=pl.BlockSpec((None, NA_WIN_ROWS, GRID_W, NA_WIN_ROWS * GRID_W),
                               lambda h: (h, 0, 0, 0)),
        out_shape=jax.ShapeDtypeStruct(
            (NA_HEADS, NA_WIN_ROWS, GRID_W, NA_WIN_ROWS * GRID_W), F32),
        compiler_params=_cparams(("parallel",)),
        name="na_bias",
    )(rpb_flat)


def _na_kernel(q_ref, k_ref, v_ref, bias_ref, o_ref):
    scale = NA_HEAD_DIM ** -0.5
    n_keys = k_ref.shape[0]
    gw = NA_GROUP * NA_HEAD_DIM
    lane_head = lax.broadcasted_iota(jnp.int32, (GRID_W, gw), 1) // NA_HEAD_DIM
    for g in range(NA_HEADS // NA_GROUP):
        cs = slice(g * gw, (g + 1) * gw)
        qg = q_ref[:, cs] * scale
        kg = k_ref[:, cs]
        vg = v_ref[:, cs]
        zero = jnp.zeros_like(qg)
        q4 = jnp.concatenate([jnp.where(lane_head == h, qg, zero) for h in range(NA_GROUP)],
                             axis=0)
        s = lax.dot_general(q4, kg, (((1,), (1,)), ((), ())), preferred_element_type=F32)
        s = s + bias_ref[NA_GROUP * g:NA_GROUP * (g + 1)].reshape(NA_GROUP * GRID_W, n_keys)
        m = jnp.max(s, axis=-1, keepdims=True)
        p = jnp.exp(s - m)
        l = jnp.sum(p, axis=-1, keepdims=True)
        o4 = jnp.dot(p.astype(BF16), vg, preferred_element_type=F32) / l
        o = o4[0:GRID_W]
        for h in range(1, NA_GROUP):
            o = jnp.where(lane_head == h, o4[h * GRID_W:(h + 1) * GRID_W], o)
        o_ref[:, cs] = o.astype(o_ref.dtype)


def _na_attention(proj3, bias_tab):
    b, l, _ = proj3.shape
    width = GRID_W
    rows = l // width
    kh = NA_WIN_ROWS
    n_keys = kh * width

    def row_start(r):
        return jnp.clip(r - kh // 2, 0, rows - kh)

    def kv_spec(col0):
        return pl.BlockSpec((None, pl.Element(n_keys), pl.Element(D_NA)),
                            lambda bi, r: (bi, row_start(r) * width, col0))

    return pl.pallas_call(
        _na_kernel,
        grid=(b, rows),
        in_specs=[
            pl.BlockSpec((None, width, D_NA), lambda bi, r: (bi, r, 0)),
            kv_spec(D_NA),
            kv_spec(2 * D_NA),
            pl.BlockSpec((NA_HEADS, None, width, n_keys),
                         lambda bi, r: (0, row_start(r) - r + (kh - 1), 0, 0)),
        ],
        out_specs=pl.BlockSpec((None, width, D_NA), lambda bi, r: (bi, r, 0)),
        out_shape=jax.ShapeDtypeStruct((b, l, D_NA), BF16),
        compiler_params=_cparams(("parallel", "arbitrary")),
        name="na_attn",
    )(proj3, proj3, proj3, bias_tab)


def _conv3(u, w, b):
    n = u.shape[0]
    row = lax.broadcasted_iota(jnp.int32, u.shape, 0)
    prev = jnp.where(row == 0, 0.0, pltpu.roll(u, 1, axis=0))
    nxt = jnp.where(row == n - 1, 0.0, pltpu.roll(u, n - 1, axis=0))
    return prev * w[0:1] + u * w[1:2] + nxt * w[2:3] + b


def _hy_pre_kernel(u0_ref, u1_ref, u2_ref, w0_ref, w1_ref, w2_ref,
                   b0_ref, b1_ref, b2_ref, x0_ref, vx_ref):
    x0 = _conv3(u0_ref[...].astype(F32), w0_ref[...], b0_ref[...])
    x1 = _conv3(u1_ref[...].astype(F32), w1_ref[...], b1_ref[...])
    v = _conv3(u2_ref[...].astype(F32), w2_ref[...], b2_ref[...])
    x0_ref[...] = x0.astype(x0_ref.dtype)
    vx_ref[...] = (v * x1).astype(vx_ref.dtype)


def _hy_pre(proj3, short_w, short_b, d_hy, cb=128):
    b, l, _ = proj3.shape
    base = 3 * D_NA // cb
    per = d_hy // cb

    def u_spec(g):
        return pl.BlockSpec((None, l, cb), lambda bi, j: (bi, 0, base + g * per + j))

    def w_spec(g):
        return pl.BlockSpec((3, cb), lambda bi, j: (0, g * per + j))

    def b_spec(g):
        return pl.BlockSpec((1, cb), lambda bi, j: (0, g * per + j))

    out_spec = pl.BlockSpec((None, l, cb), lambda bi, j: (bi, 0, j))
    return pl.pallas_call(
        _hy_pre_kernel,
        grid=(b, per),
        in_specs=[u_spec(0), u_spec(1), u_spec(2), w_spec(0), w_spec(1), w_spec(2),
                  b_spec(0), b_spec(1), b_spec(2)],
        out_specs=[out_spec, out_spec],
        out_shape=[jax.ShapeDtypeStruct((b, l, d_hy), BF16),
                   jax.ShapeDtypeStruct((b, l, d_hy), BF16)],
        compiler_params=_cparams(("parallel", "parallel")),
        name="hy_pre",
    )(proj3, proj3, proj3, short_w, short_w, short_w, short_b, short_b, short_b)


def _filter_kernel(z_ref, zr_ref, w1_ref, b1_ref, w2_ref, b2_ref, w3_ref, b3_ref,
                   fq_ref, w4f_ref, w4b_ref, dl_ref, o_ref, hf_ref, hb_ref):
    hi = lax.Precision.HIGHEST
    seq = z_ref.shape[0]

    @pl.when(pl.program_id(0) == 0)
    def _():
        fq = fq_ref[...]

        def mlp(z):
            h = jnp.sin(fq * (jnp.dot(z, w1_ref[...], precision=hi,
                                      preferred_element_type=F32) + b1_ref[...]))
            h = jnp.sin(fq * (jnp.dot(h, w2_ref[...], precision=hi,
                                      preferred_element_type=F32) + b2_ref[...]))
            h = jnp.sin(fq * (jnp.dot(h, w3_ref[...], precision=hi,
                                      preferred_element_type=F32) + b3_ref[...]))
            return h

        def chunk(rs):
            hf_ref[rs, :] = mlp(z_ref[rs, :])
            hb_ref[rs, :] = mlp(zr_ref[rs, :])

        _for_row_chunks(seq, 4 * ROW_CHUNK, chunk)

    dl = dl_ref[...]
    t_f = z_ref[:, 0:1]
    t_b = zr_ref[:, 0:1]
    fwd = jnp.dot(hf_ref[...], w4f_ref[...], precision=hi, preferred_element_type=F32)
    fwd = fwd * jnp.exp(-t_f * dl)
    bwd = jnp.dot(hb_ref[...], w4b_ref[...], precision=hi, preferred_element_type=F32)
    bwd = bwd * jnp.exp(-t_b * dl)
    row = lax.broadcasted_iota(jnp.int32, bwd.shape, 0)
    bwd = jnp.where(row == 0, 0.0, bwd)
    o_ref[0:seq, :] = fwd.astype(o_ref.dtype)
    o_ref[seq:2 * seq, :] = bwd.astype(o_ref.dtype)


def _filter_taps(z, z_rev, w1p, b1, w2, b2, w3, b3, freq, w4, deltas_abs, d_hy, cb=256):
    seq, kpad = z.shape
    order = w2.shape[0]
    per = d_hy // cb
    full = lambda shape: pl.BlockSpec(shape, lambda j: tuple(0 for _ in shape))
    return pl.pallas_call(
        _filter_kernel,
        grid=(per,),
        in_specs=[full((seq, kpad)), full((seq, kpad)),
                  full((kpad, order)), full((1, order)),
                  full((order, order)), full((1, order)),
                  full((order, order)), full((1, order)),
                  full((1, order)),
                  pl.BlockSpec((order, cb), lambda j: (0, j)),
                  pl.BlockSpec((order, cb), lambda j: (0, per + j)),
                  pl.BlockSpec((1, cb), lambda j: (0, j))],
        out_specs=pl.BlockSpec((2 * seq, cb), lambda j: (0, j)),
        out_shape=jax.ShapeDtypeStruct((2 * seq, d_hy), BF16),
        scratch_shapes=[pltpu.VMEM((seq, order), F32), pltpu.VMEM((seq, order), F32)],
        compiler_params=_cparams(("arbitrary",)),
        name="hy_filter",
    )(z, z_rev, w1p, b1, w2, b2, w3, b3, freq, w4, w4, deltas_abs)


V7X_SUBLANES = 8


def _fft_slow_kernel(wk_ref, x_ref, o_ref):
    k, rb, c = x_ref.shape
    m = o_ref.shape[0]
    x = x_ref[...].astype(F32)
    outs = []
    for t in range(rb // V7X_SUBLANES):
        xs = x[:, t * V7X_SUBLANES:(t + 1) * V7X_SUBLANES, :]
        xs = xs.reshape(k * V7X_SUBLANES, c).astype(BF16)
        o = jnp.dot(wk_ref[...], xs, preferred_element_type=F32)
        outs.append(o.reshape(m, V7X_SUBLANES, c))
    o_ref[...] = jnp.concatenate(outs, axis=1).astype(o_ref.dtype)


def _fft_slow(wk, x4, out_dtype, rb=16):
    g, k, r, c = x4.shape
    m = wk.shape[0] // V7X_SUBLANES
    assert wk.shape[1] == k * V7X_SUBLANES
    return pl.pallas_call(
        _fft_slow_kernel,
        grid=(g, r // rb),
        in_specs=[pl.BlockSpec(wk.shape, lambda gi, j: (0, 0)),
                  pl.BlockSpec((None, k, rb, c), lambda gi, j: (gi, 0, j, 0))],
        out_specs=pl.BlockSpec((None, m, rb, c), lambda gi, j: (gi, 0, j, 0)),
        out_shape=jax.ShapeDtypeStruct((g, m, r, c), out_dtype),
        compiler_params=_cparams(("parallel", "parallel")),
        name="fft_slow",
    )(wk, x4)


def _fft_spec_kernel(e_ref, ei_ref, af_ref, a_ref, o_ref):
    pairs, two, r, c = a_ref.shape
    e = e_ref[...]
    kf = jnp.dot(e, af_ref[...].reshape(two * r, c), preferred_element_type=F32)
    kr, ki = kf[0:r], kf[r:2 * r]
    for p in range(pairs):
        b2 = jnp.dot(e, a_ref[p].reshape(two * r, c), preferred_element_type=F32)
        br, bi = b2[0:r], b2[r:2 * r]
        y2 = jnp.concatenate([br * kr - bi * ki, br * ki + bi * kr], axis=0).astype(BF16)
        g2 = jnp.dot(ei_ref[...], y2, preferred_element_type=F32)
        o_ref[p] = g2.reshape(two, r, c).astype(o_ref.dtype)


def _fft_spec(e_fwd, e_inv, a_filt, a_data):
    pairs, two, s, r, c = a_data.shape
    return pl.pallas_call(
        _fft_spec_kernel,
        grid=(s,),
        in_specs=[pl.BlockSpec((None, two * r, two * r), lambda ks: (ks, 0, 0)),
                  pl.BlockSpec((None, two * r, two * r), lambda ks: (ks, 0, 0)),
                  pl.BlockSpec((None, two, None, r, c), lambda ks: (0, 0, ks, 0, 0)),
                  pl.BlockSpec((pairs, two, None, r, c), lambda ks: (0, 0, ks, 0, 0))],
        out_specs=pl.BlockSpec((pairs, two, None, r, c), lambda ks: (0, 0, ks, 0, 0)),
        out_shape=jax.ShapeDtypeStruct((pairs, two, s, r, c), BF16),
        compiler_params=_cparams(("parallel",)),
        name="fft_spec",
    )(e_fwd, e_inv, a_filt, a_data)


def _fft_constants(seq_len):
    n = 2 * seq_len
    r_, s_ = FFT_R, FFT_S
    assert r_ * s_ == n
    half = s_ // 2
    ks = np.arange(s_)[:, None]
    s = np.arange(s_)[None, :]
    ang = 2.0 * np.pi * ((ks * s) % s_) / s_
    fr, fi = np.cos(ang), -np.sin(ang)
    w1_data = np.block([[fr[:, :half], -fi[:, :half]], [fi[:, :half], fr[:, :half]]])
    w1_filt = np.block([[fr], [fi]])
    so = np.arange(half)[:, None]
    ko = np.arange(s_)[None, :]
    ang_i = 2.0 * np.pi * ((so * ko) % s_) / s_
    cr, ci = np.cos(ang_i), np.sin(ang_i)
    w3 = np.block([[cr, -ci], [ci, cr]]) / n
    ksv = np.arange(s_)[:, None, None]
    kr = np.arange(r_)[None, :, None]
    r = np.arange(r_)[None, None, :]
    ang_e = 2.0 * np.pi * ((r * (ksv + s_ * kr)) % n) / n
    er, ei = np.cos(ang_e), -np.sin(ang_e)
    e_fwd = np.concatenate([np.concatenate([er, -ei], axis=2),
                            np.concatenate([ei, er], axis=2)], axis=1)
    e_inv = np.transpose(e_fwd, (0, 2, 1))
    as_bf16 = lambda a: jnp.asarray(a.astype(np.float32)).astype(BF16)
    eye = np.eye(V7X_SUBLANES)
    slow = [as_bf16(np.kron(w, eye)) for w in (w1_data, w1_filt, w3)]
    return slow[0], slow[1], slow[2], as_bf16(e_fwd), as_bf16(e_inv)


def _mm_out_kernel(na_ref, y_ref, vx_ref, x0_ref, db_ref, bna_ref, bhy_ref,
                   w_ref, x_ref, gt_ref, o_ref, m_ref):
    d_na = na_ref.shape[1]

    @pl.when(pl.program_id(1) == 0)
    def _():
        def chunk(rs):
            na = na_ref[rs, :].astype(F32)
            na_n = na * lax.rsqrt(jnp.mean(na * na, axis=-1, keepdims=True) + EPS) * bna_ref[...]
            vx = vx_ref[rs, :].astype(F32)
            hy = (y_ref[rs, :] + vx * db_ref[...]) * x0_ref[rs, :].astype(F32)
            hy_n = hy * lax.rsqrt(jnp.mean(hy * hy, axis=-1, keepdims=True) + EPS) * bhy_ref[...]
            m_ref[rs, 0:d_na] = na_n.astype(BF16)
            m_ref[rs, d_na:] = hy_n.astype(BF16)

        _for_row_chunks(na_ref.shape[0], ROW_CHUNK, chunk)

    acc = jnp.dot(m_ref[...], w_ref[...], preferred_element_type=F32)
    o_ref[...] = x_ref[...] + gt_ref[0] * acc


def _mm_out(na2d, y2d, vx2d, x02d, d_bias, beta_na, beta_hy, w, x2d, gt, seq_len,
            bm=512, bn=1024):
    m, d_na = na2d.shape
    d_hy = y2d.shape[1]
    k, n = w.shape
    blocks_per_seq = seq_len // bm
    return pl.pallas_call(
        _mm_out_kernel,
        grid=(m // bm, n // bn),
        in_specs=[pl.BlockSpec((bm, d_na), lambda i, j: (i, 0)),
                  pl.BlockSpec((bm, d_hy), lambda i, j: (i, 0)),
                  pl.BlockSpec((bm, d_hy), lambda i, j: (i, 0)),
                  pl.BlockSpec((bm, d_hy), lambda i, j: (i, 0)),
                  pl.BlockSpec((1, d_hy), lambda i, j: (0, 0)),
                  pl.BlockSpec((1, d_na), lambda i, j: (0, 0)),
                  pl.BlockSpec((1, d_hy), lambda i, j: (0, 0)),
                  pl.BlockSpec((k, bn), lambda i, j: (0, j)),
                  pl.BlockSpec((bm, bn), lambda i, j: (i, j)),
                  pl.BlockSpec((1, 1, bn), lambda i, j: (i // blocks_per_seq, 0, j))],
        out_specs=pl.BlockSpec((bm, bn), lambda i, j: (i, j)),
        out_shape=jax.ShapeDtypeStruct((m, n), F32),
        scratch_shapes=[pltpu.VMEM((bm, k), BF16)],
        compiler_params=_cparams(("parallel", "arbitrary")),
        name="mm_out",
    )(na2d, y2d, vx2d, x02d, d_bias, beta_na, beta_hy, w, x2d, gt)


HALO = 16


def _mm_up_glu_kernel(x_ref, xp_ref, xn_ref, g_ref, sc_ref, sh_ref, wa_ref, wb_ref,
                      cw_ref, cb_ref, o_ref, h_ref, *, blocks_per_seq):
    bm = x_ref.shape[0]
    k = x_ref.shape[1]

    @pl.when(pl.program_id(1) == 0)
    def _():
        pos = pl.program_id(0) % blocks_per_seq
        gs = g_ref[...] * (1.0 + sc_ref[0])
        sh = sh_ref[0]

        def norm(x):
            ms = jnp.mean(x * x, axis=-1, keepdims=True)
            return x * lax.rsqrt(ms + EPS) * gs + sh

        def chunk(rs):
            h_ref[pl.ds(HALO + rs.start, rs.size), :] = norm(x_ref[rs, :]).astype(BF16)

        _for_row_chunks(bm, ROW_CHUNK, chunk)
        zero = jnp.zeros((HALO, k), F32)
        h_ref[0:HALO, :] = jnp.where(pos == 0, zero, norm(xp_ref[...])).astype(BF16)
        h_ref[HALO + bm:, :] = jnp.where(pos == blocks_per_seq - 1, zero,
                                         norm(xn_ref[...])).astype(BF16)

    a = jnp.dot(h_ref[...], wa_ref[...], preferred_element_type=F32)
    b = jnp.dot(h_ref[HALO:HALO + bm, :], wb_ref[...], preferred_element_type=F32)
    n_ext = a.shape[0]
    w = cw_ref[...]
    prev = pltpu.roll(a, 1, axis=0)[HALO:HALO + bm]
    nxt = pltpu.roll(a, n_ext - 1, axis=0)[HALO:HALO + bm]
    ac = prev * w[0:1] + a[HALO:HALO + bm] * w[1:2] + nxt * w[2:3] + cb_ref[...]
    gelu = 0.5 * ac * (1.0 + lax.erf(ac * (1.0 / math.sqrt(2.0))))
    o_ref[...] = (gelu * b).astype(o_ref.dtype)


def _mm_up_glu(x2d, g, sc, sh, w_up, conv_w, conv_b, seq_len, bm=1024, bn=512):
    m, k = x2d.shape
    d_ff = w_up.shape[1] // 2
    nb = d_ff // bn
    blocks_per_seq = seq_len // bm
    hb = bm // HALO
    last = m // HALO - 1
    return pl.pallas_call(
        functools.partial(_mm_up_glu_kernel, blocks_per_seq=blocks_per_seq),
        grid=(m // bm, nb),
        in_specs=[pl.BlockSpec((bm, k), lambda i, j: (i, 0)),
                  pl.BlockSpec((HALO, k), lambda i, j: (jnp.maximum(i * hb - 1, 0), 0)),
                  pl.BlockSpec((HALO, k), lambda i, j: (jnp.minimum((i + 1) * hb, last), 0)),
                  pl.BlockSpec((1, k), lambda i, j: (0, 0)),
                  pl.BlockSpec((1, 1, k), lambda i, j: (i // blocks_per_seq, 0, 0)),
                  pl.BlockSpec((1, 1, k), lambda i, j: (i // blocks_per_seq, 0, 0)),
                  pl.BlockSpec((k, bn), lambda i, j: (0, j)),
                  pl.BlockSpec((k, bn), lambda i, j: (0, nb + j)),
                  pl.BlockSpec((3, bn), lambda i, j: (0, j)),
                  pl.BlockSpec((1, bn), lambda i, j: (0, j))],
        out_specs=pl.BlockSpec((bm, bn), lambda i, j: (i, j)),
        out_shape=jax.ShapeDtypeStruct((m, d_ff), BF16),
        scratch_shapes=[pltpu.VMEM((bm + 2 * HALO, k), BF16)],
        compiler_params=_cparams(("parallel", "arbitrary")),
        name="mm_up_glu",
    )(x2d, x2d, x2d, g, sc, sh, w_up, w_up, conv_w, conv_b)


def _mm_down_kernel(g_ref, w_ref, x_ref, gt_ref, gf_ref, o_ref, acc_ref):
    kk = pl.program_id(1)

    @pl.when(kk == 0)
    def _():
        acc_ref[...] = jnp.zeros_like(acc_ref)

    acc_ref[...] += jnp.dot(g_ref[...], w_ref[...], preferred_element_type=F32)

    @pl.when(kk == pl.num_programs(1) - 1)
    def _():
        def chunk(rs):
            x = x_ref[rs, :] + gt_ref[0] * acc_ref[rs, :]
            ms = jnp.mean(x * x, axis=-1, keepdims=True)
            o_ref[rs, :] = x * lax.rsqrt(ms + EPS) * gf_ref[...]

        _for_row_chunks(x_ref.shape[0], ROW_CHUNK, chunk)


def _mm_down(g2d, w, x2d, gt, g_final, seq_len, bm=1024, bk=512):
    m, k = g2d.shape
    n = w.shape[1]
    blocks_per_seq = seq_len // bm
    return pl.pallas_call(
        _mm_down_kernel,
        grid=(m // bm, k // bk),
        in_specs=[pl.BlockSpec((bm, bk), lambda i, kk: (i, kk)),
                  pl.BlockSpec((bk, n), lambda i, kk: (kk, 0)),
                  pl.BlockSpec((bm, n), lambda i, kk: (i, 0)),
                  pl.BlockSpec((1, 1, n), lambda i, kk: (i // blocks_per_seq, 0, 0)),
                  pl.BlockSpec((1, n), lambda i, kk: (0, 0))],
        out_specs=pl.BlockSpec((bm, n), lambda i, kk: (i, 0)),
        out_shape=jax.ShapeDtypeStruct((m, n), F32),
        scratch_shapes=[pltpu.VMEM((bm, n), F32)],
        compiler_params=_cparams(("parallel", "arbitrary")),
        name="mm_down",
    )(g2d, w, x2d, gt, g_final)


def _position_features(seq_len, kpad):
    t = np.linspace(0.0, 1.0, seq_len)[:, None]
    bands = (FILTER_EMB - 1) // 2
    w = 2.0 * np.pi * np.arange(seq_len)[:, None] / seq_len
    fr = np.linspace(1e-4, bands - 1, bands)[None, :]
    z = np.concatenate([t, np.cos(fr * w), -np.sin(fr * w)], axis=-1)
    z = np.pad(z, ((0, 0), (0, kpad - z.shape[1])))
    idx = (seq_len - np.arange(seq_len)) % seq_len
    return z.astype(np.float32), z[idx].astype(np.float32)


def kernel(x, c, w_ada, b_ada, g_mix, w_in, na_rpb, hy_short_w, hy_short_b,
           hy_filt_w1, hy_filt_b1, hy_filt_w2, hy_filt_b2, hy_filt_w3, hy_filt_b3,
           hy_filt_w4, hy_filt_freq, hy_bias, beta_na, beta_hy, w_out, g_ffn,
           w_up, ffn_conv_w, ffn_conv_b, w_down, g_final):
    b, l, d = x.shape
    depth = w_ada.shape[0]
    d_hy = d - D_NA
    d_ff = w_down.shape[1]
    rows = l // GRID_W
    m = b * l
    assert depth == 1 and b % 2 == 0 and 2 * l == FFT_R * FFT_S

    c_pad = jnp.pad(c, ((0, 8 - b), (0, 0)))
    mod = _adaln(c_pad, w_ada[0], b_ada[0][None, :])[:b]
    sh1, sc1, gt1, sh2, sc2, gt2 = [t[:, None, :] for t in jnp.split(mod, 6, axis=-1)]

    x2d = x.reshape(m, d)

    proj = _mm_norm(x2d, g_mix, sc1, sh1, w_in[0].astype(BF16), l)

    bias_tab = _bias_table(na_rpb[0].reshape(-1))
    na = _na_attention(proj.reshape(b, l, proj.shape[1]), bias_tab)
    na2d = na.reshape(m, D_NA)

    x0, vx = _hy_pre(proj.reshape(b, l, proj.shape[1]), hy_short_w[0], hy_short_b[0][None, :], d_hy)

    kpad = V7X_LANES
    z_np, zr_np = _position_features(l, kpad)
    w1p = jnp.pad(hy_filt_w1[0], ((0, kpad - FILTER_EMB), (0, 0)))
    max_decay = math.log(DECAY_TARGET) / FAST_DECAY_PCT
    min_decay = math.log(DECAY_TARGET) / SLOW_DECAY_PCT
    deltas_abs = np.abs(np.linspace(min_decay, max_decay, d_hy))[None, :].astype(np.float32)
    taps = _filter_taps(jnp.asarray(z_np), jnp.asarray(zr_np), w1p, hy_filt_b1,
                        hy_filt_w2[0], hy_filt_b2, hy_filt_w3[0], hy_filt_b3,
                        hy_filt_freq, hy_filt_w4[0], jnp.asarray(deltas_abs), d_hy)

    w1_data, w1_filt, w3, e_fwd, e_inv = _fft_constants(l)
    a_filt = _fft_slow(w1_filt, taps.reshape(1, FFT_S, FFT_R, d_hy), BF16)
    a_data = _fft_slow(w1_data, vx.reshape(b // 2, FFT_S, FFT_R, d_hy), BF16)
    g_spec = _fft_spec(e_fwd, e_inv, a_filt.reshape(1, 2, FFT_S, FFT_R, d_hy),
                       a_data.reshape(b // 2, 2, FFT_S, FFT_R, d_hy))
    y = _fft_slow(w3, g_spec.reshape(b // 2, 2 * FFT_S, FFT_R, d_hy), F32)
    y2d = y.reshape(m, d_hy)

    x1 = _mm_out(na2d, y2d, vx.reshape(m, d_hy), x0.reshape(m, d_hy), hy_bias,
                 beta_na, beta_hy, w_out[0].astype(BF16), x2d, gt1, l)

    gl = _mm_up_glu(x1, g_ffn, sc2, sh2, w_up[0].astype(BF16), ffn_conv_w[0],
                    ffn_conv_b[0][None, :], l)
    out = _mm_down(gl, w_down[0].astype(BF16), x1, gt2, g_final[None, :], l)
    return out.reshape(b, l, d)
```

```python
import functools
import math

import numpy as np
import jax
import jax.numpy as jnp
from jax import lax
from jax.experimental import pallas as pl
from jax.experimental.pallas import tpu as pltpu

F32 = jnp.float32
BF16 = jnp.bfloat16

GRID_W = 64
NA_HEADS = 16
NA_HEAD_DIM = 64
D_NA = NA_HEADS * NA_HEAD_DIM
NA_WIN_ROWS = 8
NA_WIN_COLS = 16
NA_GROUP = 4
FILTER_EMB = 33
DECAY_TARGET = 1e-2
FAST_DECAY_PCT = 0.3
SLOW_DECAY_PCT = 1.5
EPS = 1e-6
NEG_BIG = -1e30

V7X_LANES = 128
V7X_VMEM_BYTES = 64 * 1024 * 1024
VMEM_LIMIT = 56 * 1024 * 1024

FFT_R = 128
FFT_S = 64

ROW_CHUNK = 64


def _cparams(sem, vmem=VMEM_LIMIT):
    return pltpu.CompilerParams(dimension_semantics=sem, vmem_limit_bytes=vmem)


def _for_row_chunks(n_rows, chunk, fn):
    def body(i, carry):
        fn(pl.ds(pl.multiple_of(i * chunk, chunk), chunk))
        return carry
    lax.fori_loop(0, n_rows // chunk, body, 0)


def _adaln_kernel(c_ref, w_ref, b_ref, o_ref):
    c = c_ref[...]
    cond = c / (1.0 + jnp.exp(-c))
    o_ref[...] = jnp.dot(cond.astype(BF16), w_ref[...].astype(BF16),
                         preferred_element_type=F32) + b_ref[...]


def _adaln(c_pad, w_ada, b_ada, bn=1024):
    rows, d = c_pad.shape
    n = w_ada.shape[1]
    return pl.pallas_call(
        _adaln_kernel,
        grid=(n // bn,),
        in_specs=[pl.BlockSpec((rows, d), lambda j: (0, 0)),
                  pl.BlockSpec((d, bn), lambda j: (0, j)),
                  pl.BlockSpec((1, bn), lambda j: (0, j))],
        out_specs=pl.BlockSpec((rows, bn), lambda j: (0, j)),
        out_shape=jax.ShapeDtypeStruct((rows, n), F32),
        compiler_params=_cparams(("parallel",)),
        name="adaln",
    )(c_pad, w_ada, b_ada)


def _skewed_steps(ha_ref, hb_ref, step):
    t = pl.program_id(0)

    @pl.when(t == 0)
    def _():
        hb_ref[...] = jnp.zeros_like(hb_ref)

    @pl.when(t % 2 == 0)
    def _():
        step(ha_ref, hb_ref)

    @pl.when(t % 2 == 1)
    def _():
        step(hb_ref, ha_ref)


def _resident(shape):
    return pl.BlockSpec(shape, lambda t: tuple(0 for _ in shape), pipeline_mode=pl.Buffered(1))


def _mm_norm_kernel(x_ref, g_ref, sc_ref, sh_ref, w_ref, o_ref, ha_ref, hb_ref):
    bm = x_ref.shape[0]

    def step(h_new, h_cur):
        gs = g_ref[...] * (1.0 + sc_ref[0])
        sh = sh_ref[0]
        for r0 in range(0, bm, ROW_CHUNK):
            x = x_ref[r0:r0 + ROW_CHUNK, :]
            ms = jnp.mean(x * x, axis=-1, keepdims=True)
            h_new[r0:r0 + ROW_CHUNK, :] = (x * lax.rsqrt(ms + EPS) * gs + sh).astype(BF16)
        o_ref[...] = jnp.dot(h_cur[...], w_ref[...],
                             preferred_element_type=F32).astype(o_ref.dtype)

    _skewed_steps(ha_ref, hb_ref, step)


def _mm_norm(x2d, g, sc, sh, w, seq_len, bm=256):
    m, k = x2d.shape
    n = w.shape[1]
    nb = m // bm
    blocks_per_seq = seq_len // bm
    new = lambda t: jnp.minimum(t, nb - 1)
    cur = lambda t: jnp.maximum(t - 1, 0)
    return pl.pallas_call(
        _mm_norm_kernel,
        grid=(nb + 1,),
        in_specs=[pl.BlockSpec((bm, k), lambda t: (new(t), 0)),
                  _resident((1, k)),
                  pl.BlockSpec((1, 1, k), lambda t: (new(t) // blocks_per_seq, 0, 0)),
                  pl.BlockSpec((1, 1, k), lambda t: (new(t) // blocks_per_seq, 0, 0)),
                  _resident((k, n))],
        out_specs=pl.BlockSpec((bm, n), lambda t: (cur(t), 0)),
        out_shape=jax.ShapeDtypeStruct((m, n), BF16),
        scratch_shapes=[pltpu.VMEM((bm, k), BF16), pltpu.VMEM((bm, k), BF16)],
        compiler_params=_cparams(("arbitrary",)),
        name="mm_norm",
    )(x2d, g, sc, sh, w)


def _bias_kernel(rpb_ref, o_ref):
    h = pl.program_id(0)
    n_rows = 2 * NA_WIN_ROWS - 1
    n_cols = 2 * NA_WIN_COLS - 1
    shape = (GRID_W, 2 * GRID_W)
    lane = lax.broadcasted_iota(jnp.int32, shape, 1)
    cq = lax.broadcasted_iota(jnp.int32, shape, 0)
    ck = lane & (GRID_W - 1)
    first = lane < GRID_W
    cs = jnp.clip(cq - NA_WIN_COLS // 2, 0, GRID_W - NA_WIN_COLS)
    valid = (ck >= cs) & (ck < cs + NA_WIN_COLS)
    d = jnp.clip(ck - cq, -(NA_WIN_COLS - 1), NA_WIN_COLS - 1) + (NA_WIN_COLS - 1)
    pair = []
    for j in range(n_rows - 1):
        base0 = (h * n_rows + j) * n_cols
        base1 = base0 + n_cols
        acc = jnp.zeros(shape, F32)
        for dd in range(n_cols):
            val = jnp.where(first, rpb_ref[base0 + dd], rpb_ref[base1 + dd])
            acc = jnp.where(d == dd, val, acc)
        pair.append(jnp.where(valid, acc, NEG_BIG))
    for w in range(NA_WIN_ROWS):
        for ip in range(NA_WIN_ROWS // 2):
            o_ref[w, :, ip * 2 * GRID_W:(ip + 1) * 2 * GRID_W] = pair[w + 2 * ip]


def _bias_table(rpb_flat):
    return pl.pallas_call(
        _bias_kernel,
        grid=(NA_HEADS,),
        in_specs=[pl.BlockSpec(memory_space=pltpu.SMEM)],
        out_specs=pl.BlockSpec((None, NA_WIN_ROWS, GRID_W, NA_WIN_ROWS * GRID_W),
                               lambda h: (h, 0, 0, 0)),
        out_shape=jax.ShapeDtypeStruct(
            (NA_HEADS, NA_WIN_ROWS, GRID_W, NA_WIN_ROWS * GRID_W), F32),
        compiler_params=_cparams(("parallel",)),
        name="na_bias",
    )(rpb_flat)


def _na_kernel(q_ref, k_ref, v_ref, bias_ref, o_ref):
    scale = NA_HEAD_DIM ** -0.5
    n_keys = k_ref.shape[0]
    gw = NA_GROUP * NA_HEAD_DIM
    lane_head = lax.broadcasted_iota(jnp.int32, (GRID_W, gw), 1) // NA_HEAD_DIM
    for g in range(NA_HEADS // NA_GROUP):
        cs = slice(g * gw, (g + 1) * gw)
        qg = q_ref[:, cs] * scale
        kg = k_ref[:, cs]
        vg = v_ref[:, cs]
        zero = jnp.zeros_like(qg)
        q4 = jnp.concatenate([jnp.where(lane_head == h, qg, zero) for h in range(NA_GROUP)],
                             axis=0)
        s = lax.dot_general(q4, kg, (((1,), (1,)), ((), ())), preferred_element_type=F32)
        s = s + bias_ref[NA_GROUP * g:NA_GROUP * (g + 1)].reshape(NA_GROUP * GRID_W, n_keys)
        m = jnp.max(s, axis=-1, keepdims=True)
        p = jnp.exp(s - m)
        l = jnp.sum(p, axis=-1, keepdims=True)
        o4 = jnp.dot(p.astype(BF16), vg, preferred_element_type=F32) / l
        o = o4[0:GRID_W]
        for h in range(1, NA_GROUP):
            o = jnp.where(lane_head == h, o4[h * GRID_W:(h + 1) * GRID_W], o)
        o_ref[:, cs] = o.astype(o_ref.dtype)


def _na_attention(proj3, bias_tab):
    b, l, _ = proj3.shape
    width = GRID_W
    rows = l // width
    kh = NA_WIN_ROWS
    n_keys = kh * width

    def row_start(r):
        return jnp.clip(r - kh // 2, 0, rows - kh)

    def kv_spec(col0):
        return pl.BlockSpec((None, pl.Element(n_keys), pl.Element(D_NA)),
                            lambda bi, r: (bi, row_start(r) * width, col0))

    return pl.pallas_call(
        _na_kernel,
        grid=(b, rows),
        in_specs=[
            pl.BlockSpec((None, width, D_NA), lambda bi, r: (bi, r, 0)),
            kv_spec(D_NA),
            kv_spec(2 * D_NA),
            pl.BlockSpec((NA_HEADS, None, width, n_keys),
                         lambda bi, r: (0, row_start(r) - r + (kh - 1), 0, 0)),
        ],
        out_specs=pl.BlockSpec((None, width, D_NA), lambda bi, r: (bi, r, 0)),
        out_shape=jax.ShapeDtypeStruct((b, l, D_NA), BF16),
        compiler_params=_cparams(("parallel", "arbitrary")),
        name="na_attn",
    )(proj3, proj3, proj3, bias_tab)


def _conv3(u, w, b):
    n = u.shape[0]
    row = lax.broadcasted_iota(jnp.int32, u.shape, 0)
    prev = jnp.where(row == 0, 0.0, pltpu.roll(u, 1, axis=0))
    nxt = jnp.where(row == n - 1, 0.0, pltpu.roll(u, n - 1, axis=0))
    return prev * w[0:1] + u * w[1:2] + nxt * w[2:3] + b


def _hy_pre_kernel(u0_ref, u1_ref, u2_ref, w0_ref, w1_ref, w2_ref,
                   b0_ref, b1_ref, b2_ref, x0_ref, vx_ref):
    x0 = _conv3(u0_ref[...].astype(F32), w0_ref[...], b0_ref[...])
    x1 = _conv3(u1_ref[...].astype(F32), w1_ref[...], b1_ref[...])
    v = _conv3(u2_ref[...].astype(F32), w2_ref[...], b2_ref[...])
    x0_ref[...] = x0.astype(x0_ref.dtype)
    vx_ref[...] = (v * x1).astype(vx_ref.dtype)


def _hy_pre(proj3, short_w, short_b, d_hy, cb=128):
    b, l, _ = proj3.shape
    base = 3 * D_NA // cb
    per = d_hy // cb

    def u_spec(g):
        return pl.BlockSpec((None, l, cb), lambda bi, j: (bi, 0, base + g * per + j))

    def w_spec(g):
        return pl.BlockSpec((3, cb), lambda bi, j: (0, g * per + j))

    def b_spec(g):
        return pl.BlockSpec((1, cb), lambda bi, j: (0, g * per + j))

    out_spec = pl.BlockSpec((None, l, cb), lambda bi, j: (bi, 0, j))
    return pl.pallas_call(
        _hy_pre_kernel,
        grid=(b, per),
        in_specs=[u_spec(0), u_spec(1), u_spec(2), w_spec(0), w_spec(1), w_spec(2),
                  b_spec(0), b_spec(1), b_spec(2)],
        out_specs=[out_spec, out_spec],
        out_shape=[jax.ShapeDtypeStruct((b, l, d_hy), BF16),
                   jax.ShapeDtypeStruct((b, l, d_hy), BF16)],
        compiler_params=_cparams(("parallel", "parallel")),
        name="hy_pre",
    )(proj3, proj3, proj3, short_w, short_w, short_w, short_b, short_b, short_b)


def _filter_kernel(z_ref, zr_ref, w1_ref, b1_ref, w2_ref, b2_ref, w3_ref, b3_ref,
                   fq_ref, w4f_ref, w4b_ref, dl_ref, o_ref, hf_ref, hb_ref):
    hi = lax.Precision.HIGHEST
    seq = z_ref.shape[0]

    @pl.when(pl.program_id(0) == 0)
    def _():
        fq = fq_ref[...]

        def mlp(z):
            h = jnp.sin(fq * (jnp.dot(z, w1_ref[...], precision=hi,
                                      preferred_element_type=F32) + b1_ref[...]))
            h = jnp.sin(fq * (jnp.dot(h, w2_ref[...], precision=hi,
                                      preferred_element_type=F32) + b2_ref[...]))
            h = jnp.sin(fq * (jnp.dot(h, w3_ref[...], precision=hi,
                                      preferred_element_type=F32) + b3_ref[...]))
            return h

        def chunk(rs):
            hf_ref[rs, :] = mlp(z_ref[rs, :])
            hb_ref[rs, :] = mlp(zr_ref[rs, :])

        _for_row_chunks(seq, 4 * ROW_CHUNK, chunk)

    dl = dl_ref[...]
    t_f = z_ref[:, 0:1]
    t_b = zr_ref[:, 0:1]
    fwd = jnp.dot(hf_ref[...], w4f_ref[...], precision=hi, preferred_element_type=F32)
    fwd = fwd * jnp.exp(-t_f * dl)
    bwd = jnp.dot(hb_ref[...], w4b_ref[...], precision=hi, preferred_element_type=F32)
    bwd = bwd * jnp.exp(-t_b * dl)
    row = lax.broadcasted_iota(jnp.int32, bwd.shape, 0)
    bwd = jnp.where(row == 0, 0.0, bwd)
    o_ref[0:seq, :] = fwd.astype(o_ref.dtype)
    o_ref[seq:2 * seq, :] = bwd.astype(o_ref.dtype)


def _filter_taps(z, z_rev, w1p, b1, w2, b2, w3, b3, freq, w4, deltas_abs, d_hy, cb=256):
    seq, kpad = z.shape
    order = w2.shape[0]
    per = d_hy // cb
    full = lambda shape: pl.BlockSpec(shape, lambda j: tuple(0 for _ in shape))
    return pl.pallas_call(
        _filter_kernel,
        grid=(per,),
        in_specs=[full((seq, kpad)), full((seq, kpad)),
                  full((kpad, order)), full((1, order)),
                  full((order, order)), full((1, order)),
                  full((order, order)), full((1, order)),
                  full((1, order)),
                  pl.BlockSpec((order, cb), lambda j: (0, j)),
                  pl.BlockSpec((order, cb), lambda j: (0, per + j)),
                  pl.BlockSpec((1, cb), lambda j: (0, j))],
        out_specs=pl.BlockSpec((2 * seq, cb), lambda j: (0, j)),
        out_shape=jax.ShapeDtypeStruct((2 * seq, d_hy), BF16),
        scratch_shapes=[pltpu.VMEM((seq, order), F32), pltpu.VMEM((seq, order), F32)],
        compiler_params=_cparams(("arbitrary",)),
        name="hy_filter",
    )(z, z_rev, w1p, b1, w2, b2, w3, b3, freq, w4, w4, deltas_abs)


V7X_SUBLANES = 8


def _fft_slow_kernel(wk_ref, x_ref, o_ref):
    k, rb, c = x_ref.shape
    m = o_ref.shape[0]
    x = x_ref[...].astype(F32)
    outs = []
    for t in range(rb // V7X_SUBLANES):
        xs = x[:, t * V7X_SUBLANES:(t + 1) * V7X_SUBLANES, :]
        xs = xs.reshape(k * V7X_SUBLANES, c).astype(BF16)
        o = jnp.dot(wk_ref[...], xs, preferred_element_type=F32)
        outs.append(o.reshape(m, V7X_SUBLANES, c))
    o_ref[...] = jnp.concatenate(outs, axis=1).astype(o_ref.dtype)


def _fft_slow(wk, x4, out_dtype, rb=16):
    g, k, r, c = x4.shape
    m = wk.shape[0] // V7X_SUBLANES
    assert wk.shape[1] == k * V7X_SUBLANES
    return pl.pallas_call(
        _fft_slow_kernel,
        grid=(g, r // rb),
        in_specs=[pl.BlockSpec(wk.shape, lambda gi, j: (0, 0)),
                  pl.BlockSpec((None, k, rb, c), lambda gi, j: (gi, 0, j, 0))],
        out_specs=pl.BlockSpec((None, m, rb, c), lambda gi, j: (gi, 0, j, 0)),
        out_shape=jax.ShapeDtypeStruct((g, m, r, c), out_dtype),
        compiler_params=_cparams(("parallel", "parallel")),
        name="fft_slow",
    )(wk, x4)


def _fft_spec_kernel(e_ref, ei_ref, af_ref, a_ref, o_ref):
    pairs, two, r, c = a_ref.shape
    e = e_ref[...]
    kf = jnp.dot(e, af_ref[...].reshape(two * r, c), preferred_element_type=F32)
    kr, ki = kf[0:r], kf[r:2 * r]
    for p in range(pairs):
        b2 = jnp.dot(e, a_ref[p].reshape(two * r, c), preferred_element_type=F32)
        br, bi = b2[0:r], b2[r:2 * r]
        y2 = jnp.concatenate([br * kr - bi * ki, br * ki + bi * kr], axis=0).astype(BF16)
        g2 = jnp.dot(ei_ref[...], y2, preferred_element_type=F32)
        o_ref[p] = g2.reshape(two, r, c).astype(o_ref.dtype)


def _fft_spec(e_fwd, e_inv, a_filt, a_data):
    pairs, two, s, r, c = a_data.shape
    return pl.pallas_call(
        _fft_spec_kernel,
        grid=(s,),
        in_specs=[pl.BlockSpec((None, two * r, two * r), lambda ks: (ks, 0, 0)),
                  pl.BlockSpec((None, two * r, two * r), lambda ks: (ks, 0, 0)),
                  pl.BlockSpec((None, two, None, r, c), lambda ks: (0, 0, ks, 0, 0)),
                  pl.BlockSpec((pairs, two, None, r, c), lambda ks: (0, 0, ks, 0, 0))],
        out_specs=pl.BlockSpec((pairs, two, None, r, c), lambda ks: (0, 0, ks, 0, 0)),
        out_shape=jax.ShapeDtypeStruct((pairs, two, s, r, c), BF16),
        compiler_params=_cparams(("parallel",)),
        name="fft_spec",
    )(e_fwd, e_inv, a_filt, a_data)


def _fft_constants(seq_len):
    n = 2 * seq_len
    r_, s_ = FFT_R, FFT_S
    assert r_ * s_ == n
    half = s_ // 2
    ks = np.arange(s_)[:, None]
    s = np.arange(s_)[None, :]
    ang = 2.0 * np.pi * ((ks * s) % s_) / s_
    fr, fi = np.cos(ang), -np.sin(ang)
    w1_data = np.block([[fr[:, :half], -fi[:, :half]], [fi[:, :half], fr[:, :half]]])
    w1_filt = np.block([[fr], [fi]])
    so = np.arange(half)[:, None]
    ko = np.arange(s_)[None, :]
    ang_i = 2.0 * np.pi * ((so * ko) % s_) / s_
    cr, ci = np.cos(ang_i), np.sin(ang_i)
    w3 = np.block([[cr, -ci], [ci, cr]]) / n
    ksv = np.arange(s_)[:, None, None]
    kr = np.arange(r_)[None, :, None]
    r = np.arange(r_)[None, None, :]
    ang_e = 2.0 * np.pi * ((r * (ksv + s_ * kr)) % n) / n
    er, ei = np.cos(ang_e), -np.sin(ang_e)
    e_fwd = np.concatenate([np.concatenate([er, -ei], axis=2),
                            np.concatenate([ei, er], axis=2)], axis=1)
    e_inv = np.transpose(e_fwd, (0, 2, 1))
    as_bf16 = lambda a: jnp.asarray(a.astype(np.float32)).astype(BF16)
    eye = np.eye(V7X_SUBLANES)
    slow = [as_bf16(np.kron(w, eye)) for w in (w1_data, w1_filt, w3)]
    return slow[0], slow[1], slow[2], as_bf16(e_fwd), as_bf16(e_inv)


def _mm_out_kernel(na_ref, y_ref, vx_ref, x0_ref, db_ref, bna_ref, bhy_ref,
                   w_ref, x_ref, gt_ref, o_ref, ha_ref, hb_ref):
    bm, d_na = na_ref.shape

    def step(h_new, h_cur):
        for r0 in range(0, bm, ROW_CHUNK):
            rs = slice(r0, r0 + ROW_CHUNK)
            na = na_ref[rs, :].astype(F32)
            na_n = na * lax.rsqrt(jnp.mean(na * na, axis=-1, keepdims=True) + EPS) * bna_ref[...]
            vx = vx_ref[rs, :].astype(F32)
            hy = (y_ref[rs, :] + vx * db_ref[...]) * x0_ref[rs, :].astype(F32)
            hy_n = hy * lax.rsqrt(jnp.mean(hy * hy, axis=-1, keepdims=True) + EPS) * bhy_ref[...]
            h_new[rs, 0:d_na] = na_n.astype(BF16)
            h_new[rs, d_na:] = hy_n.astype(BF16)
        acc = jnp.dot(h_cur[...], w_ref[...], preferred_element_type=F32)
        o_ref[...] = x_ref[...] + gt_ref[0] * acc

    _skewed_steps(ha_ref, hb_ref, step)


def _mm_out(na2d, y2d, vx2d, x02d, d_bias, beta_na, beta_hy, w, x2d, gt, seq_len, bm=256):
    m, d_na = na2d.shape
    d_hy = y2d.shape[1]
    k, n = w.shape
    nb = m // bm
    blocks_per_seq = seq_len // bm
    new = lambda t: jnp.minimum(t, nb - 1)
    cur = lambda t: jnp.maximum(t - 1, 0)
    return pl.pallas_call(
        _mm_out_kernel,
        grid=(nb + 1,),
        in_specs=[pl.BlockSpec((bm, d_na), lambda t: (new(t), 0)),
                  pl.BlockSpec((bm, d_hy), lambda t: (new(t), 0)),
                  pl.BlockSpec((bm, d_hy), lambda t: (new(t), 0)),
                  pl.BlockSpec((bm, d_hy), lambda t: (new(t), 0)),
                  _resident((1, d_hy)),
                  _resident((1, d_na)),
                  _resident((1, d_hy)),
                  _resident((k, n)),
                  pl.BlockSpec((bm, n), lambda t: (cur(t), 0)),
                  pl.BlockSpec((1, 1, n), lambda t: (cur(t) // blocks_per_seq, 0, 0))],
        out_specs=pl.BlockSpec((bm, n), lambda t: (cur(t), 0)),
        out_shape=jax.ShapeDtypeStruct((m, n), F32),
        scratch_shapes=[pltpu.VMEM((bm, k), BF16), pltpu.VMEM((bm, k), BF16)],
        compiler_params=_cparams(("arbitrary",)),
        name="mm_out",
    )(na2d, y2d, vx2d, x02d, d_bias, beta_na, beta_hy, w, x2d, gt)


HALO = 16


def _mm_up_glu_kernel(x_ref, xp_ref, xn_ref, g_ref, sc_ref, sh_ref, wa_ref, wb_ref,
                      cw_ref, cb_ref, o_ref, h_ref, *, blocks_per_seq):
    bm = x_ref.shape[0]
    k = x_ref.shape[1]

    @pl.when(pl.program_id(1) == 0)
    def _():
        pos = pl.program_id(0) % blocks_per_seq
        gs = g_ref[...] * (1.0 + sc_ref[0])
        sh = sh_ref[0]

        def norm(x):
            ms = jnp.mean(x * x, axis=-1, keepdims=True)
            return x * lax.rsqrt(ms + EPS) * gs + sh

        def chunk(rs):
            h_ref[pl.ds(HALO + rs.start, rs.size), :] = norm(x_ref[rs, :]).astype(BF16)

        _for_row_chunks(bm, ROW_CHUNK, chunk)
        zero = jnp.zeros((HALO, k), F32)
        h_ref[0:HALO, :] = jnp.where(pos == 0, zero, norm(xp_ref[...])).astype(BF16)
        h_ref[HALO + bm:, :] = jnp.where(pos == blocks_per_seq - 1, zero,
                                         norm(xn_ref[...])).astype(BF16)

    a = jnp.dot(h_ref[...], wa_ref[...], preferred_element_type=F32)
    b = jnp.dot(h_ref[HALO:HALO + bm, :], wb_ref[...], preferred_element_type=F32)
    n_ext = a.shape[0]
    w = cw_ref[...]
    prev = pltpu.roll(a, 1, axis=0)[HALO:HALO + bm]
    nxt = pltpu.roll(a, n_ext - 1, axis=0)[HALO:HALO + bm]
    ac = prev * w[0:1] + a[HALO:HALO + bm] * w[1:2] + nxt * w[2:3] + cb_ref[...]
    gelu = 0.5 * ac * (1.0 + lax.erf(ac * (1.0 / math.sqrt(2.0))))
    o_ref[...] = (gelu * b).astype(o_ref.dtype)


def _mm_up_glu(x2d, g, sc, sh, w_up, conv_w, conv_b, seq_len, bm=1024, bn=512):
    m, k = x2d.shape
    d_ff = w_up.shape[1] // 2
    nb = d_ff // bn
    blocks_per_seq = seq_len // bm
    hb = bm // HALO
    last = m // HALO - 1
    return pl.pallas_call(
        functools.partial(_mm_up_glu_kernel, blocks_per_seq=blocks_per_seq),
        grid=(m // bm, nb),
        in_specs=[pl.BlockSpec((bm, k), lambda i, j: (i, 0)),
                  pl.BlockSpec((HALO, k), lambda i, j: (jnp.maximum(i * hb - 1, 0), 0)),
                  pl.BlockSpec((HALO, k), lambda i, j: (jnp.minimum((i + 1) * hb, last), 0)),
                  pl.BlockSpec((1, k), lambda i, j: (0, 0)),
                  pl.BlockSpec((1, 1, k), lambda i, j: (i // blocks_per_seq, 0, 0)),
                  pl.BlockSpec((1, 1, k), lambda i, j: (i // blocks_per_seq, 0, 0)),
                  pl.BlockSpec((k, bn), lambda i, j: (0, j)),
                  pl.BlockSpec((k, bn), lambda i, j: (0, nb + j)),
                  pl.BlockSpec((3, bn), lambda i, j: (0, j)),
                  pl.BlockSpec((1, bn), lambda i, j: (0, j))],
        out_specs=pl.BlockSpec((bm, bn), lambda i, j: (i, j)),
        out_shape=jax.ShapeDtypeStruct((m, d_ff), BF16),
        scratch_shapes=[pltpu.VMEM((bm + 2 * HALO, k), BF16)],
        compiler_params=_cparams(("parallel", "arbitrary")),
        name="mm_up_glu",
    )(x2d, x2d, x2d, g, sc, sh, w_up, w_up, conv_w, conv_b)


def _mm_down_kernel(g_ref, w_ref, x_ref, gt_ref, gf_ref, o_ref):
    acc = jnp.dot(g_ref[...], w_ref[...], preferred_element_type=F32)
    x = x_ref[...] + gt_ref[0] * acc
    ms = jnp.mean(x * x, axis=-1, keepdims=True)
    o_ref[...] = x * lax.rsqrt(ms + EPS) * gf_ref[...]


def _mm_down(g2d, w, x2d, gt, g_final, seq_len, bm=256):
    m, k = g2d.shape
    n = w.shape[1]
    blocks_per_seq = seq_len // bm
    return pl.pallas_call(
        _mm_down_kernel,
        grid=(m // bm,),
        in_specs=[pl.BlockSpec((bm, k), lambda i: (i, 0)),
                  _resident((k, n)),
                  pl.BlockSpec((bm, n), lambda i: (i, 0)),
                  pl.BlockSpec((1, 1, n), lambda i: (i // blocks_per_seq, 0, 0)),
                  _resident((1, n))],
        out_specs=pl.BlockSpec((bm, n), lambda i: (i, 0)),
        out_shape=jax.ShapeDtypeStruct((m, n), F32),
        compiler_params=_cparams(("parallel",)),
        name="mm_down",
    )(g2d, w, x2d, gt, g_final)


def _position_features(seq_len, kpad):
    t = np.linspace(0.0, 1.0, seq_len)[:, None]
    bands = (FILTER_EMB - 1) // 2
    w = 2.0 * np.pi * np.arange(seq_len)[:, None] / seq_len
    fr = np.linspace(1e-4, bands - 1, bands)[None, :]
    z = np.concatenate([t, np.cos(fr * w), -np.sin(fr * w)], axis=-1)
    z = np.pad(z, ((0, 0), (0, kpad - z.shape[1])))
    idx = (seq_len - np.arange(seq_len)) % seq_len
    return z.astype(np.float32), z[idx].astype(np.float32)


def kernel(x, c, w_ada, b_ada, g_mix, w_in, na_rpb, hy_short_w, hy_short_b,
           hy_filt_w1, hy_filt_b1, hy_filt_w2, hy_filt_b2, hy_filt_w3, hy_filt_b3,
           hy_filt_w4, hy_filt_freq, hy_bias, beta_na, beta_hy, w_out, g_ffn,
           w_up, ffn_conv_w, ffn_conv_b, w_down, g_final):
    b, l, d = x.shape
    depth = w_ada.shape[0]
    d_hy = d - D_NA
    d_ff = w_down.shape[1]
    rows = l // GRID_W
    m = b * l
    assert depth == 1 and b % 2 == 0 and 2 * l == FFT_R * FFT_S

    c_pad = jnp.pad(c, ((0, 8 - b), (0, 0)))
    mod = _adaln(c_pad, w_ada[0], b_ada[0][None, :])[:b]
    sh1, sc1, gt1, sh2, sc2, gt2 = [t[:, None, :] for t in jnp.split(mod, 6, axis=-1)]

    x2d = x.reshape(m, d)

    proj = _mm_norm(x2d, g_mix, sc1, sh1, w_in[0].astype(BF16), l)

    bias_tab = _bias_table(na_rpb[0].reshape(-1))
    na = _na_attention(proj.reshape(b, l, proj.shape[1]), bias_tab)
    na2d = na.reshape(m, D_NA)

    x0, vx = _hy_pre(proj.reshape(b, l, proj.shape[1]), hy_short_w[0], hy_short_b[0][None, :], d_hy)

    kpad = V7X_LANES
    z_np, zr_np = _position_features(l, kpad)
    w1p = jnp.pad(hy_filt_w1[0], ((0, kpad - FILTER_EMB), (0, 0)))
    max_decay = math.log(DECAY_TARGET) / FAST_DECAY_PCT
    min_decay = math.log(DECAY_TARGET) / SLOW_DECAY_PCT
    deltas_abs = np.abs(np.linspace(min_decay, max_decay, d_hy))[None, :].astype(np.float32)
    taps = _filter_taps(jnp.asarray(z_np), jnp.asarray(zr_np), w1p, hy_filt_b1,
                        hy_filt_w2[0], hy_filt_b2, hy_filt_w3[0], hy_filt_b3,
                        hy_filt_freq, hy_filt_w4[0], jnp.asarray(deltas_abs), d_hy)

    w1_data, w1_filt, w3, e_fwd, e_inv = _fft_constants(l)
    a_filt = _fft_slow(w1_filt, taps.reshape(1, FFT_S, FFT_R, d_hy), BF16)
    a_data = _fft_slow(w1_data, vx.reshape(b // 2, FFT_S, FFT_R, d_hy), BF16)
    g_spec = _fft_spec(e_fwd, e_inv, a_filt.reshape(1, 2, FFT_S, FFT_R, d_hy),
                       a_data.reshape(b // 2, 2, FFT_S, FFT_R, d_hy))
    y = _fft_slow(w3, g_spec.reshape(b // 2, 2 * FFT_S, FFT_R, d_hy), F32)
    y2d = y.reshape(m, d_hy)

    x1 = _mm_out(na2d, y2d, vx.reshape(m, d_hy), x0.reshape(m, d_hy), hy_bias,
                 beta_na, beta_hy, w_out[0].astype(BF16), x2d, gt1, l)

    gl = _mm_up_glu(x1, g_ffn, sc2, sh2, w_up[0].astype(BF16), ffn_conv_w[0],
                    ffn_conv_b[0][None, :], l)
    out = _mm_down(gl, w_down[0].astype(BF16), x1, gt2, g_final[None, :], l)
    return out.reshape(b, l, d)
```

```python
import functools
import math

import numpy as np
import jax
import jax.numpy as jnp
from jax import lax
from jax.experimental import pallas as pl
from jax.experimental.pallas import tpu as pltpu

F32 = jnp.float32
BF16 = jnp.bfloat16

GRID_W = 64
NA_HEADS = 16
NA_HEAD_DIM = 64
D_NA = NA_HEADS * NA_HEAD_DIM
NA_WIN_ROWS = 8
NA_WIN_COLS = 16
NA_GROUP = 4
FILTER_EMB = 33
DECAY_TARGET = 1e-2
FAST_DECAY_PCT = 0.3
SLOW_DECAY_PCT = 1.5
EPS = 1e-6
NEG_BIG = -1e30

V7X_LANES = 128
V7X_VMEM_BYTES = 64 * 1024 * 1024
VMEM_LIMIT = 56 * 1024 * 1024

FFT_R = 128
FFT_S = 64

ROW_CHUNK = 64


def _cparams(sem, vmem=VMEM_LIMIT):
    return pltpu.CompilerParams(dimension_semantics=sem, vmem_limit_bytes=vmem)


def _for_row_chunks(n_rows, chunk, fn):
    def body(i, carry):
        fn(pl.ds(pl.multiple_of(i * chunk, chunk), chunk))
        return carry
    lax.fori_loop(0, n_rows // chunk, body, 0)


def _adaln_kernel(c_ref, w_ref, b_ref, o_ref):
    c = c_ref[...]
    cond = c / (1.0 + jnp.exp(-c))
    o_ref[...] = jnp.dot(cond.astype(BF16), w_ref[...].astype(BF16),
                         preferred_element_type=F32) + b_ref[...]


def _adaln(c_pad, w_ada, b_ada, bn=1024):
    rows, d = c_pad.shape
    n = w_ada.shape[1]
    return pl.pallas_call(
        _adaln_kernel,
        grid=(n // bn,),
        in_specs=[pl.BlockSpec((rows, d), lambda j: (0, 0)),
                  pl.BlockSpec((d, bn), lambda j: (0, j)),
                  pl.BlockSpec((1, bn), lambda j: (0, j))],
        out_specs=pl.BlockSpec((rows, bn), lambda j: (0, j)),
        out_shape=jax.ShapeDtypeStruct((rows, n), F32),
        compiler_params=_cparams(("parallel",)),
        name="adaln",
    )(c_pad, w_ada, b_ada)


def _skewed_steps(ha_ref, hb_ref, step):
    t = pl.program_id(0)

    @pl.when(t == 0)
    def _():
        hb_ref[...] = jnp.zeros_like(hb_ref)

    @pl.when(t % 2 == 0)
    def _():
        step(ha_ref, hb_ref)

    @pl.when(t % 2 == 1)
    def _():
        step(hb_ref, ha_ref)


def _resident(shape):
    return pl.BlockSpec(shape, lambda t: tuple(0 for _ in shape), pipeline_mode=pl.Buffered(1))


def _mm_norm_kernel(x_ref, g_ref, sc_ref, sh_ref, w_ref, o_ref, ha_ref, hb_ref):
    bm = x_ref.shape[0]

    def step(h_new, h_cur):
        gs = g_ref[...] * (1.0 + sc_ref[0])
        sh = sh_ref[0]
        for r0 in range(0, bm, ROW_CHUNK):
            x = x_ref[r0:r0 + ROW_CHUNK, :]
            ms = jnp.mean(x * x, axis=-1, keepdims=True)
            h_new[r0:r0 + ROW_CHUNK, :] = (x * lax.rsqrt(ms + EPS) * gs + sh).astype(BF16)
        o_ref[...] = jnp.dot(h_cur[...], w_ref[...],
                             preferred_element_type=F32).astype(o_ref.dtype)

    _skewed_steps(ha_ref, hb_ref, step)


def _mm_norm(x2d, g, sc, sh, w, seq_len, bm=256):
    m, k = x2d.shape
    n = w.shape[1]
    nb = m // bm
    blocks_per_seq = seq_len // bm
    new = lambda t: jnp.minimum(t, nb - 1)
    cur = lambda t: jnp.maximum(t - 1, 0)
    return pl.pallas_call(
        _mm_norm_kernel,
        grid=(nb + 1,),
        in_specs=[pl.BlockSpec((bm, k), lambda t: (new(t), 0)),
                  _resident((1, k)),
                  pl.BlockSpec((1, 1, k), lambda t: (new(t) // blocks_per_seq, 0, 0)),
                  pl.BlockSpec((1, 1, k), lambda t: (new(t) // blocks_per_seq, 0, 0)),
                  _resident((k, n))],
        out_specs=pl.BlockSpec((bm, n), lambda t: (cur(t), 0)),
        out_shape=jax.ShapeDtypeStruct((m, n), BF16),
        scratch_shapes=[pltpu.VMEM((bm, k), BF16), pltpu.VMEM((bm, k), BF16)],
        compiler_params=_cparams(("arbitrary",)),
        name="mm_norm",
    )(x2d, g, sc, sh, w)


def _bias_kernel(rpb_ref, o_ref):
    h = pl.program_id(0)
    n_rows = 2 * NA_WIN_ROWS - 1
    n_cols = 2 * NA_WIN_COLS - 1
    shape = (GRID_W, 2 * GRID_W)
    lane = lax.broadcasted_iota(jnp.int32, shape, 1)
    cq = lax.broadcasted_iota(jnp.int32, shape, 0)
    ck = lane & (GRID_W - 1)
    first = lane < GRID_W
    cs = jnp.clip(cq - NA_WIN_COLS // 2, 0, GRID_W - NA_WIN_COLS)
    valid = (ck >= cs) & (ck < cs + NA_WIN_COLS)
    d = jnp.clip(ck - cq, -(NA_WIN_COLS - 1), NA_WIN_COLS - 1) + (NA_WIN_COLS - 1)
    pair = []
    for j in range(n_rows - 1):
        base0 = (h * n_rows + j) * n_cols
        base1 = base0 + n_cols
        acc = jnp.zeros(shape, F32)
        for dd in range(n_cols):
            val = jnp.where(first, rpb_ref[base0 + dd], rpb_ref[base1 + dd])
            acc = jnp.where(d == dd, val, acc)
        pair.append(jnp.where(valid, acc, NEG_BIG))
    for w in range(NA_WIN_ROWS):
        for ip in range(NA_WIN_ROWS // 2):
            o_ref[w, :, ip * 2 * GRID_W:(ip + 1) * 2 * GRID_W] = pair[w + 2 * ip]


def _bias_table(rpb_flat):
    return pl.pallas_call(
        _bias_kernel,
        grid=(NA_HEADS,),
        in_specs=[pl.BlockSpec(memory_space=pltpu.SMEM)],
        out_specs=pl.BlockSpec((None, NA_WIN_ROWS, GRID_W, NA_WIN_ROWS * GRID_W),
                               lambda h: (h, 0, 0, 0)),
        out_shape=jax.ShapeDtypeStruct(
            (NA_HEADS, NA_WIN_ROWS, GRID_W, NA_WIN_ROWS * GRID_W), F32),
        compiler_params=_cparams(("parallel",)),
        name="na_bias",
    )(rpb_flat)


NA_ROWS_PER_STEP = 4


def _na_row_start(r, rows):
    return jnp.clip(r - NA_WIN_ROWS // 2, 0, rows - NA_WIN_ROWS)


def _na_window_start(r0, rows):
    span = NA_WIN_ROWS + NA_ROWS_PER_STEP - 1
    return jnp.clip(r0 - NA_WIN_ROWS // 2, 0, rows - span)


def _na_kernel(q_ref, k_ref, v_ref, bias_ref, o_ref, *, rows):
    scale = NA_HEAD_DIM ** -0.5
    n_keys = NA_WIN_ROWS * GRID_W
    gw = NA_GROUP * NA_HEAD_DIM
    lane_head = lax.broadcasted_iota(jnp.int32, (GRID_W, gw), 1) // NA_HEAD_DIM
    r0 = pl.program_id(1) * NA_ROWS_PER_STEP
    win0 = _na_window_start(r0, rows)
    for j in range(NA_ROWS_PER_STEP):
        r = r0 + j
        rs = _na_row_start(r, rows)
        ks = pl.ds(pl.multiple_of((rs - win0) * GRID_W, GRID_W), n_keys)
        w = rs - r + (NA_WIN_ROWS - 1)
        qs = slice(j * GRID_W, (j + 1) * GRID_W)
        for g in range(NA_HEADS // NA_GROUP):
            cs = slice(g * gw, (g + 1) * gw)
            qg = q_ref[qs, cs] * scale
            kg = k_ref[ks, cs]
            vg = v_ref[ks, cs]
            zero = jnp.zeros_like(qg)
            q4 = jnp.concatenate(
                [jnp.where(lane_head == h, qg, zero) for h in range(NA_GROUP)], axis=0)
            s = lax.dot_general(q4, kg, (((1,), (1,)), ((), ())), preferred_element_type=F32)
            bias = bias_ref[NA_GROUP * g:NA_GROUP * (g + 1), w]
            s = s + bias.reshape(NA_GROUP * GRID_W, n_keys)
            m = jnp.max(s, axis=-1, keepdims=True)
            p = jnp.exp(s - m)
            l = jnp.sum(p, axis=-1, keepdims=True)
            o4 = jnp.dot(p.astype(BF16), vg, preferred_element_type=F32) / l
            o = o4[0:GRID_W]
            for h in range(1, NA_GROUP):
                o = jnp.where(lane_head == h, o4[h * GRID_W:(h + 1) * GRID_W], o)
            o_ref[qs, cs] = o.astype(o_ref.dtype)


def _na_attention(proj3, bias_tab):
    b, l, _ = proj3.shape
    rows = l // GRID_W
    rb = NA_ROWS_PER_STEP
    span = NA_WIN_ROWS + rb - 1

    def kv_spec(col0):
        return pl.BlockSpec((None, pl.Element(span * GRID_W), pl.Element(D_NA)),
                            lambda bi, t: (bi, _na_window_start(t * rb, rows) * GRID_W, col0))

    return pl.pallas_call(
        functools.partial(_na_kernel, rows=rows),
        grid=(b, rows // rb),
        in_specs=[
            pl.BlockSpec((None, rb * GRID_W, D_NA), lambda bi, t: (bi, t, 0)),
            kv_spec(D_NA),
            kv_spec(2 * D_NA),
            pl.BlockSpec(bias_tab.shape, lambda bi, t: (0, 0, 0, 0),
                         pipeline_mode=pl.Buffered(1)),
        ],
        out_specs=pl.BlockSpec((None, rb * GRID_W, D_NA), lambda bi, t: (bi, t, 0)),
        out_shape=jax.ShapeDtypeStruct((b, l, D_NA), BF16),
        compiler_params=_cparams(("parallel", "arbitrary")),
        name="na_attn",
    )(proj3, proj3, proj3, bias_tab)


def _conv3(u, w, b):
    n = u.shape[0]
    row = lax.broadcasted_iota(jnp.int32, u.shape, 0)
    prev = jnp.where(row == 0, 0.0, pltpu.roll(u, 1, axis=0))
    nxt = jnp.where(row == n - 1, 0.0, pltpu.roll(u, n - 1, axis=0))
    return prev * w[0:1] + u * w[1:2] + nxt * w[2:3] + b


def _hy_pre_kernel(u0_ref, u1_ref, u2_ref, w0_ref, w1_ref, w2_ref,
                   b0_ref, b1_ref, b2_ref, x0_ref, vx_ref):
    x0 = _conv3(u0_ref[...].astype(F32), w0_ref[...], b0_ref[...])
    x1 = _conv3(u1_ref[...].astype(F32), w1_ref[...], b1_ref[...])
    v = _conv3(u2_ref[...].astype(F32), w2_ref[...], b2_ref[...])
    x0_ref[...] = x0.astype(x0_ref.dtype)
    vx_ref[...] = (v * x1).astype(vx_ref.dtype)


def _hy_pre(proj3, short_w, short_b, d_hy, cb=128):
    b, l, _ = proj3.shape
    base = 3 * D_NA // cb
    per = d_hy // cb

    def u_spec(g):
        return pl.BlockSpec((None, l, cb), lambda bi, j: (bi, 0, base + g * per + j))

    def w_spec(g):
        return pl.BlockSpec((3, cb), lambda bi, j: (0, g * per + j))

    def b_spec(g):
        return pl.BlockSpec((1, cb), lambda bi, j: (0, g * per + j))

    out_spec = pl.BlockSpec((None, l, cb), lambda bi, j: (bi, 0, j))
    return pl.pallas_call(
        _hy_pre_kernel,
        grid=(b, per),
        in_specs=[u_spec(0), u_spec(1), u_spec(2), w_spec(0), w_spec(1), w_spec(2),
                  b_spec(0), b_spec(1), b_spec(2)],
        out_specs=[out_spec, out_spec],
        out_shape=[jax.ShapeDtypeStruct((b, l, d_hy), BF16),
                   jax.ShapeDtypeStruct((b, l, d_hy), BF16)],
        compiler_params=_cparams(("parallel", "parallel")),
        name="hy_pre",
    )(proj3, proj3, proj3, short_w, short_w, short_w, short_b, short_b, short_b)


def _filter_kernel(z_ref, zr_ref, w1_ref, b1_ref, w2_ref, b2_ref, w3_ref, b3_ref,
                   fq_ref, w4f_ref, w4b_ref, dl_ref, o_ref, hf_ref, hb_ref):
    hi = lax.Precision.HIGHEST
    seq = z_ref.shape[0]

    @pl.when(pl.program_id(0) == 0)
    def _():
        fq = fq_ref[...]

        def mlp(z):
            h = jnp.sin(fq * (jnp.dot(z, w1_ref[...], precision=hi,
                                      preferred_element_type=F32) + b1_ref[...]))
            h = jnp.sin(fq * (jnp.dot(h, w2_ref[...], precision=hi,
                                      preferred_element_type=F32) + b2_ref[...]))
            h = jnp.sin(fq * (jnp.dot(h, w3_ref[...], precision=hi,
                                      preferred_element_type=F32) + b3_ref[...]))
            return h

        def chunk(rs):
            hf_ref[rs, :] = mlp(z_ref[rs, :])
            hb_ref[rs, :] = mlp(zr_ref[rs, :])

        _for_row_chunks(seq, 4 * ROW_CHUNK, chunk)

    dl = dl_ref[...]
    t_f = z_ref[:, 0:1]
    t_b = zr_ref[:, 0:1]
    fwd = jnp.dot(hf_ref[...], w4f_ref[...], precision=hi, preferred_element_type=F32)
    fwd = fwd * jnp.exp(-t_f * dl)
    bwd = jnp.dot(hb_ref[...], w4b_ref[...], precision=hi, preferred_element_type=F32)
    bwd = bwd * jnp.exp(-t_b * dl)
    row = lax.broadcasted_iota(jnp.int32, bwd.shape, 0)
    bwd = jnp.where(row == 0, 0.0, bwd)
    o_ref[0:seq, :] = fwd.astype(o_ref.dtype)
    o_ref[seq:2 * seq, :] = bwd.astype(o_ref.dtype)


def _filter_taps(z, z_rev, w1p, b1, w2, b2, w3, b3, freq, w4, deltas_abs, d_hy, cb=256):
    seq, kpad = z.shape
    order = w2.shape[0]
    per = d_hy // cb
    full = lambda shape: pl.BlockSpec(shape, lambda j: tuple(0 for _ in shape))
    return pl.pallas_call(
        _filter_kernel,
        grid=(per,),
        in_specs=[full((seq, kpad)), full((seq, kpad)),
                  full((kpad, order)), full((1, order)),
                  full((order, order)), full((1, order)),
                  full((order, order)), full((1, order)),
                  full((1, order)),
                  pl.BlockSpec((order, cb), lambda j: (0, j)),
                  pl.BlockSpec((order, cb), lambda j: (0, per + j)),
                  pl.BlockSpec((1, cb), lambda j: (0, j))],
        out_specs=pl.BlockSpec((2 * seq, cb), lambda j: (0, j)),
        out_shape=jax.ShapeDtypeStruct((2 * seq, d_hy), BF16),
        scratch_shapes=[pltpu.VMEM((seq, order), F32), pltpu.VMEM((seq, order), F32)],
        compiler_params=_cparams(("arbitrary",)),
        name="hy_filter",
    )(z, z_rev, w1p, b1, w2, b2, w3, b3, freq, w4, w4, deltas_abs)


V7X_SUBLANES = 8


def _fft_slow_kernel(wk_ref, x_ref, o_ref):
    k, rb, c = x_ref.shape
    m = o_ref.shape[0]
    x = x_ref[...].astype(F32)
    outs = []
    for t in range(rb // V7X_SUBLANES):
        xs = x[:, t * V7X_SUBLANES:(t + 1) * V7X_SUBLANES, :]
        xs = xs.reshape(k * V7X_SUBLANES, c).astype(BF16)
        o = jnp.dot(wk_ref[...], xs, preferred_element_type=F32)
        outs.append(o.reshape(m, V7X_SUBLANES, c))
    o_ref[...] = jnp.concatenate(outs, axis=1).astype(o_ref.dtype)


def _fft_slow(wk, x4, out_dtype, rb=16):
    g, k, r, c = x4.shape
    m = wk.shape[0] // V7X_SUBLANES
    assert wk.shape[1] == k * V7X_SUBLANES
    return pl.pallas_call(
        _fft_slow_kernel,
        grid=(g, r // rb),
        in_specs=[pl.BlockSpec(wk.shape, lambda gi, j: (0, 0)),
                  pl.BlockSpec((None, k, rb, c), lambda gi, j: (gi, 0, j, 0))],
        out_specs=pl.BlockSpec((None, m, rb, c), lambda gi, j: (gi, 0, j, 0)),
        out_shape=jax.ShapeDtypeStruct((g, m, r, c), out_dtype),
        compiler_params=_cparams(("parallel", "parallel")),
        name="fft_slow",
    )(wk, x4)


def _fft_spec_kernel(e_ref, ei_ref, af_ref, a_ref, o_ref):
    pairs, two, r, c = a_ref.shape
    e = e_ref[...]
    kf = jnp.dot(e, af_ref[...].reshape(two * r, c), preferred_element_type=F32)
    kr, ki = kf[0:r], kf[r:2 * r]
    for p in range(pairs):
        b2 = jnp.dot(e, a_ref[p].reshape(two * r, c), preferred_element_type=F32)
        br, bi = b2[0:r], b2[r:2 * r]
        y2 = jnp.concatenate([br * kr - bi * ki, br * ki + bi * kr], axis=0).astype(BF16)
        g2 = jnp.dot(ei_ref[...], y2, preferred_element_type=F32)
        o_ref[p] = g2.reshape(two, r, c).astype(o_ref.dtype)


def _fft_spec(e_fwd, e_inv, a_filt, a_data):
    pairs, two, s, r, c = a_data.shape
    return pl.pallas_call(
        _fft_spec_kernel,
        grid=(s,),
        in_specs=[pl.BlockSpec((None, two * r, two * r), lambda ks: (ks, 0, 0)),
                  pl.BlockSpec((None, two * r, two * r), lambda ks: (ks, 0, 0)),
                  pl.BlockSpec((None, two, None, r, c), lambda ks: (0, 0, ks, 0, 0)),
                  pl.BlockSpec((pairs, two, None, r, c), lambda ks: (0, 0, ks, 0, 0))],
        out_specs=pl.BlockSpec((pairs, two, None, r, c), lambda ks: (0, 0, ks, 0, 0)),
        out_shape=jax.ShapeDtypeStruct((pairs, two, s, r, c), BF16),
        compiler_params=_cparams(("parallel",)),
        name="fft_spec",
    )(e_fwd, e_inv, a_filt, a_data)


def _fft_constants(seq_len):
    n = 2 * seq_len
    r_, s_ = FFT_R, FFT_S
    assert r_ * s_ == n
    half = s_ // 2
    ks = np.arange(s_)[:, None]
    s = np.arange(s_)[None, :]
    ang = 2.0 * np.pi * ((ks * s) % s_) / s_
    fr, fi = np.cos(ang), -np.sin(ang)
    w1_data = np.block([[fr[:, :half], -fi[:, :half]], [fi[:, :half], fr[:, :half]]])
    w1_filt = np.block([[fr], [fi]])
    so = np.arange(half)[:, None]
    ko = np.arange(s_)[None, :]
    ang_i = 2.0 * np.pi * ((so * ko) % s_) / s_
    cr, ci = np.cos(ang_i), np.sin(ang_i)
    w3 = np.block([[cr, -ci], [ci, cr]]) / n
    ksv = np.arange(s_)[:, None, None]
    kr = np.arange(r_)[None, :, None]
    r = np.arange(r_)[None, None, :]
    ang_e = 2.0 * np.pi * ((r * (ksv + s_ * kr)) % n) / n
    er, ei = np.cos(ang_e), -np.sin(ang_e)
    e_fwd = np.concatenate([np.concatenate([er, -ei], axis=2),
                            np.concatenate([ei, er], axis=2)], axis=1)
    e_inv = np.transpose(e_fwd, (0, 2, 1))
    as_bf16 = lambda a: jnp.asarray(a.astype(np.float32)).astype(BF16)
    eye = np.eye(V7X_SUBLANES)
    slow = [as_bf16(np.kron(w, eye)) for w in (w1_data, w1_filt, w3)]
    return slow[0], slow[1], slow[2], as_bf16(e_fwd), as_bf16(e_inv)


def _mm_out_kernel(na_ref, y_ref, vx_ref, x0_ref, db_ref, bna_ref, bhy_ref,
                   w_ref, x_ref, gt_ref, o_ref, ha_ref, hb_ref):
    bm, d_na = na_ref.shape

    def step(h_new, h_cur):
        for r0 in range(0, bm, ROW_CHUNK):
            rs = slice(r0, r0 + ROW_CHUNK)
            na = na_ref[rs, :].astype(F32)
            na_n = na * lax.rsqrt(jnp.mean(na * na, axis=-1, keepdims=True) + EPS) * bna_ref[...]
            vx = vx_ref[rs, :].astype(F32)
            hy = (y_ref[rs, :] + vx * db_ref[...]) * x0_ref[rs, :].astype(F32)
            hy_n = hy * lax.rsqrt(jnp.mean(hy * hy, axis=-1, keepdims=True) + EPS) * bhy_ref[...]
            h_new[rs, 0:d_na] = na_n.astype(BF16)
            h_new[rs, d_na:] = hy_n.astype(BF16)
        acc = jnp.dot(h_cur[...], w_ref[...], preferred_element_type=F32)
        o_ref[...] = x_ref[...] + gt_ref[0] * acc

    _skewed_steps(ha_ref, hb_ref, step)


def _mm_out(na2d, y2d, vx2d, x02d, d_bias, beta_na, beta_hy, w, x2d, gt, seq_len, bm=256):
    m, d_na = na2d.shape
    d_hy = y2d.shape[1]
    k, n = w.shape
    nb = m // bm
    blocks_per_seq = seq_len // bm
    new = lambda t: jnp.minimum(t, nb - 1)
    cur = lambda t: jnp.maximum(t - 1, 0)
    return pl.pallas_call(
        _mm_out_kernel,
        grid=(nb + 1,),
        in_specs=[pl.BlockSpec((bm, d_na), lambda t: (new(t), 0)),
                  pl.BlockSpec((bm, d_hy), lambda t: (new(t), 0)),
                  pl.BlockSpec((bm, d_hy), lambda t: (new(t), 0)),
                  pl.BlockSpec((bm, d_hy), lambda t: (new(t), 0)),
                  _resident((1, d_hy)),
                  _resident((1, d_na)),
                  _resident((1, d_hy)),
                  _resident((k, n)),
                  pl.BlockSpec((bm, n), lambda t: (cur(t), 0)),
                  pl.BlockSpec((1, 1, n), lambda t: (cur(t) // blocks_per_seq, 0, 0))],
        out_specs=pl.BlockSpec((bm, n), lambda t: (cur(t), 0)),
        out_shape=jax.ShapeDtypeStruct((m, n), F32),
        scratch_shapes=[pltpu.VMEM((bm, k), BF16), pltpu.VMEM((bm, k), BF16)],
        compiler_params=_cparams(("arbitrary",)),
        name="mm_out",
    )(na2d, y2d, vx2d, x02d, d_bias, beta_na, beta_hy, w, x2d, gt)


HALO = 16


def _mm_up_glu_kernel(x_ref, xp_ref, xn_ref, g_ref, sc_ref, sh_ref, wa_ref, wb_ref,
                      cw_ref, cb_ref, o_ref, h_ref, *, blocks_per_seq):
    bm = x_ref.shape[0]
    k = x_ref.shape[1]

    @pl.when(pl.program_id(1) == 0)
    def _():
        pos = pl.program_id(0) % blocks_per_seq
        gs = g_ref[...] * (1.0 + sc_ref[0])
        sh = sh_ref[0]

        def norm(x):
            ms = jnp.mean(x * x, axis=-1, keepdims=True)
            return x * lax.rsqrt(ms + EPS) * gs + sh

        def chunk(rs):
            h_ref[pl.ds(HALO + rs.start, rs.size), :] = norm(x_ref[rs, :]).astype(BF16)

        _for_row_chunks(bm, ROW_CHUNK, chunk)
        zero = jnp.zeros((HALO, k), F32)
        h_ref[0:HALO, :] = jnp.where(pos == 0, zero, norm(xp_ref[...])).astype(BF16)
        h_ref[HALO + bm:, :] = jnp.where(pos == blocks_per_seq - 1, zero,
                                         norm(xn_ref[...])).astype(BF16)

    a = jnp.dot(h_ref[...], wa_ref[...], preferred_element_type=F32)
    b = jnp.dot(h_ref[HALO:HALO + bm, :], wb_ref[...], preferred_element_type=F32)
    n_ext = a.shape[0]
    w = cw_ref[...]
    prev = pltpu.roll(a, 1, axis=0)[HALO:HALO + bm]
    nxt = pltpu.roll(a, n_ext - 1, axis=0)[HALO:HALO + bm]
    ac = prev * w[0:1] + a[HALO:HALO + bm] * w[1:2] + nxt * w[2:3] + cb_ref[...]
    gelu = 0.5 * ac * (1.0 + lax.erf(ac * (1.0 / math.sqrt(2.0))))
    o_ref[...] = (gelu * b).astype(o_ref.dtype)


def _mm_up_glu(x2d, g, sc, sh, w_up, conv_w, conv_b, seq_len, bm=1024, bn=512):
    m, k = x2d.shape
    d_ff = w_up.shape[1] // 2
    nb = d_ff // bn
    blocks_per_seq = seq_len // bm
    hb = bm // HALO
    last = m // HALO - 1
    return pl.pallas_call(
        functools.partial(_mm_up_glu_kernel, blocks_per_seq=blocks_per_seq),
        grid=(m // bm, nb),
        in_specs=[pl.BlockSpec((bm, k), lambda i, j: (i, 0)),
                  pl.BlockSpec((HALO, k), lambda i, j: (jnp.maximum(i * hb - 1, 0), 0)),
                  pl.BlockSpec((HALO, k), lambda i, j: (jnp.minimum((i + 1) * hb, last), 0)),
                  pl.BlockSpec((1, k), lambda i, j: (0, 0)),
                  pl.BlockSpec((1, 1, k), lambda i, j: (i // blocks_per_seq, 0, 0)),
                  pl.BlockSpec((1, 1, k), lambda i, j: (i // blocks_per_seq, 0, 0)),
                  pl.BlockSpec((k, bn), lambda i, j: (0, j)),
                  pl.BlockSpec((k, bn), lambda i, j: (0, nb + j)),
                  pl.BlockSpec((3, bn), lambda i, j: (0, j)),
                  pl.BlockSpec((1, bn), lambda i, j: (0, j))],
        out_specs=pl.BlockSpec((bm, bn), lambda i, j: (i, j)),
        out_shape=jax.ShapeDtypeStruct((m, d_ff), BF16),
        scratch_shapes=[pltpu.VMEM((bm + 2 * HALO, k), BF16)],
        compiler_params=_cparams(("parallel", "arbitrary")),
        name="mm_up_glu",
    )(x2d, x2d, x2d, g, sc, sh, w_up, w_up, conv_w, conv_b)


def _mm_down_kernel(g_ref, w_ref, x_ref, gt_ref, gf_ref, o_ref):
    acc = jnp.dot(g_ref[...], w_ref[...], preferred_element_type=F32)
    x = x_ref[...] + gt_ref[0] * acc
    ms = jnp.mean(x * x, axis=-1, keepdims=True)
    o_ref[...] = x * lax.rsqrt(ms + EPS) * gf_ref[...]


def _mm_down(g2d, w, x2d, gt, g_final, seq_len, bm=256):
    m, k = g2d.shape
    n = w.shape[1]
    blocks_per_seq = seq_len // bm
    return pl.pallas_call(
        _mm_down_kernel,
        grid=(m // bm,),
        in_specs=[pl.BlockSpec((bm, k), lambda i: (i, 0)),
                  _resident((k, n)),
                  pl.BlockSpec((bm, n), lambda i: (i, 0)),
                  pl.BlockSpec((1, 1, n), lambda i: (i // blocks_per_seq, 0, 0)),
                  _resident((1, n))],
        out_specs=pl.BlockSpec((bm, n), lambda i: (i, 0)),
        out_shape=jax.ShapeDtypeStruct((m, n), F32),
        compiler_params=_cparams(("parallel",)),
        name="mm_down",
    )(g2d, w, x2d, gt, g_final)


def _position_features(seq_len, kpad):
    t = np.linspace(0.0, 1.0, seq_len)[:, None]
    bands = (FILTER_EMB - 1) // 2
    w = 2.0 * np.pi * np.arange(seq_len)[:, None] / seq_len
    fr = np.linspace(1e-4, bands - 1, bands)[None, :]
    z = np.concatenate([t, np.cos(fr * w), -np.sin(fr * w)], axis=-1)
    z = np.pad(z, ((0, 0), (0, kpad - z.shape[1])))
    idx = (seq_len - np.arange(seq_len)) % seq_len
    return z.astype(np.float32), z[idx].astype(np.float32)


def kernel(x, c, w_ada, b_ada, g_mix, w_in, na_rpb, hy_short_w, hy_short_b,
           hy_filt_w1, hy_filt_b1, hy_filt_w2, hy_filt_b2, hy_filt_w3, hy_filt_b3,
           hy_filt_w4, hy_filt_freq, hy_bias, beta_na, beta_hy, w_out, g_ffn,
           w_up, ffn_conv_w, ffn_conv_b, w_down, g_final):
    b, l, d = x.shape
    depth = w_ada.shape[0]
    d_hy = d - D_NA
    d_ff = w_down.shape[1]
    rows = l // GRID_W
    m = b * l
    assert depth == 1 and b % 2 == 0 and 2 * l == FFT_R * FFT_S

    c_pad = jnp.pad(c, ((0, 8 - b), (0, 0)))
    mod = _adaln(c_pad, w_ada[0], b_ada[0][None, :])[:b]
    sh1, sc1, gt1, sh2, sc2, gt2 = [t[:, None, :] for t in jnp.split(mod, 6, axis=-1)]

    x2d = x.reshape(m, d)

    proj = _mm_norm(x2d, g_mix, sc1, sh1, w_in[0].astype(BF16), l)

    bias_tab = _bias_table(na_rpb[0].reshape(-1))
    na = _na_attention(proj.reshape(b, l, proj.shape[1]), bias_tab)
    na2d = na.reshape(m, D_NA)

    x0, vx = _hy_pre(proj.reshape(b, l, proj.shape[1]), hy_short_w[0], hy_short_b[0][None, :], d_hy)

    kpad = V7X_LANES
    z_np, zr_np = _position_features(l, kpad)
    w1p = jnp.pad(hy_filt_w1[0], ((0, kpad - FILTER_EMB), (0, 0)))
    max_decay = math.log(DECAY_TARGET) / FAST_DECAY_PCT
    min_decay = math.log(DECAY_TARGET) / SLOW_DECAY_PCT
    deltas_abs = np.abs(np.linspace(min_decay, max_decay, d_hy))[None, :].astype(np.float32)
    taps = _filter_taps(jnp.asarray(z_np), jnp.asarray(zr_np), w1p, hy_filt_b1,
                        hy_filt_w2[0], hy_filt_b2, hy_filt_w3[0], hy_filt_b3,
                        hy_filt_freq, hy_filt_w4[0], jnp.asarray(deltas_abs), d_hy)

    w1_data, w1_filt, w3, e_fwd, e_inv = _fft_constants(l)
    a_filt = _fft_slow(w1_filt, taps.reshape(1, FFT_S, FFT_R, d_hy), BF16)
    a_data = _fft_slow(w1_data, vx.reshape(b // 2, FFT_S, FFT_R, d_hy), BF16)
    g_spec = _fft_spec(e_fwd, e_inv, a_filt.reshape(1, 2, FFT_S, FFT_R, d_hy),
                       a_data.reshape(b // 2, 2, FFT_S, FFT_R, d_hy))
    y = _fft_slow(w3, g_spec.reshape(b // 2, 2 * FFT_S, FFT_R, d_hy), F32)
    y2d = y.reshape(m, d_hy)

    x1 = _mm_out(na2d, y2d, vx.reshape(m, d_hy), x0.reshape(m, d_hy), hy_bias,
                 beta_na, beta_hy, w_out[0].astype(BF16), x2d, gt1, l)

    gl = _mm_up_glu(x1, g_ffn, sc2, sh2, w_up[0].astype(BF16), ffn_conv_w[0],
                    ffn_conv_b[0][None, :], l)
    out = _mm_down(gl, w_down[0].astype(BF16), x1, gt2, g_final[None, :], l)
    return out.reshape(b, l, d)
```

```python
import functools
import math

import numpy as np
import jax
import jax.numpy as jnp
from jax import lax
from jax.experimental import pallas as pl
from jax.experimental.pallas import tpu as pltpu

F32 = jnp.float32
BF16 = jnp.bfloat16

GRID_W = 64
NA_HEADS = 16
NA_HEAD_DIM = 64
D_NA = NA_HEADS * NA_HEAD_DIM
NA_WIN_ROWS = 8
NA_WIN_COLS = 16
NA_GROUP = 4
FILTER_EMB = 33
DECAY_TARGET = 1e-2
FAST_DECAY_PCT = 0.3
SLOW_DECAY_PCT = 1.5
EPS = 1e-6
NEG_BIG = -1e30

V7X_LANES = 128
V7X_VMEM_BYTES = 64 * 1024 * 1024
VMEM_LIMIT = 56 * 1024 * 1024

FFT_R = 128
FFT_S = 64

ROW_CHUNK = 64


def _cparams(sem, vmem=VMEM_LIMIT):
    return pltpu.CompilerParams(dimension_semantics=sem, vmem_limit_bytes=vmem)


def _for_row_chunks(n_rows, chunk, fn):
    def body(i, carry):
        fn(pl.ds(pl.multiple_of(i * chunk, chunk), chunk))
        return carry
    lax.fori_loop(0, n_rows // chunk, body, 0)


def _adaln_kernel(c_ref, w_ref, b_ref, o_ref):
    c = c_ref[...]
    cond = c / (1.0 + jnp.exp(-c))
    o_ref[...] = jnp.dot(cond.astype(BF16), w_ref[...].astype(BF16),
                         preferred_element_type=F32) + b_ref[...]


def _adaln(c_pad, w_ada, b_ada, bn=1024):
    rows, d = c_pad.shape
    n = w_ada.shape[1]
    return pl.pallas_call(
        _adaln_kernel,
        grid=(n // bn,),
        in_specs=[pl.BlockSpec((rows, d), lambda j: (0, 0)),
                  pl.BlockSpec((d, bn), lambda j: (0, j)),
                  pl.BlockSpec((1, bn), lambda j: (0, j))],
        out_specs=pl.BlockSpec((rows, bn), lambda j: (0, j)),
        out_shape=jax.ShapeDtypeStruct((rows, n), F32),
        compiler_params=_cparams(("parallel",)),
        name="adaln",
    )(c_pad, w_ada, b_ada)


def _skewed_steps(ha_ref, hb_ref, step):
    t = pl.program_id(0)

    @pl.when(t == 0)
    def _():
        hb_ref[...] = jnp.zeros_like(hb_ref)

    @pl.when(t % 2 == 0)
    def _():
        step(ha_ref, hb_ref)

    @pl.when(t % 2 == 1)
    def _():
        step(hb_ref, ha_ref)


def _resident(shape):
    return pl.BlockSpec(shape, lambda t: tuple(0 for _ in shape), pipeline_mode=pl.Buffered(1))


def _mm_norm_kernel(x_ref, g_ref, sc_ref, sh_ref, w_ref, o_ref, ha_ref, hb_ref):
    bm = x_ref.shape[0]

    def step(h_new, h_cur):
        gs = g_ref[...] * (1.0 + sc_ref[0])
        sh = sh_ref[0]
        for r0 in range(0, bm, ROW_CHUNK):
            x = x_ref[r0:r0 + ROW_CHUNK, :]
            ms = jnp.mean(x * x, axis=-1, keepdims=True)
            h_new[r0:r0 + ROW_CHUNK, :] = (x * lax.rsqrt(ms + EPS) * gs + sh).astype(BF16)
        o_ref[...] = jnp.dot(h_cur[...], w_ref[...],
                             preferred_element_type=F32).astype(o_ref.dtype)

    _skewed_steps(ha_ref, hb_ref, step)


def _mm_norm(x2d, g, sc, sh, w, seq_len, bm=256):
    m, k = x2d.shape
    n = w.shape[1]
    nb = m // bm
    blocks_per_seq = seq_len // bm
    new = lambda t: jnp.minimum(t, nb - 1)
    cur = lambda t: jnp.maximum(t - 1, 0)
    return pl.pallas_call(
        _mm_norm_kernel,
        grid=(nb + 1,),
        in_specs=[pl.BlockSpec((bm, k), lambda t: (new(t), 0)),
                  _resident((1, k)),
                  pl.BlockSpec((1, 1, k), lambda t: (new(t) // blocks_per_seq, 0, 0)),
                  pl.BlockSpec((1, 1, k), lambda t: (new(t) // blocks_per_seq, 0, 0)),
                  _resident((k, n))],
        out_specs=pl.BlockSpec((bm, n), lambda t: (cur(t), 0)),
        out_shape=jax.ShapeDtypeStruct((m, n), BF16),
        scratch_shapes=[pltpu.VMEM((bm, k), BF16), pltpu.VMEM((bm, k), BF16)],
        compiler_params=_cparams(("arbitrary",)),
        name="mm_norm",
    )(x2d, g, sc, sh, w)


def _bias_kernel(rpb_ref, o_ref):
    h = pl.program_id(0)
    n_rows = 2 * NA_WIN_ROWS - 1
    n_cols = 2 * NA_WIN_COLS - 1
    shape = (GRID_W, 2 * GRID_W)
    lane = lax.broadcasted_iota(jnp.int32, shape, 1)
    cq = lax.broadcasted_iota(jnp.int32, shape, 0)
    ck = lane & (GRID_W - 1)
    first = lane < GRID_W
    cs = jnp.clip(cq - NA_WIN_COLS // 2, 0, GRID_W - NA_WIN_COLS)
    valid = (ck >= cs) & (ck < cs + NA_WIN_COLS)
    d = jnp.clip(ck - cq, -(NA_WIN_COLS - 1), NA_WIN_COLS - 1) + (NA_WIN_COLS - 1)
    pair = []
    for j in range(n_rows - 1):
        base0 = (h * n_rows + j) * n_cols
        base1 = base0 + n_cols
        acc = jnp.zeros(shape, F32)
        for dd in range(n_cols):
            val = jnp.where(first, rpb_ref[base0 + dd], rpb_ref[base1 + dd])
            acc = jnp.where(d == dd, val, acc)
        pair.append(jnp.where(valid, acc, NEG_BIG))
    for w in range(NA_WIN_ROWS):
        for ip in range(NA_WIN_ROWS // 2):
            o_ref[w, :, ip * 2 * GRID_W:(ip + 1) * 2 * GRID_W] = pair[w + 2 * ip]


def _bias_table(rpb_flat):
    return pl.pallas_call(
        _bias_kernel,
        grid=(NA_HEADS,),
        in_specs=[pl.BlockSpec(memory_space=pltpu.SMEM)],
        out_specs=pl.BlockSpec((None, NA_WIN_ROWS, GRID_W, NA_WIN_ROWS * GRID_W),
                               lambda h: (h, 0, 0, 0)),
        out_shape=jax.ShapeDtypeStruct(
            (NA_HEADS, NA_WIN_ROWS, GRID_W, NA_WIN_ROWS * GRID_W), F32),
        compiler_params=_cparams(("parallel",)),
        name="na_bias",
    )(rpb_flat)


NA_ROWS_PER_STEP = 4


def _na_row_start(r, rows):
    return jnp.clip(r - NA_WIN_ROWS // 2, 0, rows - NA_WIN_ROWS)


def _na_window_start(r0, rows):
    span = NA_WIN_ROWS + NA_ROWS_PER_STEP - 1
    return jnp.clip(r0 - NA_WIN_ROWS // 2, 0, rows - span)


def _na_kernel(q_ref, k_ref, v_ref, bias_ref, o_ref, *, rows):
    scale = NA_HEAD_DIM ** -0.5
    n_keys = NA_WIN_ROWS * GRID_W
    gw = NA_GROUP * NA_HEAD_DIM
    lane_head = lax.broadcasted_iota(jnp.int32, (GRID_W, gw), 1) // NA_HEAD_DIM
    r0 = pl.program_id(1) * NA_ROWS_PER_STEP
    win0 = _na_window_start(r0, rows)
    for j in range(NA_ROWS_PER_STEP):
        r = r0 + j
        rs = _na_row_start(r, rows)
        ks = pl.ds(pl.multiple_of((rs - win0) * GRID_W, GRID_W), n_keys)
        w = rs - r + (NA_WIN_ROWS - 1)
        qs = slice(j * GRID_W, (j + 1) * GRID_W)
        for g in range(NA_HEADS // NA_GROUP):
            cs = slice(g * gw, (g + 1) * gw)
            qg = q_ref[qs, cs] * scale
            kg = k_ref[ks, cs]
            vg = v_ref[ks, cs]
            zero = jnp.zeros_like(qg)
            q4 = jnp.concatenate(
                [jnp.where(lane_head == h, qg, zero) for h in range(NA_GROUP)], axis=0)
            s = lax.dot_general(q4, kg, (((1,), (1,)), ((), ())), preferred_element_type=F32)
            bias = bias_ref[NA_GROUP * g:NA_GROUP * (g + 1), w]
            s = s + bias.reshape(NA_GROUP * GRID_W, n_keys)
            m = jnp.max(s, axis=-1, keepdims=True)
            p = jnp.exp(s - m)
            l = jnp.sum(p, axis=-1, keepdims=True)
            o4 = jnp.dot(p.astype(BF16), vg, preferred_element_type=F32) / l
            o = o4[0:GRID_W]
            for h in range(1, NA_GROUP):
                o = jnp.where(lane_head == h, o4[h * GRID_W:(h + 1) * GRID_W], o)
            o_ref[qs, cs] = o.astype(o_ref.dtype)


def _na_attention(proj3, bias_tab):
    b, l, _ = proj3.shape
    rows = l // GRID_W
    rb = NA_ROWS_PER_STEP
    span = NA_WIN_ROWS + rb - 1

    def kv_spec(col0):
        return pl.BlockSpec((None, pl.Element(span * GRID_W), pl.Element(D_NA)),
                            lambda bi, t: (bi, _na_window_start(t * rb, rows) * GRID_W, col0))

    return pl.pallas_call(
        functools.partial(_na_kernel, rows=rows),
        grid=(b, rows // rb),
        in_specs=[
            pl.BlockSpec((None, rb * GRID_W, D_NA), lambda bi, t: (bi, t, 0)),
            kv_spec(D_NA),
            kv_spec(2 * D_NA),
            pl.BlockSpec(bias_tab.shape, lambda bi, t: (0, 0, 0, 0),
                         pipeline_mode=pl.Buffered(1)),
        ],
        out_specs=pl.BlockSpec((None, rb * GRID_W, D_NA), lambda bi, t: (bi, t, 0)),
        out_shape=jax.ShapeDtypeStruct((b, l, D_NA), BF16),
        compiler_params=_cparams(("parallel", "arbitrary")),
        name="na_attn",
    )(proj3, proj3, proj3, bias_tab)


def _conv3(u, w, b):
    n, c = u.shape
    t8 = 8
    row = lax.broadcasted_iota(jnp.int32, (t8, c), 0)
    prev = pltpu.roll(u, 1, axis=0)
    prev = jnp.concatenate([jnp.where(row == 0, 0.0, prev[0:t8]), prev[t8:]], axis=0)
    nxt = pltpu.roll(u, n - 1, axis=0)
    nxt = jnp.concatenate([nxt[0:n - t8], jnp.where(row == t8 - 1, 0.0, nxt[n - t8:])], axis=0)
    return prev * w[0:1] + u * w[1:2] + nxt * w[2:3] + b


def _hy_pre_kernel(u0_ref, u1_ref, u2_ref, w0_ref, w1_ref, w2_ref,
                   b0_ref, b1_ref, b2_ref, x0_ref, vx_ref):
    x0 = _conv3(u0_ref[...].astype(F32), w0_ref[...], b0_ref[...])
    x1 = _conv3(u1_ref[...].astype(F32), w1_ref[...], b1_ref[...])
    v = _conv3(u2_ref[...].astype(F32), w2_ref[...], b2_ref[...])
    x0_ref[...] = x0.astype(x0_ref.dtype)
    vx_ref[...] = (v * x1).astype(vx_ref.dtype)


def _hy_pre(proj3, short_w, short_b, d_hy, cb=128):
    b, l, _ = proj3.shape
    base = 3 * D_NA // cb
    per = d_hy // cb

    def u_spec(g):
        return pl.BlockSpec((None, l, cb), lambda bi, j: (bi, 0, base + g * per + j))

    def w_spec(g):
        return pl.BlockSpec((3, cb), lambda bi, j: (0, g * per + j))

    def b_spec(g):
        return pl.BlockSpec((1, cb), lambda bi, j: (0, g * per + j))

    out_spec = pl.BlockSpec((None, l, cb), lambda bi, j: (bi, 0, j))
    return pl.pallas_call(
        _hy_pre_kernel,
        grid=(b, per),
        in_specs=[u_spec(0), u_spec(1), u_spec(2), w_spec(0), w_spec(1), w_spec(2),
                  b_spec(0), b_spec(1), b_spec(2)],
        out_specs=[out_spec, out_spec],
        out_shape=[jax.ShapeDtypeStruct((b, l, d_hy), BF16),
                   jax.ShapeDtypeStruct((b, l, d_hy), BF16)],
        compiler_params=_cparams(("parallel", "parallel")),
        name="hy_pre",
    )(proj3, proj3, proj3, short_w, short_w, short_w, short_b, short_b, short_b)


def _filter_kernel(z_ref, w1_ref, b1_ref, w2_ref, b2_ref, w3_ref, b3_ref,
                   fq_ref, w4f_ref, w4b_ref, dl_ref, o_ref, h_ref):
    hi = lax.Precision.HIGHEST
    seq = z_ref.shape[0]

    @pl.when(pl.program_id(0) == 0)
    def _():
        fq = fq_ref[...]

        def mlp(z):
            h = jnp.sin(fq * (jnp.dot(z, w1_ref[...], precision=hi,
                                      preferred_element_type=F32) + b1_ref[...]))
            h = jnp.sin(fq * (jnp.dot(h, w2_ref[...], precision=hi,
                                      preferred_element_type=F32) + b2_ref[...]))
            h = jnp.sin(fq * (jnp.dot(h, w3_ref[...], precision=hi,
                                      preferred_element_type=F32) + b3_ref[...]))
            return h

        def chunk(rs):
            h_ref[rs, :] = mlp(z_ref[rs, :]).astype(h_ref.dtype)

        _for_row_chunks(seq, 4 * ROW_CHUNK, chunk)

    decay = jnp.exp(-z_ref[:, 0:1] * dl_ref[...])
    h3 = h_ref[...]
    fwd = jnp.dot(h3, w4f_ref[...].astype(BF16), preferred_element_type=F32) * decay
    bwd = jnp.dot(h3, w4b_ref[...].astype(BF16), preferred_element_type=F32) * decay
    row = lax.broadcasted_iota(jnp.int32, bwd.shape, 0)
    bwd = jnp.where(row == 0, 0.0, bwd)
    o_ref[0:seq, :] = fwd.astype(o_ref.dtype)
    o_ref[seq:2 * seq, :] = bwd.astype(o_ref.dtype)


def _filter_taps(z, w1p, b1, w2, b2, w3, b3, freq, w4, deltas_abs, d_hy, cb=256):
    seq, kpad = z.shape
    order = w2.shape[0]
    per = d_hy // cb
    full = lambda shape: pl.BlockSpec(shape, lambda j: tuple(0 for _ in shape))
    return pl.pallas_call(
        _filter_kernel,
        grid=(per,),
        in_specs=[full((seq, kpad)),
                  full((kpad, order)), full((1, order)),
                  full((order, order)), full((1, order)),
                  full((order, order)), full((1, order)),
                  full((1, order)),
                  pl.BlockSpec((order, cb), lambda j: (0, j)),
                  pl.BlockSpec((order, cb), lambda j: (0, per + j)),
                  pl.BlockSpec((1, cb), lambda j: (0, j))],
        out_specs=pl.BlockSpec((2 * seq, cb), lambda j: (0, j)),
        out_shape=jax.ShapeDtypeStruct((2 * seq, d_hy), BF16),
        scratch_shapes=[pltpu.VMEM((seq, order), BF16)],
        compiler_params=_cparams(("arbitrary",)),
        name="hy_filter",
    )(z, w1p, b1, w2, b2, w3, b3, freq, w4, w4, deltas_abs)


V7X_SUBLANES = 8


def _fft_slow_kernel(wk_ref, x_ref, o_ref):
    k, rb, c = x_ref.shape
    m = o_ref.shape[0]
    x = x_ref[...].astype(F32)
    outs = []
    for t in range(rb // V7X_SUBLANES):
        xs = x[:, t * V7X_SUBLANES:(t + 1) * V7X_SUBLANES, :]
        xs = xs.reshape(k * V7X_SUBLANES, c).astype(BF16)
        o = jnp.dot(wk_ref[...], xs, preferred_element_type=F32)
        outs.append(o.reshape(m, V7X_SUBLANES, c))
    o_ref[...] = jnp.concatenate(outs, axis=1).astype(o_ref.dtype)


def _fft_slow(wk, x4, out_dtype, rb=16):
    g, k, r, c = x4.shape
    m = wk.shape[0] // V7X_SUBLANES
    assert wk.shape[1] == k * V7X_SUBLANES
    return pl.pallas_call(
        _fft_slow_kernel,
        grid=(g, r // rb),
        in_specs=[pl.BlockSpec(wk.shape, lambda gi, j: (0, 0)),
                  pl.BlockSpec((None, k, rb, c), lambda gi, j: (gi, 0, j, 0))],
        out_specs=pl.BlockSpec((None, m, rb, c), lambda gi, j: (gi, 0, j, 0)),
        out_shape=jax.ShapeDtypeStruct((g, m, r, c), out_dtype),
        compiler_params=_cparams(("parallel", "parallel")),
        name="fft_slow",
    )(wk, x4)


def _fft_spec_kernel(e_ref, ei_ref, af_ref, a_ref, o_ref):
    pairs, two, r, c = a_ref.shape
    e = e_ref[...]
    hf = jnp.dot(e, af_ref[0].reshape(two * r, c), preferred_element_type=F32)
    hb = jnp.dot(e, af_ref[1].reshape(two * r, c), preferred_element_type=F32)
    kr = hf[0:r] + hb[0:r]
    ki = hf[r:2 * r] - hb[r:2 * r]
    for p in range(pairs):
        b2 = jnp.dot(e, a_ref[p].reshape(two * r, c), preferred_element_type=F32)
        br, bi = b2[0:r], b2[r:2 * r]
        y2 = jnp.concatenate([br * kr - bi * ki, br * ki + bi * kr], axis=0).astype(BF16)
        g2 = jnp.dot(ei_ref[...], y2, preferred_element_type=F32)
        o_ref[p] = g2.reshape(two, r, c).astype(o_ref.dtype)


def _fft_spec(e_fwd, e_inv, a_filt, a_data):
    pairs, two, s, r, c = a_data.shape
    return pl.pallas_call(
        _fft_spec_kernel,
        grid=(s,),
        in_specs=[pl.BlockSpec((None, two * r, two * r), lambda ks: (ks, 0, 0)),
                  pl.BlockSpec((None, two * r, two * r), lambda ks: (ks, 0, 0)),
                  pl.BlockSpec((2, two, None, r, c), lambda ks: (0, 0, ks, 0, 0)),
                  pl.BlockSpec((pairs, two, None, r, c), lambda ks: (0, 0, ks, 0, 0))],
        out_specs=pl.BlockSpec((pairs, two, None, r, c), lambda ks: (0, 0, ks, 0, 0)),
        out_shape=jax.ShapeDtypeStruct((pairs, two, s, r, c), BF16),
        compiler_params=_cparams(("parallel",)),
        name="fft_spec",
    )(e_fwd, e_inv, a_filt, a_data)


def _fft_constants(seq_len):
    n = 2 * seq_len
    r_, s_ = FFT_R, FFT_S
    assert r_ * s_ == n
    half = s_ // 2
    ks = np.arange(s_)[:, None]
    s = np.arange(s_)[None, :]
    ang = 2.0 * np.pi * ((ks * s) % s_) / s_
    fr, fi = np.cos(ang), -np.sin(ang)
    w1_data = np.block([[fr[:, :half], -fi[:, :half]], [fi[:, :half], fr[:, :half]]])
    w1_filt = np.block([[fr[:, :half]], [fi[:, :half]]])
    so = np.arange(half)[:, None]
    ko = np.arange(s_)[None, :]
    ang_i = 2.0 * np.pi * ((so * ko) % s_) / s_
    cr, ci = np.cos(ang_i), np.sin(ang_i)
    w3 = np.block([[cr, -ci], [ci, cr]]) / n
    ksv = np.arange(s_)[:, None, None]
    kr = np.arange(r_)[None, :, None]
    r = np.arange(r_)[None, None, :]
    ang_e = 2.0 * np.pi * ((r * (ksv + s_ * kr)) % n) / n
    er, ei = np.cos(ang_e), -np.sin(ang_e)
    e_fwd = np.concatenate([np.concatenate([er, -ei], axis=2),
                            np.concatenate([ei, er], axis=2)], axis=1)
    e_inv = np.transpose(e_fwd, (0, 2, 1))
    as_bf16 = lambda a: jnp.asarray(a.astype(np.float32)).astype(BF16)
    eye = np.eye(V7X_SUBLANES)
    slow = [as_bf16(np.kron(w, eye)) for w in (w1_data, w1_filt, w3)]
    return slow[0], slow[1], slow[2], as_bf16(e_fwd), as_bf16(e_inv)


def _mm_out_kernel(na_ref, y_ref, vx_ref, x0_ref, db_ref, bna_ref, bhy_ref,
                   w_ref, x_ref, gt_ref, o_ref, ha_ref, hb_ref):
    bm, d_na = na_ref.shape

    def step(h_new, h_cur):
        for r0 in range(0, bm, ROW_CHUNK):
            rs = slice(r0, r0 + ROW_CHUNK)
            na = na_ref[rs, :].astype(F32)
            na_n = na * lax.rsqrt(jnp.mean(na * na, axis=-1, keepdims=True) + EPS) * bna_ref[...]
            vx = vx_ref[rs, :].astype(F32)
            hy = (y_ref[rs, :] + vx * db_ref[...]) * x0_ref[rs, :].astype(F32)
            hy_n = hy * lax.rsqrt(jnp.mean(hy * hy, axis=-1, keepdims=True) + EPS) * bhy_ref[...]
            h_new[rs, 0:d_na] = na_n.astype(BF16)
            h_new[rs, d_na:] = hy_n.astype(BF16)
        acc = jnp.dot(h_cur[...], w_ref[...], preferred_element_type=F32)
        o_ref[...] = x_ref[...] + gt_ref[0] * acc

    _skewed_steps(ha_ref, hb_ref, step)


def _mm_out(na2d, y2d, vx2d, x02d, d_bias, beta_na, beta_hy, w, x2d, gt, seq_len, bm=256):
    m, d_na = na2d.shape
    d_hy = y2d.shape[1]
    k, n = w.shape
    nb = m // bm
    blocks_per_seq = seq_len // bm
    new = lambda t: jnp.minimum(t, nb - 1)
    cur = lambda t: jnp.maximum(t - 1, 0)
    return pl.pallas_call(
        _mm_out_kernel,
        grid=(nb + 1,),
        in_specs=[pl.BlockSpec((bm, d_na), lambda t: (new(t), 0)),
                  pl.BlockSpec((bm, d_hy), lambda t: (new(t), 0)),
                  pl.BlockSpec((bm, d_hy), lambda t: (new(t), 0)),
                  pl.BlockSpec((bm, d_hy), lambda t: (new(t), 0)),
                  _resident((1, d_hy)),
                  _resident((1, d_na)),
                  _resident((1, d_hy)),
                  _resident((k, n)),
                  pl.BlockSpec((bm, n), lambda t: (cur(t), 0)),
                  pl.BlockSpec((1, 1, n), lambda t: (cur(t) // blocks_per_seq, 0, 0))],
        out_specs=pl.BlockSpec((bm, n), lambda t: (cur(t), 0)),
        out_shape=jax.ShapeDtypeStruct((m, n), F32),
        scratch_shapes=[pltpu.VMEM((bm, k), BF16), pltpu.VMEM((bm, k), BF16)],
        compiler_params=_cparams(("arbitrary",)),
        name="mm_out",
    )(na2d, y2d, vx2d, x02d, d_bias, beta_na, beta_hy, w, x2d, gt)


HALO = 16


def _mm_up_glu_kernel(x_ref, xp_ref, xn_ref, g_ref, sc_ref, sh_ref, wa_ref, wb_ref,
                      cw_ref, cb_ref, o_ref, ha_ref, hb_ref, *, blocks_per_seq, n_blocks):
    bm, k = x_ref.shape
    t = pl.program_id(0)
    j = pl.program_id(1)
    pos = jnp.minimum(t, n_blocks - 1) % blocks_per_seq

    def norm(x):
        gs = g_ref[...] * (1.0 + sc_ref[0])
        ms = jnp.mean(x * x, axis=-1, keepdims=True)
        return x * lax.rsqrt(ms + EPS) * gs + sh_ref[0]

    def build(h_new):
        zero = jnp.zeros((HALO, k), F32)
        h_new[0:HALO, :] = jnp.where(pos == 0, zero, norm(xp_ref[...])).astype(BF16)
        for r0 in range(0, bm, ROW_CHUNK):
            h_new[HALO + r0:HALO + r0 + ROW_CHUNK, :] = norm(
                x_ref[r0:r0 + ROW_CHUNK, :]).astype(BF16)
        h_new[HALO + bm:, :] = jnp.where(pos == blocks_per_seq - 1, zero,
                                         norm(xn_ref[...])).astype(BF16)

    def multiply(h_cur):
        a = jnp.dot(h_cur[...], wa_ref[...], preferred_element_type=F32)
        b = jnp.dot(h_cur[HALO:HALO + bm, :], wb_ref[...], preferred_element_type=F32)
        n_ext = a.shape[0]
        w = cw_ref[...]
        prev = pltpu.roll(a, 1, axis=0)[HALO:HALO + bm]
        nxt = pltpu.roll(a, n_ext - 1, axis=0)[HALO:HALO + bm]
        ac = prev * w[0:1] + a[HALO:HALO + bm] * w[1:2] + nxt * w[2:3] + cb_ref[...]
        gelu = 0.5 * ac * (1.0 + lax.erf(ac * (1.0 / math.sqrt(2.0))))
        o_ref[...] = (gelu * b).astype(o_ref.dtype)

    even = t % 2 == 0

    @pl.when((t == 0) & (j == 0))
    def _():
        build(ha_ref)

    @pl.when((t > 0) & even & (j == 0))
    def _():
        build(ha_ref)
        multiply(hb_ref)

    @pl.when((t > 0) & even & (j > 0))
    def _():
        multiply(hb_ref)

    @pl.when(jnp.logical_not(even) & (j == 0))
    def _():
        build(hb_ref)
        multiply(ha_ref)

    @pl.when(jnp.logical_not(even) & (j > 0))
    def _():
        multiply(ha_ref)


def _mm_up_glu(x2d, g, sc, sh, w_up, conv_w, conv_b, seq_len, bm=1024, bn=512):
    m, k = x2d.shape
    d_ff = w_up.shape[1] // 2
    nbn = d_ff // bn
    nb = m // bm
    blocks_per_seq = seq_len // bm
    hb = bm // HALO
    last = m // HALO - 1
    new = lambda t: jnp.minimum(t, nb - 1)
    cur = lambda t: jnp.maximum(t - 1, 0)
    return pl.pallas_call(
        functools.partial(_mm_up_glu_kernel, blocks_per_seq=blocks_per_seq, n_blocks=nb),
        grid=(nb + 1, nbn),
        in_specs=[pl.BlockSpec((bm, k), lambda t, j: (new(t), 0)),
                  pl.BlockSpec((HALO, k), lambda t, j: (jnp.maximum(new(t) * hb - 1, 0), 0)),
                  pl.BlockSpec((HALO, k),
                               lambda t, j: (jnp.minimum((new(t) + 1) * hb, last), 0)),
                  pl.BlockSpec((1, k), lambda t, j: (0, 0)),
                  pl.BlockSpec((1, 1, k), lambda t, j: (new(t) // blocks_per_seq, 0, 0)),
                  pl.BlockSpec((1, 1, k), lambda t, j: (new(t) // blocks_per_seq, 0, 0)),
                  pl.BlockSpec((k, bn), lambda t, j: (0, j)),
                  pl.BlockSpec((k, bn), lambda t, j: (0, nbn + j)),
                  pl.BlockSpec((3, bn), lambda t, j: (0, j)),
                  pl.BlockSpec((1, bn), lambda t, j: (0, j))],
        out_specs=pl.BlockSpec((bm, bn), lambda t, j: (cur(t), jnp.where(t == 0, 0, j))),
        out_shape=jax.ShapeDtypeStruct((m, d_ff), BF16),
        scratch_shapes=[pltpu.VMEM((bm + 2 * HALO, k), BF16),
                        pltpu.VMEM((bm + 2 * HALO, k), BF16)],
        compiler_params=_cparams(("arbitrary", "arbitrary")),
        name="mm_up_glu",
    )(x2d, x2d, x2d, g, sc, sh, w_up, w_up, conv_w, conv_b)


def _mm_down_kernel(g_ref, w_ref, x_ref, gt_ref, gf_ref, o_ref):
    acc = jnp.dot(g_ref[...], w_ref[...], preferred_element_type=F32)
    x = x_ref[...] + gt_ref[0] * acc
    ms = jnp.mean(x * x, axis=-1, keepdims=True)
    o_ref[...] = x * lax.rsqrt(ms + EPS) * gf_ref[...]


def _mm_down(g2d, w, x2d, gt, g_final, seq_len, bm=256):
    m, k = g2d.shape
    n = w.shape[1]
    blocks_per_seq = seq_len // bm
    return pl.pallas_call(
        _mm_down_kernel,
        grid=(m // bm,),
        in_specs=[pl.BlockSpec((bm, k), lambda i: (i, 0)),
                  _resident((k, n)),
                  pl.BlockSpec((bm, n), lambda i: (i, 0)),
                  pl.BlockSpec((1, 1, n), lambda i: (i // blocks_per_seq, 0, 0)),
                  _resident((1, n))],
        out_specs=pl.BlockSpec((bm, n), lambda i: (i, 0)),
        out_shape=jax.ShapeDtypeStruct((m, n), F32),
        compiler_params=_cparams(("parallel",)),
        name="mm_down",
    )(g2d, w, x2d, gt, g_final)


def _position_features(seq_len, kpad):
    t = np.linspace(0.0, 1.0, seq_len)[:, None]
    bands = (FILTER_EMB - 1) // 2
    w = 2.0 * np.pi * np.arange(seq_len)[:, None] / seq_len
    fr = np.linspace(1e-4, bands - 1, bands)[None, :]
    z = np.concatenate([t, np.cos(fr * w), -np.sin(fr * w)], axis=-1)
    z = np.pad(z, ((0, 0), (0, kpad - z.shape[1])))
    return z.astype(np.float32)


def kernel(x, c, w_ada, b_ada, g_mix, w_in, na_rpb, hy_short_w, hy_short_b,
           hy_filt_w1, hy_filt_b1, hy_filt_w2, hy_filt_b2, hy_filt_w3, hy_filt_b3,
           hy_filt_w4, hy_filt_freq, hy_bias, beta_na, beta_hy, w_out, g_ffn,
           w_up, ffn_conv_w, ffn_conv_b, w_down, g_final):
    b, l, d = x.shape
    depth = w_ada.shape[0]
    d_hy = d - D_NA
    d_ff = w_down.shape[1]
    rows = l // GRID_W
    m = b * l
    assert depth == 1 and b % 2 == 0 and 2 * l == FFT_R * FFT_S

    c_pad = jnp.pad(c, ((0, 8 - b), (0, 0)))
    mod = _adaln(c_pad, w_ada[0], b_ada[0][None, :])[:b]
    sh1, sc1, gt1, sh2, sc2, gt2 = [t[:, None, :] for t in jnp.split(mod, 6, axis=-1)]

    x2d = x.reshape(m, d)

    proj = _mm_norm(x2d, g_mix, sc1, sh1, w_in[0].astype(BF16), l)

    bias_tab = _bias_table(na_rpb[0].reshape(-1))
    na = _na_attention(proj.reshape(b, l, proj.shape[1]), bias_tab)
    na2d = na.reshape(m, D_NA)

    x0, vx = _hy_pre(proj.reshape(b, l, proj.shape[1]), hy_short_w[0], hy_short_b[0][None, :], d_hy)

    kpad = V7X_LANES
    z_np = _position_features(l, kpad)
    w1p = jnp.pad(hy_filt_w1[0], ((0, kpad - FILTER_EMB), (0, 0)))
    max_decay = math.log(DECAY_TARGET) / FAST_DECAY_PCT
    min_decay = math.log(DECAY_TARGET) / SLOW_DECAY_PCT
    deltas_abs = np.abs(np.linspace(min_decay, max_decay, d_hy))[None, :].astype(np.float32)
    taps = _filter_taps(jnp.asarray(z_np), w1p, hy_filt_b1,
                        hy_filt_w2[0], hy_filt_b2, hy_filt_w3[0], hy_filt_b3,
                        hy_filt_freq, hy_filt_w4[0], jnp.asarray(deltas_abs), d_hy)

    w1_data, w1_filt, w3, e_fwd, e_inv = _fft_constants(l)
    a_filt = _fft_slow(w1_filt, taps.reshape(2, FFT_S // 2, FFT_R, d_hy), BF16)
    a_data = _fft_slow(w1_data, vx.reshape(b // 2, FFT_S, FFT_R, d_hy), BF16)
    g_spec = _fft_spec(e_fwd, e_inv, a_filt.reshape(2, 2, FFT_S, FFT_R, d_hy),
                       a_data.reshape(b // 2, 2, FFT_S, FFT_R, d_hy))
    y = _fft_slow(w3, g_spec.reshape(b // 2, 2 * FFT_S, FFT_R, d_hy), F32)
    y2d = y.reshape(m, d_hy)

    x1 = _mm_out(na2d, y2d, vx.reshape(m, d_hy), x0.reshape(m, d_hy), hy_bias,
                 beta_na, beta_hy, w_out[0].astype(BF16), x2d, gt1, l)

    gl = _mm_up_glu(x1, g_ffn, sc2, sh2, w_up[0].astype(BF16), ffn_conv_w[0],
                    ffn_conv_b[0][None, :], l)
    out = _mm_down(gl, w_down[0].astype(BF16), x1, gt2, g_final[None, :], l)
    return out.reshape(b, l, d)
```

```python
import functools
import math

import numpy as np
import jax
import jax.numpy as jnp
from jax import lax
from jax.experimental import pallas as pl
from jax.experimental.pallas import tpu as pltpu

F32 = jnp.float32
BF16 = jnp.bfloat16

GRID_W = 64
NA_HEADS = 16
NA_HEAD_DIM = 64
D_NA = NA_HEADS * NA_HEAD_DIM
NA_WIN_ROWS = 8
NA_WIN_COLS = 16
NA_GROUP = 4
FILTER_EMB = 33
DECAY_TARGET = 1e-2
FAST_DECAY_PCT = 0.3
SLOW_DECAY_PCT = 1.5
EPS = 1e-6
NEG_BIG = -1e30
LOG2_E = math.log2(math.e)
NA_Q_SCALE = NA_HEAD_DIM ** -0.5 * LOG2_E

V7X_LANES = 128
V7X_VMEM_BYTES = 64 * 1024 * 1024
VMEM_LIMIT = 56 * 1024 * 1024

FFT_R = 128
FFT_S = 64

ROW_CHUNK = 64


def _cparams(sem, vmem=VMEM_LIMIT):
    return pltpu.CompilerParams(dimension_semantics=sem, vmem_limit_bytes=vmem)


def _for_row_chunks(n_rows, chunk, fn):
    def body(i, carry):
        fn(pl.ds(pl.multiple_of(i * chunk, chunk), chunk))
        return carry
    lax.fori_loop(0, n_rows // chunk, body, 0)


def _adaln_kernel(c_ref, w_ref, b_ref, o_ref):
    c = c_ref[...]
    cond = c / (1.0 + jnp.exp(-c))
    o_ref[...] = jnp.dot(cond.astype(BF16), w_ref[...].astype(BF16),
                         preferred_element_type=F32) + b_ref[...]


def _adaln(c_pad, w_ada, b_ada, bn=1024):
    rows, d = c_pad.shape
    n = w_ada.shape[1]
    return pl.pallas_call(
        _adaln_kernel,
        grid=(n // bn,),
        in_specs=[pl.BlockSpec((rows, d), lambda j: (0, 0)),
                  pl.BlockSpec((d, bn), lambda j: (0, j)),
                  pl.BlockSpec((1, bn), lambda j: (0, j))],
        out_specs=pl.BlockSpec((rows, bn), lambda j: (0, j)),
        out_shape=jax.ShapeDtypeStruct((rows, n), F32),
        compiler_params=_cparams(("parallel",)),
        name="adaln",
    )(c_pad, w_ada, b_ada)


def _skewed_steps(ha_ref, hb_ref, step):
    t = pl.program_id(0)

    @pl.when(t == 0)
    def _():
        hb_ref[...] = jnp.zeros_like(hb_ref)

    @pl.when(t % 2 == 0)
    def _():
        step(ha_ref, hb_ref)

    @pl.when(t % 2 == 1)
    def _():
        step(hb_ref, ha_ref)


def _resident(shape):
    return pl.BlockSpec(shape, lambda t: tuple(0 for _ in shape), pipeline_mode=pl.Buffered(1))


def _mm_norm_kernel(x_ref, g_ref, sc_ref, sh_ref, w_ref, o_ref, ha_ref, hb_ref):
    bm = x_ref.shape[0]

    def step(h_new, h_cur):
        gs = g_ref[...] * (1.0 + sc_ref[0])
        sh = sh_ref[0]
        for r0 in range(0, bm, ROW_CHUNK):
            x = x_ref[r0:r0 + ROW_CHUNK, :]
            ms = jnp.mean(x * x, axis=-1, keepdims=True)
            h_new[r0:r0 + ROW_CHUNK, :] = (x * lax.rsqrt(ms + EPS) * gs + sh).astype(BF16)
        o_ref[...] = jnp.dot(h_cur[...], w_ref[...],
                             preferred_element_type=F32).astype(o_ref.dtype)

    _skewed_steps(ha_ref, hb_ref, step)


def _mm_norm(x2d, g, sc, sh, w, seq_len, bm=256):
    m, k = x2d.shape
    n = w.shape[1]
    nb = m // bm
    blocks_per_seq = seq_len // bm
    new = lambda t: jnp.minimum(t, nb - 1)
    cur = lambda t: jnp.maximum(t - 1, 0)
    return pl.pallas_call(
        _mm_norm_kernel,
        grid=(nb + 1,),
        in_specs=[pl.BlockSpec((bm, k), lambda t: (new(t), 0)),
                  _resident((1, k)),
                  pl.BlockSpec((1, 1, k), lambda t: (new(t) // blocks_per_seq, 0, 0)),
                  pl.BlockSpec((1, 1, k), lambda t: (new(t) // blocks_per_seq, 0, 0)),
                  _resident((k, n))],
        out_specs=pl.BlockSpec((bm, n), lambda t: (cur(t), 0)),
        out_shape=jax.ShapeDtypeStruct((m, n), BF16),
        scratch_shapes=[pltpu.VMEM((bm, k), BF16), pltpu.VMEM((bm, k), BF16)],
        compiler_params=_cparams(("arbitrary",)),
        name="mm_norm",
    )(x2d, g, sc, sh, w)


def _bias_kernel(rpb_ref, o_ref):
    h = pl.program_id(0)
    n_rows = 2 * NA_WIN_ROWS - 1
    n_cols = 2 * NA_WIN_COLS - 1
    shape = (GRID_W, 2 * GRID_W)
    lane = lax.broadcasted_iota(jnp.int32, shape, 1)
    cq = lax.broadcasted_iota(jnp.int32, shape, 0)
    ck = lane & (GRID_W - 1)
    first = lane < GRID_W
    cs = jnp.clip(cq - NA_WIN_COLS // 2, 0, GRID_W - NA_WIN_COLS)
    valid = (ck >= cs) & (ck < cs + NA_WIN_COLS)
    d = jnp.clip(ck - cq, -(NA_WIN_COLS - 1), NA_WIN_COLS - 1) + (NA_WIN_COLS - 1)
    pair = []
    for j in range(n_rows - 1):
        base0 = (h * n_rows + j) * n_cols
        base1 = base0 + n_cols
        acc = jnp.zeros(shape, F32)
        for dd in range(n_cols):
            val = jnp.where(first, rpb_ref[base0 + dd], rpb_ref[base1 + dd])
            acc = jnp.where(d == dd, val, acc)
        pair.append(jnp.where(valid, acc * LOG2_E, NEG_BIG))
    for w in range(NA_WIN_ROWS):
        for ip in range(NA_WIN_ROWS // 2):
            o_ref[w, :, ip * 2 * GRID_W:(ip + 1) * 2 * GRID_W] = pair[w + 2 * ip]


def _bias_table(rpb_flat):
    return pl.pallas_call(
        _bias_kernel,
        grid=(NA_HEADS,),
        in_specs=[pl.BlockSpec(memory_space=pltpu.SMEM)],
        out_specs=pl.BlockSpec((None, NA_WIN_ROWS, GRID_W, NA_WIN_ROWS * GRID_W),
                               lambda h: (h, 0, 0, 0)),
        out_shape=jax.ShapeDtypeStruct(
            (NA_HEADS, NA_WIN_ROWS, GRID_W, NA_WIN_ROWS * GRID_W), F32),
        compiler_params=_cparams(("parallel",)),
        name="na_bias",
    )(rpb_flat)


NA_ROWS_PER_STEP = 4


def _na_row_start(r, rows):
    return jnp.clip(r - NA_WIN_ROWS // 2, 0, rows - NA_WIN_ROWS)


def _na_window_start(r0, rows):
    span = NA_WIN_ROWS + NA_ROWS_PER_STEP - 1
    return jnp.clip(r0 - NA_WIN_ROWS // 2, 0, rows - span)


def _na_kernel(q_ref, k_ref, v_ref, bias_ref, o_ref, *, rows):
    n_keys = NA_WIN_ROWS * GRID_W
    gw = NA_GROUP * NA_HEAD_DIM
    lane_head = lax.broadcasted_iota(jnp.int32, (GRID_W, gw), 1) // NA_HEAD_DIM
    r0 = pl.program_id(1) * NA_ROWS_PER_STEP
    win0 = _na_window_start(r0, rows)
    for j in range(NA_ROWS_PER_STEP):
        r = r0 + j
        rs = _na_row_start(r, rows)
        ks = pl.ds(pl.multiple_of((rs - win0) * GRID_W, GRID_W), n_keys)
        w = rs - r + (NA_WIN_ROWS - 1)
        qs = slice(j * GRID_W, (j + 1) * GRID_W)
        for g in range(NA_HEADS // NA_GROUP):
            cs = slice(g * gw, (g + 1) * gw)
            qg = q_ref[qs, cs]
            kg = k_ref[ks, cs]
            vg = v_ref[ks, cs]
            zero = jnp.zeros_like(qg)
            q4 = jnp.concatenate(
                [jnp.where(lane_head == h, qg, zero) for h in range(NA_GROUP)], axis=0)
            s = lax.dot_general(q4, kg, (((1,), (1,)), ((), ())), preferred_element_type=F32)
            bias = bias_ref[NA_GROUP * g:NA_GROUP * (g + 1), w]
            s = s + bias.reshape(NA_GROUP * GRID_W, n_keys)
            m = jnp.max(s, axis=-1, keepdims=True)
            p = jnp.exp2(s - m)
            l = jnp.sum(p, axis=-1, keepdims=True)
            o4 = jnp.dot(p.astype(BF16), vg, preferred_element_type=F32) / l
            o = o4[0:GRID_W]
            for h in range(1, NA_GROUP):
                o = jnp.where(lane_head == h, o4[h * GRID_W:(h + 1) * GRID_W], o)
            o_ref[qs, cs] = o.astype(o_ref.dtype)


def _na_attention(proj3, bias_tab):
    b, l, _ = proj3.shape
    rows = l // GRID_W
    rb = NA_ROWS_PER_STEP
    span = NA_WIN_ROWS + rb - 1

    def kv_spec(col0):
        return pl.BlockSpec((None, pl.Element(span * GRID_W), pl.Element(D_NA)),
                            lambda bi, t: (bi, _na_window_start(t * rb, rows) * GRID_W, col0))

    return pl.pallas_call(
        functools.partial(_na_kernel, rows=rows),
        grid=(b, rows // rb),
        in_specs=[
            pl.BlockSpec((None, rb * GRID_W, D_NA), lambda bi, t: (bi, t, 0)),
            kv_spec(D_NA),
            kv_spec(2 * D_NA),
            pl.BlockSpec(bias_tab.shape, lambda bi, t: (0, 0, 0, 0),
                         pipeline_mode=pl.Buffered(1)),
        ],
        out_specs=pl.BlockSpec((None, rb * GRID_W, D_NA), lambda bi, t: (bi, t, 0)),
        out_shape=jax.ShapeDtypeStruct((b, l, D_NA), BF16),
        compiler_params=_cparams(("parallel", "arbitrary")),
        name="na_attn",
    )(proj3, proj3, proj3, bias_tab)


def _conv3(u, w, b):
    n, c = u.shape
    t8 = 8
    row = lax.broadcasted_iota(jnp.int32, (t8, c), 0)
    prev = pltpu.roll(u, 1, axis=0)
    prev = jnp.concatenate([jnp.where(row == 0, 0.0, prev[0:t8]), prev[t8:]], axis=0)
    nxt = pltpu.roll(u, n - 1, axis=0)
    nxt = jnp.concatenate([nxt[0:n - t8], jnp.where(row == t8 - 1, 0.0, nxt[n - t8:])], axis=0)
    return prev * w[0:1] + u * w[1:2] + nxt * w[2:3] + b


def _hy_pre_kernel(u0_ref, u1_ref, u2_ref, w0_ref, w1_ref, w2_ref,
                   b0_ref, b1_ref, b2_ref, x0_ref, vx_ref):
    x0 = _conv3(u0_ref[...].astype(F32), w0_ref[...], b0_ref[...])
    x1 = _conv3(u1_ref[...].astype(F32), w1_ref[...], b1_ref[...])
    v = _conv3(u2_ref[...].astype(F32), w2_ref[...], b2_ref[...])
    x0_ref[...] = x0.astype(x0_ref.dtype)
    vx_ref[...] = (v * x1).astype(vx_ref.dtype)


def _hy_pre(proj3, short_w, short_b, d_hy, cb=128):
    b, l, _ = proj3.shape
    base = 3 * D_NA // cb
    per = d_hy // cb

    def u_spec(g):
        return pl.BlockSpec((None, l, cb), lambda bi, j: (bi, 0, base + g * per + j))

    def w_spec(g):
        return pl.BlockSpec((3, cb), lambda bi, j: (0, g * per + j))

    def b_spec(g):
        return pl.BlockSpec((1, cb), lambda bi, j: (0, g * per + j))

    out_spec = pl.BlockSpec((None, l, cb), lambda bi, j: (bi, 0, j))
    return pl.pallas_call(
        _hy_pre_kernel,
        grid=(b, per),
        in_specs=[u_spec(0), u_spec(1), u_spec(2), w_spec(0), w_spec(1), w_spec(2),
                  b_spec(0), b_spec(1), b_spec(2)],
        out_specs=[out_spec, out_spec],
        out_shape=[jax.ShapeDtypeStruct((b, l, d_hy), BF16),
                   jax.ShapeDtypeStruct((b, l, d_hy), BF16)],
        compiler_params=_cparams(("parallel", "parallel")),
        name="hy_pre",
    )(proj3, proj3, proj3, short_w, short_w, short_w, short_b, short_b, short_b)


def _filter_kernel(z_ref, w1_ref, b1_ref, w2_ref, b2_ref, w3_ref, b3_ref,
                   fq_ref, w4f_ref, w4b_ref, dl_ref, o_ref, h_ref):
    hi = lax.Precision.HIGHEST
    seq = z_ref.shape[0]

    @pl.when(pl.program_id(0) == 0)
    def _():
        fq = fq_ref[...]

        def mlp(z):
            h = jnp.sin(fq * (jnp.dot(z, w1_ref[...], precision=hi,
                                      preferred_element_type=F32) + b1_ref[...]))
            h = jnp.sin(fq * (jnp.dot(h, w2_ref[...], precision=hi,
                                      preferred_element_type=F32) + b2_ref[...]))
            h = jnp.sin(fq * (jnp.dot(h, w3_ref[...], precision=hi,
                                      preferred_element_type=F32) + b3_ref[...]))
            return h

        def chunk(rs):
            h_ref[rs, :] = mlp(z_ref[rs, :]).astype(h_ref.dtype)

        _for_row_chunks(seq, 4 * ROW_CHUNK, chunk)

    decay = jnp.exp(-z_ref[:, 0:1] * dl_ref[...])
    h3 = h_ref[...]
    fwd = jnp.dot(h3, w4f_ref[...].astype(BF16), preferred_element_type=F32) * decay
    bwd = jnp.dot(h3, w4b_ref[...].astype(BF16), preferred_element_type=F32) * decay
    row = lax.broadcasted_iota(jnp.int32, bwd.shape, 0)
    bwd = jnp.where(row == 0, 0.0, bwd)
    o_ref[0:seq, :] = fwd.astype(o_ref.dtype)
    o_ref[seq:2 * seq, :] = bwd.astype(o_ref.dtype)


def _filter_taps(z, w1p, b1, w2, b2, w3, b3, freq, w4, deltas_abs, d_hy, cb=256):
    seq, kpad = z.shape
    order = w2.shape[0]
    per = d_hy // cb
    full = lambda shape: pl.BlockSpec(shape, lambda j: tuple(0 for _ in shape))
    return pl.pallas_call(
        _filter_kernel,
        grid=(per,),
        in_specs=[full((seq, kpad)),
                  full((kpad, order)), full((1, order)),
                  full((order, order)), full((1, order)),
                  full((order, order)), full((1, order)),
                  full((1, order)),
                  pl.BlockSpec((order, cb), lambda j: (0, j)),
                  pl.BlockSpec((order, cb), lambda j: (0, per + j)),
                  pl.BlockSpec((1, cb), lambda j: (0, j))],
        out_specs=pl.BlockSpec((2 * seq, cb), lambda j: (0, j)),
        out_shape=jax.ShapeDtypeStruct((2 * seq, d_hy), BF16),
        scratch_shapes=[pltpu.VMEM((seq, order), BF16)],
        compiler_params=_cparams(("arbitrary",)),
        name="hy_filter",
    )(z, w1p, b1, w2, b2, w3, b3, freq, w4, w4, deltas_abs)


V7X_SUBLANES = 8


def _fft_slow_kernel(wk_ref, x_ref, o_ref):
    k, rb, c = x_ref.shape
    m = o_ref.shape[0]
    x = x_ref[...].astype(F32)
    outs = []
    for t in range(rb // V7X_SUBLANES):
        xs = x[:, t * V7X_SUBLANES:(t + 1) * V7X_SUBLANES, :]
        xs = xs.reshape(k * V7X_SUBLANES, c).astype(BF16)
        o = jnp.dot(wk_ref[...], xs, preferred_element_type=F32)
        outs.append(o.reshape(m, V7X_SUBLANES, c))
    o_ref[...] = jnp.concatenate(outs, axis=1).astype(o_ref.dtype)


def _fft_slow(wk, x4, out_dtype, rb=16):
    g, k, r, c = x4.shape
    m = wk.shape[0] // V7X_SUBLANES
    assert wk.shape[1] == k * V7X_SUBLANES
    return pl.pallas_call(
        _fft_slow_kernel,
        grid=(g, r // rb),
        in_specs=[pl.BlockSpec(wk.shape, lambda gi, j: (0, 0)),
                  pl.BlockSpec((None, k, rb, c), lambda gi, j: (gi, 0, j, 0))],
        out_specs=pl.BlockSpec((None, m, rb, c), lambda gi, j: (gi, 0, j, 0)),
        out_shape=jax.ShapeDtypeStruct((g, m, r, c), out_dtype),
        compiler_params=_cparams(("parallel", "parallel")),
        name="fft_slow",
    )(wk, x4)


SPEC_SLABS = 2


def _fft_spec_kernel(e_ref, ei_ref, af_ref, a_ref, o_ref):
    pairs, two, slabs, r, c = a_ref.shape
    for q in range(slabs):
        e = e_ref[q]
        hf = jnp.dot(e, af_ref[0, :, q].reshape(two * r, c), preferred_element_type=F32)
        hb = jnp.dot(e, af_ref[1, :, q].reshape(two * r, c), preferred_element_type=F32)
        kr = hf[0:r] + hb[0:r]
        ki = hf[r:2 * r] - hb[r:2 * r]
        for p in range(pairs):
            b2 = jnp.dot(e, a_ref[p, :, q].reshape(two * r, c), preferred_element_type=F32)
            br, bi = b2[0:r], b2[r:2 * r]
            y2 = jnp.concatenate([br * kr - bi * ki, br * ki + bi * kr], axis=0).astype(BF16)
            g2 = jnp.dot(ei_ref[q], y2, preferred_element_type=F32)
            o_ref[p, :, q] = g2.reshape(two, r, c).astype(o_ref.dtype)


def _fft_spec(e_fwd, e_inv, a_filt, a_data):
    pairs, two, s, r, c = a_data.shape
    q = SPEC_SLABS
    return pl.pallas_call(
        _fft_spec_kernel,
        grid=(s // q,),
        in_specs=[pl.BlockSpec((q, two * r, two * r), lambda i: (i, 0, 0)),
                  pl.BlockSpec((q, two * r, two * r), lambda i: (i, 0, 0)),
                  pl.BlockSpec((2, two, q, r, c), lambda i: (0, 0, i, 0, 0)),
                  pl.BlockSpec((pairs, two, q, r, c), lambda i: (0, 0, i, 0, 0))],
        out_specs=pl.BlockSpec((pairs, two, q, r, c), lambda i: (0, 0, i, 0, 0)),
        out_shape=jax.ShapeDtypeStruct((pairs, two, s, r, c), BF16),
        compiler_params=_cparams(("parallel",)),
        name="fft_spec",
    )(e_fwd, e_inv, a_filt, a_data)


def _fft_constants(seq_len):
    n = 2 * seq_len
    r_, s_ = FFT_R, FFT_S
    assert r_ * s_ == n
    half = s_ // 2
    ks = np.arange(s_)[:, None]
    s = np.arange(s_)[None, :]
    ang = 2.0 * np.pi * ((ks * s) % s_) / s_
    fr, fi = np.cos(ang), -np.sin(ang)
    w1_data = np.block([[fr[:, :half], -fi[:, :half]], [fi[:, :half], fr[:, :half]]])
    w1_filt = np.block([[fr[:, :half]], [fi[:, :half]]])
    so = np.arange(half)[:, None]
    ko = np.arange(s_)[None, :]
    ang_i = 2.0 * np.pi * ((so * ko) % s_) / s_
    cr, ci = np.cos(ang_i), np.sin(ang_i)
    w3 = np.block([[cr, -ci], [ci, cr]]) / n
    ksv = np.arange(s_)[:, None, None]
    kr = np.arange(r_)[None, :, None]
    r = np.arange(r_)[None, None, :]
    ang_e = 2.0 * np.pi * ((r * (ksv + s_ * kr)) % n) / n
    er, ei = np.cos(ang_e), -np.sin(ang_e)
    e_fwd = np.concatenate([np.concatenate([er, -ei], axis=2),
                            np.concatenate([ei, er], axis=2)], axis=1)
    e_inv = np.transpose(e_fwd, (0, 2, 1))
    as_bf16 = lambda a: jnp.asarray(a.astype(np.float32)).astype(BF16)
    eye = np.eye(V7X_SUBLANES)
    slow = [as_bf16(np.kron(w, eye)) for w in (w1_data, w1_filt, w3)]
    return slow[0], slow[1], slow[2], as_bf16(e_fwd), as_bf16(e_inv)


def _mm_out_kernel(na_ref, y_ref, vx_ref, x0_ref, db_ref, bna_ref, bhy_ref,
                   w_ref, x_ref, gt_ref, o_ref, ha_ref, hb_ref):
    bm, d_na = na_ref.shape

    def step(h_new, h_cur):
        for r0 in range(0, bm, ROW_CHUNK):
            rs = slice(r0, r0 + ROW_CHUNK)
            na = na_ref[rs, :].astype(F32)
            na_n = na * lax.rsqrt(jnp.mean(na * na, axis=-1, keepdims=True) + EPS) * bna_ref[...]
            vx = vx_ref[rs, :].astype(F32)
            hy = (y_ref[rs, :] + vx * db_ref[...]) * x0_ref[rs, :].astype(F32)
            hy_n = hy * lax.rsqrt(jnp.mean(hy * hy, axis=-1, keepdims=True) + EPS) * bhy_ref[...]
            h_new[rs, 0:d_na] = na_n.astype(BF16)
            h_new[rs, d_na:] = hy_n.astype(BF16)
        acc = jnp.dot(h_cur[...], w_ref[...], preferred_element_type=F32)
        o_ref[...] = x_ref[...] + gt_ref[0] * acc

    _skewed_steps(ha_ref, hb_ref, step)


def _mm_out(na2d, y2d, vx2d, x02d, d_bias, beta_na, beta_hy, w, x2d, gt, seq_len, bm=256):
    m, d_na = na2d.shape
    d_hy = y2d.shape[1]
    k, n = w.shape
    nb = m // bm
    blocks_per_seq = seq_len // bm
    new = lambda t: jnp.minimum(t, nb - 1)
    cur = lambda t: jnp.maximum(t - 1, 0)
    return pl.pallas_call(
        _mm_out_kernel,
        grid=(nb + 1,),
        in_specs=[pl.BlockSpec((bm, d_na), lambda t: (new(t), 0)),
                  pl.BlockSpec((bm, d_hy), lambda t: (new(t), 0)),
                  pl.BlockSpec((bm, d_hy), lambda t: (new(t), 0)),
                  pl.BlockSpec((bm, d_hy), lambda t: (new(t), 0)),
                  _resident((1, d_hy)),
                  _resident((1, d_na)),
                  _resident((1, d_hy)),
                  _resident((k, n)),
                  pl.BlockSpec((bm, n), lambda t: (cur(t), 0)),
                  pl.BlockSpec((1, 1, n), lambda t: (cur(t) // blocks_per_seq, 0, 0))],
        out_specs=pl.BlockSpec((bm, n), lambda t: (cur(t), 0)),
        out_shape=jax.ShapeDtypeStruct((m, n), F32),
        scratch_shapes=[pltpu.VMEM((bm, k), BF16), pltpu.VMEM((bm, k), BF16)],
        compiler_params=_cparams(("arbitrary",)),
        name="mm_out",
    )(na2d, y2d, vx2d, x02d, d_bias, beta_na, beta_hy, w, x2d, gt)


HALO = 16


def _mm_up_glu_kernel(x_ref, xp_ref, xn_ref, g_ref, sc_ref, sh_ref, wa_ref, wb_ref,
                      cw_ref, cb_ref, o_ref, ha_ref, hb_ref, *, blocks_per_seq, n_blocks):
    bm, k = x_ref.shape
    t = pl.program_id(0)
    j = pl.program_id(1)
    pos = jnp.minimum(t, n_blocks - 1) % blocks_per_seq

    def norm(x):
        gs = g_ref[...] * (1.0 + sc_ref[0])
        ms = jnp.mean(x * x, axis=-1, keepdims=True)
        return x * lax.rsqrt(ms + EPS) * gs + sh_ref[0]

    def build(h_new):
        zero = jnp.zeros((HALO, k), F32)
        h_new[0:HALO, :] = jnp.where(pos == 0, zero, norm(xp_ref[...])).astype(BF16)
        for r0 in range(0, bm, ROW_CHUNK):
            h_new[HALO + r0:HALO + r0 + ROW_CHUNK, :] = norm(
                x_ref[r0:r0 + ROW_CHUNK, :]).astype(BF16)
        h_new[HALO + bm:, :] = jnp.where(pos == blocks_per_seq - 1, zero,
                                         norm(xn_ref[...])).astype(BF16)

    def multiply(h_cur):
        a = jnp.dot(h_cur[...], wa_ref[...].astype(BF16), preferred_element_type=F32)
        b = jnp.dot(h_cur[HALO:HALO + bm, :], wb_ref[...].astype(BF16),
                    preferred_element_type=F32)
        n_ext = a.shape[0]
        w = cw_ref[...]
        prev = pltpu.roll(a, 1, axis=0)[HALO:HALO + bm]
        nxt = pltpu.roll(a, n_ext - 1, axis=0)[HALO:HALO + bm]
        ac = prev * w[0:1] + a[HALO:HALO + bm] * w[1:2] + nxt * w[2:3] + cb_ref[...]
        gelu = 0.5 * ac * (1.0 + lax.erf(ac * (1.0 / math.sqrt(2.0))))
        o_ref[...] = (gelu * b).astype(o_ref.dtype)

    even = t % 2 == 0

    @pl.when((t == 0) & (j == 0))
    def _():
        build(ha_ref)

    @pl.when((t > 0) & even & (j == 0))
    def _():
        build(ha_ref)
        multiply(hb_ref)

    @pl.when((t > 0) & even & (j > 0))
    def _():
        multiply(hb_ref)

    @pl.when(jnp.logical_not(even) & (j == 0))
    def _():
        build(hb_ref)
        multiply(ha_ref)

    @pl.when(jnp.logical_not(even) & (j > 0))
    def _():
        multiply(ha_ref)


def _mm_up_glu(x2d, g, sc, sh, w_up, conv_w, conv_b, seq_len, bm=1024, bn=512):
    m, k = x2d.shape
    d_ff = w_up.shape[1] // 2
    nbn = d_ff // bn
    nb = m // bm
    blocks_per_seq = seq_len // bm
    hb = bm // HALO
    last = m // HALO - 1
    new = lambda t: jnp.minimum(t, nb - 1)
    cur = lambda t: jnp.maximum(t - 1, 0)
    return pl.pallas_call(
        functools.partial(_mm_up_glu_kernel, blocks_per_seq=blocks_per_seq, n_blocks=nb),
        grid=(nb + 1, nbn),
        in_specs=[pl.BlockSpec((bm, k), lambda t, j: (new(t), 0)),
                  pl.BlockSpec((HALO, k), lambda t, j: (jnp.maximum(new(t) * hb - 1, 0), 0)),
                  pl.BlockSpec((HALO, k),
                               lambda t, j: (jnp.minimum((new(t) + 1) * hb, last), 0)),
                  pl.BlockSpec((1, k), lambda t, j: (0, 0)),
                  pl.BlockSpec((1, 1, k), lambda t, j: (new(t) // blocks_per_seq, 0, 0)),
                  pl.BlockSpec((1, 1, k), lambda t, j: (new(t) // blocks_per_seq, 0, 0)),
                  pl.BlockSpec((k, bn), lambda t, j: (0, j)),
                  pl.BlockSpec((k, bn), lambda t, j: (0, nbn + j)),
                  pl.BlockSpec((3, bn), lambda t, j: (0, j)),
                  pl.BlockSpec((1, bn), lambda t, j: (0, j))],
        out_specs=pl.BlockSpec((bm, bn), lambda t, j: (cur(t), jnp.where(t == 0, 0, j))),
        out_shape=jax.ShapeDtypeStruct((m, d_ff), BF16),
        scratch_shapes=[pltpu.VMEM((bm + 2 * HALO, k), BF16),
                        pltpu.VMEM((bm + 2 * HALO, k), BF16)],
        compiler_params=_cparams(("arbitrary", "arbitrary")),
        name="mm_up_glu",
    )(x2d, x2d, x2d, g, sc, sh, w_up, w_up, conv_w, conv_b)


def _mm_down_kernel(g_ref, w_ref, x_ref, gt_ref, gf_ref, o_ref):
    acc = jnp.dot(g_ref[...], w_ref[...], preferred_element_type=F32)
    x = x_ref[...] + gt_ref[0] * acc
    ms = jnp.mean(x * x, axis=-1, keepdims=True)
    o_ref[...] = x * lax.rsqrt(ms + EPS) * gf_ref[...]


def _mm_down(g2d, w, x2d, gt, g_final, seq_len, bm=256):
    m, k = g2d.shape
    n = w.shape[1]
    blocks_per_seq = seq_len // bm
    return pl.pallas_call(
        _mm_down_kernel,
        grid=(m // bm,),
        in_specs=[pl.BlockSpec((bm, k), lambda i: (i, 0)),
                  _resident((k, n)),
                  pl.BlockSpec((bm, n), lambda i: (i, 0)),
                  pl.BlockSpec((1, 1, n), lambda i: (i // blocks_per_seq, 0, 0)),
                  _resident((1, n))],
        out_specs=pl.BlockSpec((bm, n), lambda i: (i, 0)),
        out_shape=jax.ShapeDtypeStruct((m, n), F32),
        compiler_params=_cparams(("parallel",)),
        name="mm_down",
    )(g2d, w, x2d, gt, g_final)


def _position_features(seq_len, kpad):
    t = np.linspace(0.0, 1.0, seq_len)[:, None]
    bands = (FILTER_EMB - 1) // 2
    w = 2.0 * np.pi * np.arange(seq_len)[:, None] / seq_len
    fr = np.linspace(1e-4, bands - 1, bands)[None, :]
    z = np.concatenate([t, np.cos(fr * w), -np.sin(fr * w)], axis=-1)
    z = np.pad(z, ((0, 0), (0, kpad - z.shape[1])))
    return z.astype(np.float32)


def kernel(x, c, w_ada, b_ada, g_mix, w_in, na_rpb, hy_short_w, hy_short_b,
           hy_filt_w1, hy_filt_b1, hy_filt_w2, hy_filt_b2, hy_filt_w3, hy_filt_b3,
           hy_filt_w4, hy_filt_freq, hy_bias, beta_na, beta_hy, w_out, g_ffn,
           w_up, ffn_conv_w, ffn_conv_b, w_down, g_final):
    b, l, d = x.shape
    depth = w_ada.shape[0]
    d_hy = d - D_NA
    d_ff = w_down.shape[1]
    rows = l // GRID_W
    m = b * l
    assert depth == 1 and b % 2 == 0 and 2 * l == FFT_R * FFT_S

    c_pad = jnp.pad(c, ((0, 8 - b), (0, 0)))
    mod = _adaln(c_pad, w_ada[0], b_ada[0][None, :])[:b]
    sh1, sc1, gt1, sh2, sc2, gt2 = [t[:, None, :] for t in jnp.split(mod, 6, axis=-1)]

    x2d = x.reshape(m, d)

    col_scale = np.ones((w_in.shape[2],), np.float32)
    col_scale[:D_NA] = NA_Q_SCALE
    w_in_bf = (w_in[0] * jnp.asarray(col_scale)[None, :]).astype(BF16)
    proj = _mm_norm(x2d, g_mix, sc1, sh1, w_in_bf, l)

    bias_tab = _bias_table(na_rpb[0].reshape(-1))
    na = _na_attention(proj.reshape(b, l, proj.shape[1]), bias_tab)
    na2d = na.reshape(m, D_NA)

    x0, vx = _hy_pre(proj.reshape(b, l, proj.shape[1]), hy_short_w[0], hy_short_b[0][None, :], d_hy)

    kpad = V7X_LANES
    z_np = _position_features(l, kpad)
    w1p = jnp.pad(hy_filt_w1[0], ((0, kpad - FILTER_EMB), (0, 0)))
    max_decay = math.log(DECAY_TARGET) / FAST_DECAY_PCT
    min_decay = math.log(DECAY_TARGET) / SLOW_DECAY_PCT
    deltas_abs = np.abs(np.linspace(min_decay, max_decay, d_hy))[None, :].astype(np.float32)
    taps = _filter_taps(jnp.asarray(z_np), w1p, hy_filt_b1,
                        hy_filt_w2[0], hy_filt_b2, hy_filt_w3[0], hy_filt_b3,
                        hy_filt_freq, hy_filt_w4[0], jnp.asarray(deltas_abs), d_hy)

    w1_data, w1_filt, w3, e_fwd, e_inv = _fft_constants(l)
    a_filt = _fft_slow(w1_filt, taps.reshape(2, FFT_S // 2, FFT_R, d_hy), BF16)
    a_data = _fft_slow(w1_data, vx.reshape(b // 2, FFT_S, FFT_R, d_hy), BF16)
    g_spec = _fft_spec(e_fwd, e_inv, a_filt.reshape(2, 2, FFT_S, FFT_R, d_hy),
                       a_data.reshape(b // 2, 2, FFT_S, FFT_R, d_hy))
    y = _fft_slow(w3, g_spec.reshape(b // 2, 2 * FFT_S, FFT_R, d_hy), F32)
    y2d = y.reshape(m, d_hy)

    x1 = _mm_out(na2d, y2d, vx.reshape(m, d_hy), x0.reshape(m, d_hy), hy_bias,
                 beta_na, beta_hy, w_out[0].astype(BF16), x2d, gt1, l)

    gl = _mm_up_glu(x1, g_ffn, sc2, sh2, w_up[0], ffn_conv_w[0],
                    ffn_conv_b[0][None, :], l)
    out = _mm_down(gl, w_down[0].astype(BF16), x1, gt2, g_final[None, :], l)
    return out.reshape(b, l, d)
```

```python
import functools
import math

import numpy as np
import jax
import jax.numpy as jnp
from jax import lax
from jax.experimental import pallas as pl
from jax.experimental.pallas import tpu as pltpu

F32 = jnp.float32
BF16 = jnp.bfloat16

GRID_W = 64
NA_HEADS = 16
NA_HEAD_DIM = 64
D_NA = NA_HEADS * NA_HEAD_DIM
NA_WIN_ROWS = 8
NA_WIN_COLS = 16
NA_GROUP = 4
FILTER_EMB = 33
DECAY_TARGET = 1e-2
FAST_DECAY_PCT = 0.3
SLOW_DECAY_PCT = 1.5
EPS = 1e-6
NEG_BIG = -1e30
LOG2_E = math.log2(math.e)
NA_Q_SCALE = NA_HEAD_DIM ** -0.5 * LOG2_E

V7X_LANES = 128
V7X_VMEM_BYTES = 64 * 1024 * 1024
VMEM_LIMIT = 56 * 1024 * 1024

FFT_R = 128
FFT_S = 64

ROW_CHUNK = 64


def _cparams(sem, vmem=VMEM_LIMIT):
    return pltpu.CompilerParams(dimension_semantics=sem, vmem_limit_bytes=vmem)


def _for_row_chunks(n_rows, chunk, fn):
    def body(i, carry):
        fn(pl.ds(pl.multiple_of(i * chunk, chunk), chunk))
        return carry
    lax.fori_loop(0, n_rows // chunk, body, 0)


def _adaln_kernel(c_ref, w_ref, b_ref, o_ref):
    c = c_ref[...]
    cond = c / (1.0 + jnp.exp(-c))
    o_ref[...] = jnp.dot(cond.astype(BF16), w_ref[...].astype(BF16),
                         preferred_element_type=F32) + b_ref[...]


def _adaln(c_pad, w_ada, b_ada, bn=1024):
    rows, d = c_pad.shape
    n = w_ada.shape[1]
    return pl.pallas_call(
        _adaln_kernel,
        grid=(n // bn,),
        in_specs=[pl.BlockSpec((rows, d), lambda j: (0, 0)),
                  pl.BlockSpec((d, bn), lambda j: (0, j)),
                  pl.BlockSpec((1, bn), lambda j: (0, j))],
        out_specs=pl.BlockSpec((rows, bn), lambda j: (0, j)),
        out_shape=jax.ShapeDtypeStruct((rows, n), F32),
        compiler_params=_cparams(("parallel",)),
        name="adaln",
    )(c_pad, w_ada, b_ada)


def _skewed_steps(ha_ref, hb_ref, step):
    t = pl.program_id(0)

    @pl.when(t == 0)
    def _():
        hb_ref[...] = jnp.zeros_like(hb_ref)

    @pl.when(t % 2 == 0)
    def _():
        step(ha_ref, hb_ref)

    @pl.when(t % 2 == 1)
    def _():
        step(hb_ref, ha_ref)


def _resident(shape):
    return pl.BlockSpec(shape, lambda t: tuple(0 for _ in shape), pipeline_mode=pl.Buffered(1))


def _mm_norm_kernel(x_ref, g_ref, sc_ref, sh_ref, w_ref, o_ref, ha_ref, hb_ref):
    bm = x_ref.shape[0]

    def step(h_new, h_cur):
        gs = g_ref[...] * (1.0 + sc_ref[0])
        sh = sh_ref[0]
        for r0 in range(0, bm, ROW_CHUNK):
            x = x_ref[r0:r0 + ROW_CHUNK, :]
            ms = jnp.mean(x * x, axis=-1, keepdims=True)
            h_new[r0:r0 + ROW_CHUNK, :] = (x * lax.rsqrt(ms + EPS) * gs + sh).astype(BF16)
        o_ref[...] = jnp.dot(h_cur[...], w_ref[...],
                             preferred_element_type=F32).astype(o_ref.dtype)

    _skewed_steps(ha_ref, hb_ref, step)


def _mm_norm(x2d, g, sc, sh, w, seq_len, bm=256):
    m, k = x2d.shape
    n = w.shape[1]
    nb = m // bm
    blocks_per_seq = seq_len // bm
    new = lambda t: jnp.minimum(t, nb - 1)
    cur = lambda t: jnp.maximum(t - 1, 0)
    return pl.pallas_call(
        _mm_norm_kernel,
        grid=(nb + 1,),
        in_specs=[pl.BlockSpec((bm, k), lambda t: (new(t), 0)),
                  _resident((1, k)),
                  pl.BlockSpec((1, 1, k), lambda t: (new(t) // blocks_per_seq, 0, 0)),
                  pl.BlockSpec((1, 1, k), lambda t: (new(t) // blocks_per_seq, 0, 0)),
                  _resident((k, n))],
        out_specs=pl.BlockSpec((bm, n), lambda t: (cur(t), 0)),
        out_shape=jax.ShapeDtypeStruct((m, n), BF16),
        scratch_shapes=[pltpu.VMEM((bm, k), BF16), pltpu.VMEM((bm, k), BF16)],
        compiler_params=_cparams(("arbitrary",)),
        name="mm_norm",
    )(x2d, g, sc, sh, w)


def _bias_kernel(rpb_ref, o_ref):
    h = pl.program_id(0)
    n_rows = 2 * NA_WIN_ROWS - 1
    n_cols = 2 * NA_WIN_COLS - 1
    shape = (GRID_W, 2 * GRID_W)
    lane = lax.broadcasted_iota(jnp.int32, shape, 1)
    cq = lax.broadcasted_iota(jnp.int32, shape, 0)
    ck = lane & (GRID_W - 1)
    first = lane < GRID_W
    cs = jnp.clip(cq - NA_WIN_COLS // 2, 0, GRID_W - NA_WIN_COLS)
    valid = (ck >= cs) & (ck < cs + NA_WIN_COLS)
    d = jnp.clip(ck - cq, -(NA_WIN_COLS - 1), NA_WIN_COLS - 1) + (NA_WIN_COLS - 1)
    pair = []
    for j in range(n_rows - 1):
        base0 = (h * n_rows + j) * n_cols
        base1 = base0 + n_cols
        acc = jnp.zeros(shape, F32)
        for dd in range(n_cols):
            val = jnp.where(first, rpb_ref[base0 + dd], rpb_ref[base1 + dd])
            acc = jnp.where(d == dd, val, acc)
        pair.append(jnp.where(valid, acc * LOG2_E, NEG_BIG))
    for w in range(NA_WIN_ROWS):
        for ip in range(NA_WIN_ROWS // 2):
            o_ref[w, :, ip * 2 * GRID_W:(ip + 1) * 2 * GRID_W] = pair[w + 2 * ip]


def _bias_table(rpb_flat):
    return pl.pallas_call(
        _bias_kernel,
        grid=(NA_HEADS,),
        in_specs=[pl.BlockSpec(memory_space=pltpu.SMEM)],
        out_specs=pl.BlockSpec((None, NA_WIN_ROWS, GRID_W, NA_WIN_ROWS * GRID_W),
                               lambda h: (h, 0, 0, 0)),
        out_shape=jax.ShapeDtypeStruct(
            (NA_HEADS, NA_WIN_ROWS, GRID_W, NA_WIN_ROWS * GRID_W), F32),
        compiler_params=_cparams(("parallel",)),
        name="na_bias",
    )(rpb_flat)


NA_ROWS_PER_STEP = 4


def _na_row_start(r, rows):
    return jnp.clip(r - NA_WIN_ROWS // 2, 0, rows - NA_WIN_ROWS)


def _na_window_start(r0, rows):
    span = NA_WIN_ROWS + NA_ROWS_PER_STEP - 1
    return jnp.clip(r0 - NA_WIN_ROWS // 2, 0, rows - span)


def _na_kernel(q_ref, k_ref, v_ref, bias_ref, o_ref, *, rows):
    n_keys = NA_WIN_ROWS * GRID_W
    gw = NA_GROUP * NA_HEAD_DIM
    lane_head = lax.broadcasted_iota(jnp.int32, (GRID_W, gw), 1) // NA_HEAD_DIM
    r0 = pl.program_id(1) * NA_ROWS_PER_STEP
    win0 = _na_window_start(r0, rows)
    for j in range(NA_ROWS_PER_STEP):
        r = r0 + j
        rs = _na_row_start(r, rows)
        ks = pl.ds(pl.multiple_of((rs - win0) * GRID_W, GRID_W), n_keys)
        w = rs - r + (NA_WIN_ROWS - 1)
        qs = slice(j * GRID_W, (j + 1) * GRID_W)
        for g in range(NA_HEADS // NA_GROUP):
            cs = slice(g * gw, (g + 1) * gw)
            qg = q_ref[qs, cs]
            kg = k_ref[ks, cs]
            vg = v_ref[ks, cs]
            zero = jnp.zeros_like(qg)
            q4 = jnp.concatenate(
                [jnp.where(lane_head == h, qg, zero) for h in range(NA_GROUP)], axis=0)
            s = lax.dot_general(q4, kg, (((1,), (1,)), ((), ())), preferred_element_type=F32)
            bias = bias_ref[NA_GROUP * g:NA_GROUP * (g + 1), w]
            s = s + bias.reshape(NA_GROUP * GRID_W, n_keys)
            m = jnp.max(s, axis=-1, keepdims=True)
            p = jnp.exp2(s - m)
            l = jnp.sum(p, axis=-1, keepdims=True)
            o4 = jnp.dot(p.astype(BF16), vg, preferred_element_type=F32) / l
            o = o4[0:GRID_W]
            for h in range(1, NA_GROUP):
                o = jnp.where(lane_head == h, o4[h * GRID_W:(h + 1) * GRID_W], o)
            o_ref[qs, cs] = o.astype(o_ref.dtype)


def _na_attention(proj3, bias_tab):
    b, l, _ = proj3.shape
    rows = l // GRID_W
    rb = NA_ROWS_PER_STEP
    span = NA_WIN_ROWS + rb - 1

    def kv_spec(col0):
        return pl.BlockSpec((None, pl.Element(span * GRID_W), pl.Element(D_NA)),
                            lambda bi, t: (bi, _na_window_start(t * rb, rows) * GRID_W, col0))

    return pl.pallas_call(
        functools.partial(_na_kernel, rows=rows),
        grid=(b, rows // rb),
        in_specs=[
            pl.BlockSpec((None, rb * GRID_W, D_NA), lambda bi, t: (bi, t, 0)),
            kv_spec(D_NA),
            kv_spec(2 * D_NA),
            pl.BlockSpec(bias_tab.shape, lambda bi, t: (0, 0, 0, 0),
                         pipeline_mode=pl.Buffered(1)),
        ],
        out_specs=pl.BlockSpec((None, rb * GRID_W, D_NA), lambda bi, t: (bi, t, 0)),
        out_shape=jax.ShapeDtypeStruct((b, l, D_NA), BF16),
        compiler_params=_cparams(("parallel", "arbitrary")),
        name="na_attn",
    )(proj3, proj3, proj3, bias_tab)


def _conv3(u, w, b):
    n, c = u.shape
    t8 = 8
    row = lax.broadcasted_iota(jnp.int32, (t8, c), 0)
    prev = pltpu.roll(u, 1, axis=0)
    prev = jnp.concatenate([jnp.where(row == 0, 0.0, prev[0:t8]), prev[t8:]], axis=0)
    nxt = pltpu.roll(u, n - 1, axis=0)
    nxt = jnp.concatenate([nxt[0:n - t8], jnp.where(row == t8 - 1, 0.0, nxt[n - t8:])], axis=0)
    return prev * w[0:1] + u * w[1:2] + nxt * w[2:3] + b


def _hy_pre_kernel(u0_ref, u1_ref, u2_ref, w0_ref, w1_ref, w2_ref,
                   b0_ref, b1_ref, b2_ref, x0_ref, vx_ref):
    x0 = _conv3(u0_ref[...].astype(F32), w0_ref[...], b0_ref[...])
    x1 = _conv3(u1_ref[...].astype(F32), w1_ref[...], b1_ref[...])
    v = _conv3(u2_ref[...].astype(F32), w2_ref[...], b2_ref[...])
    x0_ref[...] = x0.astype(x0_ref.dtype)
    vx_ref[...] = (v * x1).astype(vx_ref.dtype)


def _hy_pre(proj3, short_w, short_b, d_hy, cb=128):
    b, l, _ = proj3.shape
    base = 3 * D_NA // cb
    per = d_hy // cb

    def u_spec(g):
        return pl.BlockSpec((None, l, cb), lambda bi, j: (bi, 0, base + g * per + j))

    def w_spec(g):
        return pl.BlockSpec((3, cb), lambda bi, j: (0, g * per + j))

    def b_spec(g):
        return pl.BlockSpec((1, cb), lambda bi, j: (0, g * per + j))

    out_spec = pl.BlockSpec((None, l, cb), lambda bi, j: (bi, 0, j))
    return pl.pallas_call(
        _hy_pre_kernel,
        grid=(b, per),
        in_specs=[u_spec(0), u_spec(1), u_spec(2), w_spec(0), w_spec(1), w_spec(2),
                  b_spec(0), b_spec(1), b_spec(2)],
        out_specs=[out_spec, out_spec],
        out_shape=[jax.ShapeDtypeStruct((b, l, d_hy), BF16),
                   jax.ShapeDtypeStruct((b, l, d_hy), BF16)],
        compiler_params=_cparams(("parallel", "parallel")),
        name="hy_pre",
    )(proj3, proj3, proj3, short_w, short_w, short_w, short_b, short_b, short_b)


def _filter_kernel(z_ref, w1_ref, b1_ref, w2_ref, b2_ref, w3_ref, b3_ref,
                   fq_ref, w4f_ref, w4b_ref, dl_ref, db_ref, o_ref, h_ref):
    hi = lax.Precision.HIGHEST
    seq = z_ref.shape[0]

    @pl.when(pl.program_id(0) == 0)
    def _():
        fq = fq_ref[...]

        def mlp(z):
            h = jnp.sin(fq * (jnp.dot(z, w1_ref[...], precision=hi,
                                      preferred_element_type=F32) + b1_ref[...]))
            h = jnp.sin(fq * (jnp.dot(h, w2_ref[...], precision=hi,
                                      preferred_element_type=F32) + b2_ref[...]))
            h = jnp.sin(fq * (jnp.dot(h, w3_ref[...], precision=hi,
                                      preferred_element_type=F32) + b3_ref[...]))
            return h

        def chunk(rs):
            h_ref[rs, :] = mlp(z_ref[rs, :]).astype(h_ref.dtype)

        _for_row_chunks(seq, 4 * ROW_CHUNK, chunk)

    decay = jnp.exp(-z_ref[:, 0:1] * dl_ref[...])
    h3 = h_ref[...]
    fwd = jnp.dot(h3, w4f_ref[...].astype(BF16), preferred_element_type=F32) * decay
    bwd = jnp.dot(h3, w4b_ref[...].astype(BF16), preferred_element_type=F32) * decay
    row = lax.broadcasted_iota(jnp.int32, bwd.shape, 0)
    bwd = jnp.where(row == 0, 0.0, bwd)
    fwd = jnp.where(row == 0, fwd + db_ref[...], fwd)
    o_ref[0:seq, :] = fwd.astype(o_ref.dtype)
    o_ref[seq:2 * seq, :] = bwd.astype(o_ref.dtype)


def _filter_taps(z, w1p, b1, w2, b2, w3, b3, freq, w4, deltas_abs, d_bias, d_hy, cb=256):
    seq, kpad = z.shape
    order = w2.shape[0]
    per = d_hy // cb
    full = lambda shape: pl.BlockSpec(shape, lambda j: tuple(0 for _ in shape))
    return pl.pallas_call(
        _filter_kernel,
        grid=(per,),
        in_specs=[full((seq, kpad)),
                  full((kpad, order)), full((1, order)),
                  full((order, order)), full((1, order)),
                  full((order, order)), full((1, order)),
                  full((1, order)),
                  pl.BlockSpec((order, cb), lambda j: (0, j)),
                  pl.BlockSpec((order, cb), lambda j: (0, per + j)),
                  pl.BlockSpec((1, cb), lambda j: (0, j)),
                  pl.BlockSpec((1, cb), lambda j: (0, j))],
        out_specs=pl.BlockSpec((2 * seq, cb), lambda j: (0, j)),
        out_shape=jax.ShapeDtypeStruct((2 * seq, d_hy), BF16),
        scratch_shapes=[pltpu.VMEM((seq, order), BF16)],
        compiler_params=_cparams(("arbitrary",)),
        name="hy_filter",
    )(z, w1p, b1, w2, b2, w3, b3, freq, w4, w4, deltas_abs, d_bias)


V7X_SUBLANES = 8


def _fft_slow_kernel(wk_ref, x_ref, o_ref):
    k, rb, c = x_ref.shape
    m = o_ref.shape[0]
    x = x_ref[...].astype(F32)
    outs = []
    for t in range(rb // V7X_SUBLANES):
        xs = x[:, t * V7X_SUBLANES:(t + 1) * V7X_SUBLANES, :]
        xs = xs.reshape(k * V7X_SUBLANES, c).astype(BF16)
        o = jnp.dot(wk_ref[...], xs, preferred_element_type=F32)
        outs.append(o.reshape(m, V7X_SUBLANES, c))
    o_ref[...] = jnp.concatenate(outs, axis=1).astype(o_ref.dtype)


def _fft_slow_gate_kernel(wk_ref, x_ref, x0_ref, beta_ref, o_ref):
    k, rb, c = x_ref.shape
    m = o_ref.shape[0]
    x = x_ref[...].astype(F32)
    outs = []
    for t in range(rb // V7X_SUBLANES):
        xs = x[:, t * V7X_SUBLANES:(t + 1) * V7X_SUBLANES, :]
        xs = xs.reshape(k * V7X_SUBLANES, c).astype(BF16)
        o = jnp.dot(wk_ref[...], xs, preferred_element_type=F32)
        outs.append(o.reshape(m, V7X_SUBLANES, c))
    hy = jnp.concatenate(outs, axis=1) * x0_ref[...].astype(F32)
    ms = jnp.mean(hy * hy, axis=-1, keepdims=True)
    o_ref[...] = (hy * lax.rsqrt(ms + EPS) * beta_ref[...]).astype(o_ref.dtype)


def _fft_slow_gate(wk, x4, x04, beta, rb=16):
    g, k, r, c = x4.shape
    m = wk.shape[0] // V7X_SUBLANES
    assert wk.shape[1] == k * V7X_SUBLANES and x04.shape == (g, m, r, c)
    return pl.pallas_call(
        _fft_slow_gate_kernel,
        grid=(g, r // rb),
        in_specs=[pl.BlockSpec(wk.shape, lambda gi, j: (0, 0)),
                  pl.BlockSpec((None, k, rb, c), lambda gi, j: (gi, 0, j, 0)),
                  pl.BlockSpec((None, m, rb, c), lambda gi, j: (gi, 0, j, 0)),
                  pl.BlockSpec((1, c), lambda gi, j: (0, 0))],
        out_specs=pl.BlockSpec((None, m, rb, c), lambda gi, j: (gi, 0, j, 0)),
        out_shape=jax.ShapeDtypeStruct((g, m, r, c), BF16),
        compiler_params=_cparams(("parallel", "parallel")),
        name="fft_slow_gate",
    )(wk, x4, x04, beta)


def _fft_slow(wk, x4, out_dtype, rb=16):
    g, k, r, c = x4.shape
    m = wk.shape[0] // V7X_SUBLANES
    assert wk.shape[1] == k * V7X_SUBLANES
    return pl.pallas_call(
        _fft_slow_kernel,
        grid=(g, r // rb),
        in_specs=[pl.BlockSpec(wk.shape, lambda gi, j: (0, 0)),
                  pl.BlockSpec((None, k, rb, c), lambda gi, j: (gi, 0, j, 0))],
        out_specs=pl.BlockSpec((None, m, rb, c), lambda gi, j: (gi, 0, j, 0)),
        out_shape=jax.ShapeDtypeStruct((g, m, r, c), out_dtype),
        compiler_params=_cparams(("parallel", "parallel")),
        name="fft_slow",
    )(wk, x4)


SPEC_SLABS = 2


def _fft_spec_kernel(e_ref, ei_ref, af_ref, a_ref, o_ref):
    pairs, two, slabs, r, c = a_ref.shape
    for q in range(slabs):
        e = e_ref[q]
        hf = jnp.dot(e, af_ref[0, :, q].reshape(two * r, c), preferred_element_type=F32)
        hb = jnp.dot(e, af_ref[1, :, q].reshape(two * r, c), preferred_element_type=F32)
        kr = hf[0:r] + hb[0:r]
        ki = hf[r:2 * r] - hb[r:2 * r]
        for p in range(pairs):
            b2 = jnp.dot(e, a_ref[p, :, q].reshape(two * r, c), preferred_element_type=F32)
            br, bi = b2[0:r], b2[r:2 * r]
            y2 = jnp.concatenate([br * kr - bi * ki, br * ki + bi * kr], axis=0).astype(BF16)
            g2 = jnp.dot(ei_ref[q], y2, preferred_element_type=F32)
            o_ref[p, :, q] = g2.reshape(two, r, c).astype(o_ref.dtype)


def _fft_spec(e_fwd, e_inv, a_filt, a_data):
    pairs, two, s, r, c = a_data.shape
    q = SPEC_SLABS
    return pl.pallas_call(
        _fft_spec_kernel,
        grid=(s // q,),
        in_specs=[pl.BlockSpec((q, two * r, two * r), lambda i: (i, 0, 0)),
                  pl.BlockSpec((q, two * r, two * r), lambda i: (i, 0, 0)),
                  pl.BlockSpec((2, two, q, r, c), lambda i: (0, 0, i, 0, 0)),
                  pl.BlockSpec((pairs, two, q, r, c), lambda i: (0, 0, i, 0, 0))],
        out_specs=pl.BlockSpec((pairs, two, q, r, c), lambda i: (0, 0, i, 0, 0)),
        out_shape=jax.ShapeDtypeStruct((pairs, two, s, r, c), BF16),
        compiler_params=_cparams(("parallel",)),
        name="fft_spec",
    )(e_fwd, e_inv, a_filt, a_data)


def _fft_constants(seq_len):
    n = 2 * seq_len
    r_, s_ = FFT_R, FFT_S
    assert r_ * s_ == n
    half = s_ // 2
    ks = np.arange(s_)[:, None]
    s = np.arange(s_)[None, :]
    ang = 2.0 * np.pi * ((ks * s) % s_) / s_
    fr, fi = np.cos(ang), -np.sin(ang)
    w1_data = np.block([[fr[:, :half], -fi[:, :half]], [fi[:, :half], fr[:, :half]]])
    w1_filt = np.block([[fr[:, :half]], [fi[:, :half]]])
    so = np.arange(half)[:, None]
    ko = np.arange(s_)[None, :]
    ang_i = 2.0 * np.pi * ((so * ko) % s_) / s_
    cr, ci = np.cos(ang_i), np.sin(ang_i)
    w3 = np.block([[cr, -ci], [ci, cr]]) / n
    ksv = np.arange(s_)[:, None, None]
    kr = np.arange(r_)[None, :, None]
    r = np.arange(r_)[None, None, :]
    ang_e = 2.0 * np.pi * ((r * (ksv + s_ * kr)) % n) / n
    er, ei = np.cos(ang_e), -np.sin(ang_e)
    e_fwd = np.concatenate([np.concatenate([er, -ei], axis=2),
                            np.concatenate([ei, er], axis=2)], axis=1)
    e_inv = np.transpose(e_fwd, (0, 2, 1))
    as_bf16 = lambda a: jnp.asarray(a.astype(np.float32)).astype(BF16)
    eye = np.eye(V7X_SUBLANES)
    slow = [as_bf16(np.kron(w, eye)) for w in (w1_data, w1_filt, w3)]
    return slow[0], slow[1], slow[2], as_bf16(e_fwd), as_bf16(e_inv)


def _mm_out_kernel(na_ref, hyn_ref, bna_ref, w_ref, x_ref, gt_ref, o_ref, ha_ref, hb_ref):
    bm, d_na = na_ref.shape

    def step(h_new, h_cur):
        for r0 in range(0, bm, ROW_CHUNK):
            rs = slice(r0, r0 + ROW_CHUNK)
            na = na_ref[rs, :].astype(F32)
            na_n = na * lax.rsqrt(jnp.mean(na * na, axis=-1, keepdims=True) + EPS) * bna_ref[...]
            h_new[rs, 0:d_na] = na_n.astype(BF16)
            h_new[rs, d_na:] = hyn_ref[rs, :]
        acc = jnp.dot(h_cur[...], w_ref[...], preferred_element_type=F32)
        o_ref[...] = x_ref[...] + gt_ref[0] * acc

    _skewed_steps(ha_ref, hb_ref, step)


def _mm_out(na2d, hyn2d, beta_na, w, x2d, gt, seq_len, bm=256):
    m, d_na = na2d.shape
    d_hy = hyn2d.shape[1]
    k, n = w.shape
    nb = m // bm
    blocks_per_seq = seq_len // bm
    new = lambda t: jnp.minimum(t, nb - 1)
    cur = lambda t: jnp.maximum(t - 1, 0)
    return pl.pallas_call(
        _mm_out_kernel,
        grid=(nb + 1,),
        in_specs=[pl.BlockSpec((bm, d_na), lambda t: (new(t), 0)),
                  pl.BlockSpec((bm, d_hy), lambda t: (new(t), 0)),
                  _resident((1, d_na)),
                  _resident((k, n)),
                  pl.BlockSpec((bm, n), lambda t: (cur(t), 0)),
                  pl.BlockSpec((1, 1, n), lambda t: (cur(t) // blocks_per_seq, 0, 0))],
        out_specs=pl.BlockSpec((bm, n), lambda t: (cur(t), 0)),
        out_shape=jax.ShapeDtypeStruct((m, n), F32),
        scratch_shapes=[pltpu.VMEM((bm, k), BF16), pltpu.VMEM((bm, k), BF16)],
        compiler_params=_cparams(("arbitrary",)),
        name="mm_out",
    )(na2d, hyn2d, beta_na, w, x2d, gt)


HALO = 16


def _mm_up_glu_kernel(x_ref, xp_ref, xn_ref, g_ref, sc_ref, sh_ref, wa_ref, wb_ref,
                      cw_ref, cb_ref, o_ref, ha_ref, hb_ref, *, blocks_per_seq, n_blocks):
    bm, k = x_ref.shape
    t = pl.program_id(0)
    j = pl.program_id(1)
    pos = jnp.minimum(t, n_blocks - 1) % blocks_per_seq

    def norm(x):
        gs = g_ref[...] * (1.0 + sc_ref[0])
        ms = jnp.mean(x * x, axis=-1, keepdims=True)
        return x * lax.rsqrt(ms + EPS) * gs + sh_ref[0]

    def build(h_new):
        zero = jnp.zeros((HALO, k), F32)
        h_new[0:HALO, :] = jnp.where(pos == 0, zero, norm(xp_ref[...])).astype(BF16)
        for r0 in range(0, bm, ROW_CHUNK):
            h_new[HALO + r0:HALO + r0 + ROW_CHUNK, :] = norm(
                x_ref[r0:r0 + ROW_CHUNK, :]).astype(BF16)
        h_new[HALO + bm:, :] = jnp.where(pos == blocks_per_seq - 1, zero,
                                         norm(xn_ref[...])).astype(BF16)

    def multiply(h_cur):
        a = jnp.dot(h_cur[...], wa_ref[...].astype(BF16), preferred_element_type=F32)
        b = jnp.dot(h_cur[HALO:HALO + bm, :], wb_ref[...].astype(BF16),
                    preferred_element_type=F32)
        n_ext = a.shape[0]
        w = cw_ref[...]
        prev = pltpu.roll(a, 1, axis=0)[HALO:HALO + bm]
        nxt = pltpu.roll(a, n_ext - 1, axis=0)[HALO:HALO + bm]
        ac = prev * w[0:1] + a[HALO:HALO + bm] * w[1:2] + nxt * w[2:3] + cb_ref[...]
        gelu = 0.5 * ac * (1.0 + lax.erf(ac * (1.0 / math.sqrt(2.0))))
        o_ref[...] = (gelu * b).astype(o_ref.dtype)

    even = t % 2 == 0

    @pl.when((t == 0) & (j == 0))
    def _():
        build(ha_ref)

    @pl.when((t > 0) & even & (j == 0))
    def _():
        build(ha_ref)
        multiply(hb_ref)

    @pl.when((t > 0) & even & (j > 0))
    def _():
        multiply(hb_ref)

    @pl.when(jnp.logical_not(even) & (j == 0))
    def _():
        build(hb_ref)
        multiply(ha_ref)

    @pl.when(jnp.logical_not(even) & (j > 0))
    def _():
        multiply(ha_ref)


def _mm_up_glu(x2d, g, sc, sh, w_up, conv_w, conv_b, seq_len, bm=1024, bn=512):
    m, k = x2d.shape
    d_ff = w_up.shape[1] // 2
    nbn = d_ff // bn
    nb = m // bm
    blocks_per_seq = seq_len // bm
    hb = bm // HALO
    last = m // HALO - 1
    new = lambda t: jnp.minimum(t, nb - 1)
    cur = lambda t: jnp.maximum(t - 1, 0)
    return pl.pallas_call(
        functools.partial(_mm_up_glu_kernel, blocks_per_seq=blocks_per_seq, n_blocks=nb),
        grid=(nb + 1, nbn),
        in_specs=[pl.BlockSpec((bm, k), lambda t, j: (new(t), 0)),
                  pl.BlockSpec((HALO, k), lambda t, j: (jnp.maximum(new(t) * hb - 1, 0), 0)),
                  pl.BlockSpec((HALO, k),
                               lambda t, j: (jnp.minimum((new(t) + 1) * hb, last), 0)),
                  pl.BlockSpec((1, k), lambda t, j: (0, 0)),
                  pl.BlockSpec((1, 1, k), lambda t, j: (new(t) // blocks_per_seq, 0, 0)),
                  pl.BlockSpec((1, 1, k), lambda t, j: (new(t) // blocks_per_seq, 0, 0)),
                  pl.BlockSpec((k, bn), lambda t, j: (0, j)),
                  pl.BlockSpec((k, bn), lambda t, j: (0, nbn + j)),
                  pl.BlockSpec((3, bn), lambda t, j: (0, j)),
                  pl.BlockSpec((1, bn), lambda t, j: (0, j))],
        out_specs=pl.BlockSpec((bm, bn), lambda t, j: (cur(t), jnp.where(t == 0, 0, j))),
        out_shape=jax.ShapeDtypeStruct((m, d_ff), BF16),
        scratch_shapes=[pltpu.VMEM((bm + 2 * HALO, k), BF16),
                        pltpu.VMEM((bm + 2 * HALO, k), BF16)],
        compiler_params=_cparams(("arbitrary", "arbitrary")),
        name="mm_up_glu",
    )(x2d, x2d, x2d, g, sc, sh, w_up, w_up, conv_w, conv_b)


def _mm_down_kernel(g_ref, w_ref, x_ref, gt_ref, gf_ref, o_ref):
    acc = jnp.dot(g_ref[...], w_ref[...], preferred_element_type=F32)
    x = x_ref[...] + gt_ref[0] * acc
    ms = jnp.mean(x * x, axis=-1, keepdims=True)
    o_ref[...] = x * lax.rsqrt(ms + EPS) * gf_ref[...]


def _mm_down(g2d, w, x2d, gt, g_final, seq_len, bm=256):
    m, k = g2d.shape
    n = w.shape[1]
    blocks_per_seq = seq_len // bm
    return pl.pallas_call(
        _mm_down_kernel,
        grid=(m // bm,),
        in_specs=[pl.BlockSpec((bm, k), lambda i: (i, 0)),
                  _resident((k, n)),
                  pl.BlockSpec((bm, n), lambda i: (i, 0)),
                  pl.BlockSpec((1, 1, n), lambda i: (i // blocks_per_seq, 0, 0)),
                  _resident((1, n))],
        out_specs=pl.BlockSpec((bm, n), lambda i: (i, 0)),
        out_shape=jax.ShapeDtypeStruct((m, n), F32),
        compiler_params=_cparams(("parallel",)),
        name="mm_down",
    )(g2d, w, x2d, gt, g_final)


def _position_features(seq_len, kpad):
    t = np.linspace(0.0, 1.0, seq_len)[:, None]
    bands = (FILTER_EMB - 1) // 2
    w = 2.0 * np.pi * np.arange(seq_len)[:, None] / seq_len
    fr = np.linspace(1e-4, bands - 1, bands)[None, :]
    z = np.concatenate([t, np.cos(fr * w), -np.sin(fr * w)], axis=-1)
    z = np.pad(z, ((0, 0), (0, kpad - z.shape[1])))
    return z.astype(np.float32)


def kernel(x, c, w_ada, b_ada, g_mix, w_in, na_rpb, hy_short_w, hy_short_b,
           hy_filt_w1, hy_filt_b1, hy_filt_w2, hy_filt_b2, hy_filt_w3, hy_filt_b3,
           hy_filt_w4, hy_filt_freq, hy_bias, beta_na, beta_hy, w_out, g_ffn,
           w_up, ffn_conv_w, ffn_conv_b, w_down, g_final):
    b, l, d = x.shape
    depth = w_ada.shape[0]
    d_hy = d - D_NA
    d_ff = w_down.shape[1]
    rows = l // GRID_W
    m = b * l
    assert depth == 1 and b % 2 == 0 and 2 * l == FFT_R * FFT_S

    c_pad = jnp.pad(c, ((0, 8 - b), (0, 0)))
    mod = _adaln(c_pad, w_ada[0], b_ada[0][None, :])[:b]
    sh1, sc1, gt1, sh2, sc2, gt2 = [t[:, None, :] for t in jnp.split(mod, 6, axis=-1)]

    x2d = x.reshape(m, d)

    col_scale = np.ones((w_in.shape[2],), np.float32)
    col_scale[:D_NA] = NA_Q_SCALE
    w_in_bf = (w_in[0] * jnp.asarray(col_scale)[None, :]).astype(BF16)
    proj = _mm_norm(x2d, g_mix, sc1, sh1, w_in_bf, l)

    bias_tab = _bias_table(na_rpb[0].reshape(-1))
    na = _na_attention(proj.reshape(b, l, proj.shape[1]), bias_tab)
    na2d = na.reshape(m, D_NA)

    x0, vx = _hy_pre(proj.reshape(b, l, proj.shape[1]), hy_short_w[0], hy_short_b[0][None, :], d_hy)

    kpad = V7X_LANES
    z_np = _position_features(l, kpad)
    w1p = jnp.pad(hy_filt_w1[0], ((0, kpad - FILTER_EMB), (0, 0)))
    max_decay = math.log(DECAY_TARGET) / FAST_DECAY_PCT
    min_decay = math.log(DECAY_TARGET) / SLOW_DECAY_PCT
    deltas_abs = np.abs(np.linspace(min_decay, max_decay, d_hy))[None, :].astype(np.float32)
    taps = _filter_taps(jnp.asarray(z_np), w1p, hy_filt_b1,
                        hy_filt_w2[0], hy_filt_b2, hy_filt_w3[0], hy_filt_b3,
                        hy_filt_freq, hy_filt_w4[0], jnp.asarray(deltas_abs), hy_bias, d_hy)

    w1_data, w1_filt, w3, e_fwd, e_inv = _fft_constants(l)
    a_filt = _fft_slow(w1_filt, taps.reshape(2, FFT_S // 2, FFT_R, d_hy), BF16)
    a_data = _fft_slow(w1_data, vx.reshape(b // 2, FFT_S, FFT_R, d_hy), BF16)
    g_spec = _fft_spec(e_fwd, e_inv, a_filt.reshape(2, 2, FFT_S, FFT_R, d_hy),
                       a_data.reshape(b // 2, 2, FFT_S, FFT_R, d_hy))
    hyn = _fft_slow_gate(w3, g_spec.reshape(b // 2, 2 * FFT_S, FFT_R, d_hy),
                         x0.reshape(b // 2, FFT_S, FFT_R, d_hy), beta_hy)

    x1 = _mm_out(na2d, hyn.reshape(m, d_hy), beta_na, w_out[0].astype(BF16), x2d, gt1, l)

    gl = _mm_up_glu(x1, g_ffn, sc2, sh2, w_up[0], ffn_conv_w[0],
                    ffn_conv_b[0][None, :], l)
    out = _mm_down(gl, w_down[0].astype(BF16), x1, gt2, g_final[None, :], l)
    return out.reshape(b, l, d)
```

```python
import functools
import math

import numpy as np
import jax
import jax.numpy as jnp
from jax import lax
from jax.experimental import pallas as pl
from jax.experimental.pallas import tpu as pltpu

F32 = jnp.float32
BF16 = jnp.bfloat16

GRID_W = 64
NA_HEADS = 16
NA_HEAD_DIM = 64
D_NA = NA_HEADS * NA_HEAD_DIM
NA_WIN_ROWS = 8
NA_WIN_COLS = 16
NA_GROUP = 4
FILTER_EMB = 33
DECAY_TARGET = 1e-2
FAST_DECAY_PCT = 0.3
SLOW_DECAY_PCT = 1.5
EPS = 1e-6
NEG_BIG = -1e30
LOG2_E = math.log2(math.e)
NA_Q_SCALE = NA_HEAD_DIM ** -0.5 * LOG2_E

V7X_LANES = 128
V7X_VMEM_BYTES = 64 * 1024 * 1024
VMEM_LIMIT = 56 * 1024 * 1024

FFT_R = 128
FFT_S = 64

ROW_CHUNK = 64


def _cparams(sem, vmem=VMEM_LIMIT):
    return pltpu.CompilerParams(dimension_semantics=sem, vmem_limit_bytes=vmem)


def _for_row_chunks(n_rows, chunk, fn):
    def body(i, carry):
        fn(pl.ds(pl.multiple_of(i * chunk, chunk), chunk))
        return carry
    lax.fori_loop(0, n_rows // chunk, body, 0)


def _adaln_kernel(c_ref, w_ref, b_ref, o_ref):
    c = c_ref[...]
    cond = c / (1.0 + jnp.exp(-c))
    o_ref[...] = jnp.dot(cond.astype(BF16), w_ref[...].astype(BF16),
                         preferred_element_type=F32) + b_ref[...]


def _adaln(c_pad, w_ada, b_ada, bn=1024):
    rows, d = c_pad.shape
    n = w_ada.shape[1]
    return pl.pallas_call(
        _adaln_kernel,
        grid=(n // bn,),
        in_specs=[pl.BlockSpec((rows, d), lambda j: (0, 0)),
                  pl.BlockSpec((d, bn), lambda j: (0, j)),
                  pl.BlockSpec((1, bn), lambda j: (0, j))],
        out_specs=pl.BlockSpec((rows, bn), lambda j: (0, j)),
        out_shape=jax.ShapeDtypeStruct((rows, n), F32),
        compiler_params=_cparams(("parallel",)),
        name="adaln",
    )(c_pad, w_ada, b_ada)


def _skewed_steps(ha_ref, hb_ref, step):
    t = pl.program_id(0)

    @pl.when(t == 0)
    def _():
        hb_ref[...] = jnp.zeros_like(hb_ref)

    @pl.when(t % 2 == 0)
    def _():
        step(ha_ref, hb_ref)

    @pl.when(t % 2 == 1)
    def _():
        step(hb_ref, ha_ref)


def _resident(shape):
    return pl.BlockSpec(shape, lambda t: tuple(0 for _ in shape), pipeline_mode=pl.Buffered(1))


def _mm_norm_kernel(x_ref, g_ref, sc_ref, sh_ref, w_ref, o_ref, ha_ref, hb_ref):
    bm = x_ref.shape[0]

    def step(h_new, h_cur):
        gs = g_ref[...] * (1.0 + sc_ref[0])
        sh = sh_ref[0]
        for r0 in range(0, bm, ROW_CHUNK):
            x = x_ref[r0:r0 + ROW_CHUNK, :]
            ms = jnp.mean(x * x, axis=-1, keepdims=True)
            h_new[r0:r0 + ROW_CHUNK, :] = (x * lax.rsqrt(ms + EPS) * gs + sh).astype(BF16)
        o_ref[...] = jnp.dot(h_cur[...], w_ref[...],
                             preferred_element_type=F32).astype(o_ref.dtype)

    _skewed_steps(ha_ref, hb_ref, step)


def _mm_norm(x2d, g, sc, sh, w, seq_len, bm=256):
    m, k = x2d.shape
    n = w.shape[1]
    nb = m // bm
    blocks_per_seq = seq_len // bm
    new = lambda t: jnp.minimum(t, nb - 1)
    cur = lambda t: jnp.maximum(t - 1, 0)
    return pl.pallas_call(
        _mm_norm_kernel,
        grid=(nb + 1,),
        in_specs=[pl.BlockSpec((bm, k), lambda t: (new(t), 0)),
                  _resident((1, k)),
                  pl.BlockSpec((1, 1, k), lambda t: (new(t) // blocks_per_seq, 0, 0)),
                  pl.BlockSpec((1, 1, k), lambda t: (new(t) // blocks_per_seq, 0, 0)),
                  _resident((k, n))],
        out_specs=pl.BlockSpec((bm, n), lambda t: (cur(t), 0)),
        out_shape=jax.ShapeDtypeStruct((m, n), BF16),
        scratch_shapes=[pltpu.VMEM((bm, k), BF16), pltpu.VMEM((bm, k), BF16)],
        compiler_params=_cparams(("arbitrary",)),
        name="mm_norm",
    )(x2d, g, sc, sh, w)


def _bias_kernel(rpb_ref, o_ref):
    h = pl.program_id(0)
    n_rows = 2 * NA_WIN_ROWS - 1
    n_cols = 2 * NA_WIN_COLS - 1
    shape = (GRID_W, 2 * GRID_W)
    lane = lax.broadcasted_iota(jnp.int32, shape, 1)
    cq = lax.broadcasted_iota(jnp.int32, shape, 0)
    ck = lane & (GRID_W - 1)
    first = lane < GRID_W
    cs = jnp.clip(cq - NA_WIN_COLS // 2, 0, GRID_W - NA_WIN_COLS)
    valid = (ck >= cs) & (ck < cs + NA_WIN_COLS)
    d = jnp.clip(ck - cq, -(NA_WIN_COLS - 1), NA_WIN_COLS - 1) + (NA_WIN_COLS - 1)
    pair = []
    for j in range(n_rows - 1):
        base0 = (h * n_rows + j) * n_cols
        base1 = base0 + n_cols
        acc = jnp.zeros(shape, F32)
        for dd in range(n_cols):
            val = jnp.where(first, rpb_ref[base0 + dd], rpb_ref[base1 + dd])
            acc = jnp.where(d == dd, val, acc)
        pair.append(jnp.where(valid, acc * LOG2_E, NEG_BIG))
    for w in range(NA_WIN_ROWS):
        for ip in range(NA_WIN_ROWS // 2):
            o_ref[w, :, ip * 2 * GRID_W:(ip + 1) * 2 * GRID_W] = pair[w + 2 * ip]


def _bias_table(rpb_flat):
    return pl.pallas_call(
        _bias_kernel,
        grid=(NA_HEADS,),
        in_specs=[pl.BlockSpec(memory_space=pltpu.SMEM)],
        out_specs=pl.BlockSpec((None, NA_WIN_ROWS, GRID_W, NA_WIN_ROWS * GRID_W),
                               lambda h: (h, 0, 0, 0)),
        out_shape=jax.ShapeDtypeStruct(
            (NA_HEADS, NA_WIN_ROWS, GRID_W, NA_WIN_ROWS * GRID_W), F32),
        compiler_params=_cparams(("parallel",)),
        name="na_bias",
    )(rpb_flat)


NA_ROWS_PER_STEP = 4


def _na_row_start(r, rows):
    return jnp.clip(r - NA_WIN_ROWS // 2, 0, rows - NA_WIN_ROWS)


def _na_window_start(r0, rows):
    span = NA_WIN_ROWS + NA_ROWS_PER_STEP - 1
    return jnp.clip(r0 - NA_WIN_ROWS // 2, 0, rows - span)


def _cast_block(shape, n_steps):
    r, c = shape
    for f in (1, 2, 4, 8):
        row_blocks = n_steps // f
        if (n_steps % f == 0 and r % (2 * V7X_SUBLANES * row_blocks) == 0
                and c % (V7X_LANES * f) == 0):
            return (r // row_blocks, c // f), (lambda s, f=f: (s // f, s % f))
    raise ValueError(f"cannot walk {shape} in {n_steps} blocks")


def _na_kernel(q_ref, k_ref, v_ref, bias_ref, *rest, rows, n_cast):
    casts_in, o_ref, casts_out = rest[:n_cast], rest[n_cast], rest[n_cast + 1:]
    for src, dst in zip(casts_in, casts_out):
        dst[...] = src[...].astype(dst.dtype)
    _na_body(q_ref, k_ref, v_ref, bias_ref, o_ref, rows=rows)


def _na_body(q_ref, k_ref, v_ref, bias_ref, o_ref, *, rows):
    n_keys = NA_WIN_ROWS * GRID_W
    gw = NA_GROUP * NA_HEAD_DIM
    lane_head = lax.broadcasted_iota(jnp.int32, (GRID_W, gw), 1) // NA_HEAD_DIM
    r0 = pl.program_id(1) * NA_ROWS_PER_STEP
    win0 = _na_window_start(r0, rows)
    for j in range(NA_ROWS_PER_STEP):
        r = r0 + j
        rs = _na_row_start(r, rows)
        ks = pl.ds(pl.multiple_of((rs - win0) * GRID_W, GRID_W), n_keys)
        w = rs - r + (NA_WIN_ROWS - 1)
        qs = slice(j * GRID_W, (j + 1) * GRID_W)
        for g in range(NA_HEADS // NA_GROUP):
            cs = slice(g * gw, (g + 1) * gw)
            qg = q_ref[qs, cs]
            kg = k_ref[ks, cs]
            vg = v_ref[ks, cs]
            zero = jnp.zeros_like(qg)
            q4 = jnp.concatenate(
                [jnp.where(lane_head == h, qg, zero) for h in range(NA_GROUP)], axis=0)
            s = lax.dot_general(q4, kg, (((1,), (1,)), ((), ())), preferred_element_type=F32)
            bias = bias_ref[NA_GROUP * g:NA_GROUP * (g + 1), w]
            s = s + bias.reshape(NA_GROUP * GRID_W, n_keys)
            m = jnp.max(s, axis=-1, keepdims=True)
            p = jnp.exp2(s - m)
            l = jnp.sum(p, axis=-1, keepdims=True)
            o4 = jnp.dot(p.astype(BF16), vg, preferred_element_type=F32) / l
            o = o4[0:GRID_W]
            for h in range(1, NA_GROUP):
                o = jnp.where(lane_head == h, o4[h * GRID_W:(h + 1) * GRID_W], o)
            o_ref[qs, cs] = o.astype(o_ref.dtype)


def _na_attention(proj3, bias_tab, weights_f32):
    b, l, _ = proj3.shape
    rows = l // GRID_W
    rb = NA_ROWS_PER_STEP
    span = NA_WIN_ROWS + rb - 1
    steps_per_batch = rows // rb
    n_steps = b * steps_per_batch

    def kv_spec(col0):
        return pl.BlockSpec((None, pl.Element(span * GRID_W), pl.Element(D_NA)),
                            lambda bi, t: (bi, _na_window_start(t * rb, rows) * GRID_W, col0))

    cast_specs = []
    for w in weights_f32:
        blk, walk = _cast_block(w.shape, n_steps)
        cast_specs.append(pl.BlockSpec(
            blk, lambda bi, t, walk=walk: walk(bi * steps_per_batch + t)))

    outs = pl.pallas_call(
        functools.partial(_na_kernel, rows=rows, n_cast=len(weights_f32)),
        grid=(b, steps_per_batch),
        in_specs=[
            pl.BlockSpec((None, rb * GRID_W, D_NA), lambda bi, t: (bi, t, 0)),
            kv_spec(D_NA),
            kv_spec(2 * D_NA),
            pl.BlockSpec(bias_tab.shape, lambda bi, t: (0, 0, 0, 0),
                         pipeline_mode=pl.Buffered(1)),
        ] + cast_specs,
        out_specs=[pl.BlockSpec((None, rb * GRID_W, D_NA), lambda bi, t: (bi, t, 0))]
        + cast_specs,
        out_shape=[jax.ShapeDtypeStruct((b, l, D_NA), BF16)]
        + [jax.ShapeDtypeStruct(w.shape, BF16) for w in weights_f32],
        compiler_params=_cparams(("arbitrary", "arbitrary")),
        name="na_attn",
    )(proj3, proj3, proj3, bias_tab, *weights_f32)
    return outs[0], outs[1:]


def _conv3(u, w, b):
    n, c = u.shape
    t8 = 8
    row = lax.broadcasted_iota(jnp.int32, (t8, c), 0)
    prev = pltpu.roll(u, 1, axis=0)
    prev = jnp.concatenate([jnp.where(row == 0, 0.0, prev[0:t8]), prev[t8:]], axis=0)
    nxt = pltpu.roll(u, n - 1, axis=0)
    nxt = jnp.concatenate([nxt[0:n - t8], jnp.where(row == t8 - 1, 0.0, nxt[n - t8:])], axis=0)
    return prev * w[0:1] + u * w[1:2] + nxt * w[2:3] + b


def _hy_pre_kernel(u0_ref, u1_ref, u2_ref, w0_ref, w1_ref, w2_ref,
                   b0_ref, b1_ref, b2_ref, x0_ref, vx_ref):
    x0 = _conv3(u0_ref[...].astype(F32), w0_ref[...], b0_ref[...])
    x1 = _conv3(u1_ref[...].astype(F32), w1_ref[...], b1_ref[...])
    v = _conv3(u2_ref[...].astype(F32), w2_ref[...], b2_ref[...])
    x0_ref[...] = x0.astype(x0_ref.dtype)
    vx_ref[...] = (v * x1).astype(vx_ref.dtype)


def _hy_pre(proj3, short_w, short_b, d_hy, cb=128):
    b, l, _ = proj3.shape
    base = 3 * D_NA // cb
    per = d_hy // cb

    def u_spec(g):
        return pl.BlockSpec((None, l, cb), lambda bi, j: (bi, 0, base + g * per + j))

    def w_spec(g):
        return pl.BlockSpec((3, cb), lambda bi, j: (0, g * per + j))

    def b_spec(g):
        return pl.BlockSpec((1, cb), lambda bi, j: (0, g * per + j))

    out_spec = pl.BlockSpec((None, l, cb), lambda bi, j: (bi, 0, j))
    return pl.pallas_call(
        _hy_pre_kernel,
        grid=(b, per),
        in_specs=[u_spec(0), u_spec(1), u_spec(2), w_spec(0), w_spec(1), w_spec(2),
                  b_spec(0), b_spec(1), b_spec(2)],
        out_specs=[out_spec, out_spec],
        out_shape=[jax.ShapeDtypeStruct((b, l, d_hy), BF16),
                   jax.ShapeDtypeStruct((b, l, d_hy), BF16)],
        compiler_params=_cparams(("parallel", "parallel")),
        name="hy_pre",
    )(proj3, proj3, proj3, short_w, short_w, short_w, short_b, short_b, short_b)


def _filter_kernel(z_ref, w1_ref, b1_ref, w2_ref, b2_ref, w3_ref, b3_ref,
                   fq_ref, w4f_ref, w4b_ref, dl_ref, db_ref, o_ref, h_ref):
    hi = lax.Precision.HIGHEST
    seq = z_ref.shape[0]

    @pl.when(pl.program_id(0) == 0)
    def _():
        fq = fq_ref[...]

        def mlp(z):
            h = jnp.sin(fq * (jnp.dot(z, w1_ref[...], precision=hi,
                                      preferred_element_type=F32) + b1_ref[...]))
            h = jnp.sin(fq * (jnp.dot(h, w2_ref[...], precision=hi,
                                      preferred_element_type=F32) + b2_ref[...]))
            h = jnp.sin(fq * (jnp.dot(h, w3_ref[...], precision=hi,
                                      preferred_element_type=F32) + b3_ref[...]))
            return h

        def chunk(rs):
            h_ref[rs, :] = mlp(z_ref[rs, :]).astype(h_ref.dtype)

        _for_row_chunks(seq, 4 * ROW_CHUNK, chunk)

    decay = jnp.exp(-z_ref[:, 0:1] * dl_ref[...])
    h3 = h_ref[...]
    fwd = jnp.dot(h3, w4f_ref[...].astype(BF16), preferred_element_type=F32) * decay
    bwd = jnp.dot(h3, w4b_ref[...].astype(BF16), preferred_element_type=F32) * decay
    row = lax.broadcasted_iota(jnp.int32, bwd.shape, 0)
    bwd = jnp.where(row == 0, 0.0, bwd)
    fwd = jnp.where(row == 0, fwd + db_ref[...], fwd)
    o_ref[0:seq, :] = fwd.astype(o_ref.dtype)
    o_ref[seq:2 * seq, :] = bwd.astype(o_ref.dtype)


def _filter_taps(z, w1p, b1, w2, b2, w3, b3, freq, w4, deltas_abs, d_bias, d_hy, cb=256):
    seq, kpad = z.shape
    order = w2.shape[0]
    per = d_hy // cb
    full = lambda shape: pl.BlockSpec(shape, lambda j: tuple(0 for _ in shape))
    return pl.pallas_call(
        _filter_kernel,
        grid=(per,),
        in_specs=[full((seq, kpad)),
                  full((kpad, order)), full((1, order)),
                  full((order, order)), full((1, order)),
                  full((order, order)), full((1, order)),
                  full((1, order)),
                  pl.BlockSpec((order, cb), lambda j: (0, j)),
                  pl.BlockSpec((order, cb), lambda j: (0, per + j)),
                  pl.BlockSpec((1, cb), lambda j: (0, j)),
                  pl.BlockSpec((1, cb), lambda j: (0, j))],
        out_specs=pl.BlockSpec((2 * seq, cb), lambda j: (0, j)),
        out_shape=jax.ShapeDtypeStruct((2 * seq, d_hy), BF16),
        scratch_shapes=[pltpu.VMEM((seq, order), BF16)],
        compiler_params=_cparams(("arbitrary",)),
        name="hy_filter",
    )(z, w1p, b1, w2, b2, w3, b3, freq, w4, w4, deltas_abs, d_bias)


V7X_SUBLANES = 8


def _fft_slow_kernel(wk_ref, x_ref, o_ref):
    k, rb, c = x_ref.shape
    m = o_ref.shape[0]
    x = x_ref[...].astype(F32)
    outs = []
    for t in range(rb // V7X_SUBLANES):
        xs = x[:, t * V7X_SUBLANES:(t + 1) * V7X_SUBLANES, :]
        xs = xs.reshape(k * V7X_SUBLANES, c).astype(BF16)
        o = jnp.dot(wk_ref[...], xs, preferred_element_type=F32)
        outs.append(o.reshape(m, V7X_SUBLANES, c))
    o_ref[...] = jnp.concatenate(outs, axis=1).astype(o_ref.dtype)


def _fft_slow_gate_kernel(wk_ref, x_ref, x0_ref, beta_ref, o_ref):
    k, rb, c = x_ref.shape
    m = o_ref.shape[0]
    x = x_ref[...].astype(F32)
    outs = []
    for t in range(rb // V7X_SUBLANES):
        xs = x[:, t * V7X_SUBLANES:(t + 1) * V7X_SUBLANES, :]
        xs = xs.reshape(k * V7X_SUBLANES, c).astype(BF16)
        o = jnp.dot(wk_ref[...], xs, preferred_element_type=F32)
        outs.append(o.reshape(m, V7X_SUBLANES, c))
    hy = jnp.concatenate(outs, axis=1) * x0_ref[...].astype(F32)
    ms = jnp.mean(hy * hy, axis=-1, keepdims=True)
    o_ref[...] = (hy * lax.rsqrt(ms + EPS) * beta_ref[...]).astype(o_ref.dtype)


def _fft_slow_gate(wk, x4, x04, beta, rb=16):
    g, k, r, c = x4.shape
    m = wk.shape[0] // V7X_SUBLANES
    assert wk.shape[1] == k * V7X_SUBLANES and x04.shape == (g, m, r, c)
    return pl.pallas_call(
        _fft_slow_gate_kernel,
        grid=(g, r // rb),
        in_specs=[pl.BlockSpec(wk.shape, lambda gi, j: (0, 0)),
                  pl.BlockSpec((None, k, rb, c), lambda gi, j: (gi, 0, j, 0)),
                  pl.BlockSpec((None, m, rb, c), lambda gi, j: (gi, 0, j, 0)),
                  pl.BlockSpec((1, c), lambda gi, j: (0, 0))],
        out_specs=pl.BlockSpec((None, m, rb, c), lambda gi, j: (gi, 0, j, 0)),
        out_shape=jax.ShapeDtypeStruct((g, m, r, c), BF16),
        compiler_params=_cparams(("parallel", "parallel")),
        name="fft_slow_gate",
    )(wk, x4, x04, beta)


def _fft_slow(wk, x4, out_dtype, rb=16):
    g, k, r, c = x4.shape
    m = wk.shape[0] // V7X_SUBLANES
    assert wk.shape[1] == k * V7X_SUBLANES
    return pl.pallas_call(
        _fft_slow_kernel,
        grid=(g, r // rb),
        in_specs=[pl.BlockSpec(wk.shape, lambda gi, j: (0, 0)),
                  pl.BlockSpec((None, k, rb, c), lambda gi, j: (gi, 0, j, 0))],
        out_specs=pl.BlockSpec((None, m, rb, c), lambda gi, j: (gi, 0, j, 0)),
        out_shape=jax.ShapeDtypeStruct((g, m, r, c), out_dtype),
        compiler_params=_cparams(("parallel", "parallel")),
        name="fft_slow",
    )(wk, x4)


SPEC_SLABS = 2


def _fft_spec_kernel(e_ref, ei_ref, af_ref, a_ref, o_ref):
    pairs, two, slabs, r, c = a_ref.shape
    for q in range(slabs):
        e = e_ref[q]
        hf = jnp.dot(e, af_ref[0, :, q].reshape(two * r, c), preferred_element_type=F32)
        hb = jnp.dot(e, af_ref[1, :, q].reshape(two * r, c), preferred_element_type=F32)
        kr = hf[0:r] + hb[0:r]
        ki = hf[r:2 * r] - hb[r:2 * r]
        for p in range(pairs):
            b2 = jnp.dot(e, a_ref[p, :, q].reshape(two * r, c), preferred_element_type=F32)
            br, bi = b2[0:r], b2[r:2 * r]
            y2 = jnp.concatenate([br * kr - bi * ki, br * ki + bi * kr], axis=0).astype(BF16)
            g2 = jnp.dot(ei_ref[q], y2, preferred_element_type=F32)
            o_ref[p, :, q] = g2.reshape(two, r, c).astype(o_ref.dtype)


def _fft_spec(e_fwd, e_inv, a_filt, a_data):
    pairs, two, s, r, c = a_data.shape
    q = SPEC_SLABS
    return pl.pallas_call(
        _fft_spec_kernel,
        grid=(s // q,),
        in_specs=[pl.BlockSpec((q, two * r, two * r), lambda i: (i, 0, 0)),
                  pl.BlockSpec((q, two * r, two * r), lambda i: (i, 0, 0)),
                  pl.BlockSpec((2, two, q, r, c), lambda i: (0, 0, i, 0, 0)),
                  pl.BlockSpec((pairs, two, q, r, c), lambda i: (0, 0, i, 0, 0))],
        out_specs=pl.BlockSpec((pairs, two, q, r, c), lambda i: (0, 0, i, 0, 0)),
        out_shape=jax.ShapeDtypeStruct((pairs, two, s, r, c), BF16),
        compiler_params=_cparams(("parallel",)),
        name="fft_spec",
    )(e_fwd, e_inv, a_filt, a_data)


def _fft_constants(seq_len):
    n = 2 * seq_len
    r_, s_ = FFT_R, FFT_S
    assert r_ * s_ == n
    half = s_ // 2
    ks = np.arange(s_)[:, None]
    s = np.arange(s_)[None, :]
    ang = 2.0 * np.pi * ((ks * s) % s_) / s_
    fr, fi = np.cos(ang), -np.sin(ang)
    w1_data = np.block([[fr[:, :half], -fi[:, :half]], [fi[:, :half], fr[:, :half]]])
    w1_filt = np.block([[fr[:, :half]], [fi[:, :half]]])
    so = np.arange(half)[:, None]
    ko = np.arange(s_)[None, :]
    ang_i = 2.0 * np.pi * ((so * ko) % s_) / s_
    cr, ci = np.cos(ang_i), np.sin(ang_i)
    w3 = np.block([[cr, -ci], [ci, cr]]) / n
    ksv = np.arange(s_)[:, None, None]
    kr = np.arange(r_)[None, :, None]
    r = np.arange(r_)[None, None, :]
    ang_e = 2.0 * np.pi * ((r * (ksv + s_ * kr)) % n) / n
    er, ei = np.cos(ang_e), -np.sin(ang_e)
    e_fwd = np.concatenate([np.concatenate([er, -ei], axis=2),
                            np.concatenate([ei, er], axis=2)], axis=1)
    e_inv = np.transpose(e_fwd, (0, 2, 1))
    as_bf16 = lambda a: jnp.asarray(a.astype(np.float32)).astype(BF16)
    eye = np.eye(V7X_SUBLANES)
    slow = [as_bf16(np.kron(w, eye)) for w in (w1_data, w1_filt, w3)]
    return slow[0], slow[1], slow[2], as_bf16(e_fwd), as_bf16(e_inv)


def _mm_out_kernel(na_ref, hyn_ref, bna_ref, w_ref, x_ref, gt_ref, o_ref, ha_ref, hb_ref):
    bm, d_na = na_ref.shape

    def step(h_new, h_cur):
        for r0 in range(0, bm, ROW_CHUNK):
            rs = slice(r0, r0 + ROW_CHUNK)
            na = na_ref[rs, :].astype(F32)
            na_n = na * lax.rsqrt(jnp.mean(na * na, axis=-1, keepdims=True) + EPS) * bna_ref[...]
            h_new[rs, 0:d_na] = na_n.astype(BF16)
            h_new[rs, d_na:] = hyn_ref[rs, :]
        acc = jnp.dot(h_cur[...], w_ref[...], preferred_element_type=F32)
        o_ref[...] = x_ref[...] + gt_ref[0] * acc

    _skewed_steps(ha_ref, hb_ref, step)


def _mm_out(na2d, hyn2d, beta_na, w, x2d, gt, seq_len, bm=256):
    m, d_na = na2d.shape
    d_hy = hyn2d.shape[1]
    k, n = w.shape
    nb = m // bm
    blocks_per_seq = seq_len // bm
    new = lambda t: jnp.minimum(t, nb - 1)
    cur = lambda t: jnp.maximum(t - 1, 0)
    return pl.pallas_call(
        _mm_out_kernel,
        grid=(nb + 1,),
        in_specs=[pl.BlockSpec((bm, d_na), lambda t: (new(t), 0)),
                  pl.BlockSpec((bm, d_hy), lambda t: (new(t), 0)),
                  _resident((1, d_na)),
                  _resident((k, n)),
                  pl.BlockSpec((bm, n), lambda t: (cur(t), 0)),
                  pl.BlockSpec((1, 1, n), lambda t: (cur(t) // blocks_per_seq, 0, 0))],
        out_specs=pl.BlockSpec((bm, n), lambda t: (cur(t), 0)),
        out_shape=jax.ShapeDtypeStruct((m, n), F32),
        scratch_shapes=[pltpu.VMEM((bm, k), BF16), pltpu.VMEM((bm, k), BF16)],
        compiler_params=_cparams(("arbitrary",)),
        name="mm_out",
    )(na2d, hyn2d, beta_na, w, x2d, gt)


HALO = 16


def _mm_up_glu_kernel(x_ref, xp_ref, xn_ref, g_ref, sc_ref, sh_ref, wa_ref, wb_ref,
                      cw_ref, cb_ref, o_ref, ha_ref, hb_ref, *, blocks_per_seq, n_blocks):
    bm, k = x_ref.shape
    t = pl.program_id(0)
    j = pl.program_id(1)
    pos = jnp.minimum(t, n_blocks - 1) % blocks_per_seq

    def norm(x):
        gs = g_ref[...] * (1.0 + sc_ref[0])
        ms = jnp.mean(x * x, axis=-1, keepdims=True)
        return x * lax.rsqrt(ms + EPS) * gs + sh_ref[0]

    def build(h_new):
        zero = jnp.zeros((HALO, k), F32)
        h_new[0:HALO, :] = jnp.where(pos == 0, zero, norm(xp_ref[...])).astype(BF16)
        for r0 in range(0, bm, ROW_CHUNK):
            h_new[HALO + r0:HALO + r0 + ROW_CHUNK, :] = norm(
                x_ref[r0:r0 + ROW_CHUNK, :]).astype(BF16)
        h_new[HALO + bm:, :] = jnp.where(pos == blocks_per_seq - 1, zero,
                                         norm(xn_ref[...])).astype(BF16)

    def multiply(h_cur):
        a = jnp.dot(h_cur[...], wa_ref[...], preferred_element_type=F32)
        b = jnp.dot(h_cur[HALO:HALO + bm, :], wb_ref[...], preferred_element_type=F32)
        n_ext = a.shape[0]
        w = cw_ref[...]
        prev = pltpu.roll(a, 1, axis=0)[HALO:HALO + bm]
        nxt = pltpu.roll(a, n_ext - 1, axis=0)[HALO:HALO + bm]
        ac = prev * w[0:1] + a[HALO:HALO + bm] * w[1:2] + nxt * w[2:3] + cb_ref[...]
        gelu = 0.5 * ac * (1.0 + lax.erf(ac * (1.0 / math.sqrt(2.0))))
        o_ref[...] = (gelu * b).astype(o_ref.dtype)

    even = t % 2 == 0

    @pl.when((t == 0) & (j == 0))
    def _():
        build(ha_ref)

    @pl.when((t > 0) & even & (j == 0))
    def _():
        build(ha_ref)
        multiply(hb_ref)

    @pl.when((t > 0) & even & (j > 0))
    def _():
        multiply(hb_ref)

    @pl.when(jnp.logical_not(even) & (j == 0))
    def _():
        build(hb_ref)
        multiply(ha_ref)

    @pl.when(jnp.logical_not(even) & (j > 0))
    def _():
        multiply(ha_ref)


def _mm_up_glu(x2d, g, sc, sh, w_up, conv_w, conv_b, seq_len, bm=1024, bn=512):
    m, k = x2d.shape
    d_ff = w_up.shape[1] // 2
    nbn = d_ff // bn
    nb = m // bm
    blocks_per_seq = seq_len // bm
    hb = bm // HALO
    last = m // HALO - 1
    new = lambda t: jnp.minimum(t, nb - 1)
    cur = lambda t: jnp.maximum(t - 1, 0)
    return pl.pallas_call(
        functools.partial(_mm_up_glu_kernel, blocks_per_seq=blocks_per_seq, n_blocks=nb),
        grid=(nb + 1, nbn),
        in_specs=[pl.BlockSpec((bm, k), lambda t, j: (new(t), 0)),
                  pl.BlockSpec((HALO, k), lambda t, j: (jnp.maximum(new(t) * hb - 1, 0), 0)),
                  pl.BlockSpec((HALO, k),
                               lambda t, j: (jnp.minimum((new(t) + 1) * hb, last), 0)),
                  pl.BlockSpec((1, k), lambda t, j: (0, 0)),
                  pl.BlockSpec((1, 1, k), lambda t, j: (new(t) // blocks_per_seq, 0, 0)),
                  pl.BlockSpec((1, 1, k), lambda t, j: (new(t) // blocks_per_seq, 0, 0)),
                  pl.BlockSpec((k, bn), lambda t, j: (0, j)),
                  pl.BlockSpec((k, bn), lambda t, j: (0, nbn + j)),
                  pl.BlockSpec((3, bn), lambda t, j: (0, j)),
                  pl.BlockSpec((1, bn), lambda t, j: (0, j))],
        out_specs=pl.BlockSpec((bm, bn), lambda t, j: (cur(t), jnp.where(t == 0, 0, j))),
        out_shape=jax.ShapeDtypeStruct((m, d_ff), BF16),
        scratch_shapes=[pltpu.VMEM((bm + 2 * HALO, k), BF16),
                        pltpu.VMEM((bm + 2 * HALO, k), BF16)],
        compiler_params=_cparams(("arbitrary", "arbitrary")),
        name="mm_up_glu",
    )(x2d, x2d, x2d, g, sc, sh, w_up, w_up, conv_w, conv_b)


def _mm_down_kernel(g_ref, w_ref, x_ref, gt_ref, gf_ref, o_ref):
    acc = jnp.dot(g_ref[...], w_ref[...], preferred_element_type=F32)
    x = x_ref[...] + gt_ref[0] * acc
    ms = jnp.mean(x * x, axis=-1, keepdims=True)
    o_ref[...] = x * lax.rsqrt(ms + EPS) * gf_ref[...]


def _mm_down(g2d, w, x2d, gt, g_final, seq_len, bm=256):
    m, k = g2d.shape
    n = w.shape[1]
    blocks_per_seq = seq_len // bm
    return pl.pallas_call(
        _mm_down_kernel,
        grid=(m // bm,),
        in_specs=[pl.BlockSpec((bm, k), lambda i: (i, 0)),
                  _resident((k, n)),
                  pl.BlockSpec((bm, n), lambda i: (i, 0)),
                  pl.BlockSpec((1, 1, n), lambda i: (i // blocks_per_seq, 0, 0)),
                  _resident((1, n))],
        out_specs=pl.BlockSpec((bm, n), lambda i: (i, 0)),
        out_shape=jax.ShapeDtypeStruct((m, n), F32),
        compiler_params=_cparams(("parallel",)),
        name="mm_down",
    )(g2d, w, x2d, gt, g_final)


def _position_features(seq_len, kpad):
    t = np.linspace(0.0, 1.0, seq_len)[:, None]
    bands = (FILTER_EMB - 1) // 2
    w = 2.0 * np.pi * np.arange(seq_len)[:, None] / seq_len
    fr = np.linspace(1e-4, bands - 1, bands)[None, :]
    z = np.concatenate([t, np.cos(fr * w), -np.sin(fr * w)], axis=-1)
    z = np.pad(z, ((0, 0), (0, kpad - z.shape[1])))
    return z.astype(np.float32)


def kernel(x, c, w_ada, b_ada, g_mix, w_in, na_rpb, hy_short_w, hy_short_b,
           hy_filt_w1, hy_filt_b1, hy_filt_w2, hy_filt_b2, hy_filt_w3, hy_filt_b3,
           hy_filt_w4, hy_filt_freq, hy_bias, beta_na, beta_hy, w_out, g_ffn,
           w_up, ffn_conv_w, ffn_conv_b, w_down, g_final):
    b, l, d = x.shape
    depth = w_ada.shape[0]
    d_hy = d - D_NA
    d_ff = w_down.shape[1]
    rows = l // GRID_W
    m = b * l
    assert depth == 1 and b % 2 == 0 and 2 * l == FFT_R * FFT_S

    c_pad = jnp.pad(c, ((0, 8 - b), (0, 0)))
    mod = _adaln(c_pad, w_ada[0], b_ada[0][None, :])[:b]
    sh1, sc1, gt1, sh2, sc2, gt2 = [t[:, None, :] for t in jnp.split(mod, 6, axis=-1)]

    x2d = x.reshape(m, d)

    col_scale = np.ones((w_in.shape[2],), np.float32)
    col_scale[:D_NA] = NA_Q_SCALE
    w_in_bf = (w_in[0] * jnp.asarray(col_scale)[None, :]).astype(BF16)
    proj = _mm_norm(x2d, g_mix, sc1, sh1, w_in_bf, l)

    bias_tab = _bias_table(na_rpb[0].reshape(-1))
    na, (w_out_bf, w_up_bf, w_down_bf) = _na_attention(
        proj.reshape(b, l, proj.shape[1]), bias_tab, (w_out[0], w_up[0], w_down[0]))
    na2d = na.reshape(m, D_NA)

    x0, vx = _hy_pre(proj.reshape(b, l, proj.shape[1]), hy_short_w[0], hy_short_b[0][None, :], d_hy)

    kpad = V7X_LANES
    z_np = _position_features(l, kpad)
    w1p = jnp.pad(hy_filt_w1[0], ((0, kpad - FILTER_EMB), (0, 0)))
    max_decay = math.log(DECAY_TARGET) / FAST_DECAY_PCT
    min_decay = math.log(DECAY_TARGET) / SLOW_DECAY_PCT
    deltas_abs = np.abs(np.linspace(min_decay, max_decay, d_hy))[None, :].astype(np.float32)
    taps = _filter_taps(jnp.asarray(z_np), w1p, hy_filt_b1,
                        hy_filt_w2[0], hy_filt_b2, hy_filt_w3[0], hy_filt_b3,
                        hy_filt_freq, hy_filt_w4[0], jnp.asarray(deltas_abs), hy_bias, d_hy)

    w1_data, w1_filt, w3, e_fwd, e_inv = _fft_constants(l)
    a_filt = _fft_slow(w1_filt, taps.reshape(2, FFT_S // 2, FFT_R, d_hy), BF16)
    a_data = _fft_slow(w1_data, vx.reshape(b // 2, FFT_S, FFT_R, d_hy), BF16)
    g_spec = _fft_spec(e_fwd, e_inv, a_filt.reshape(2, 2, FFT_S, FFT_R, d_hy),
                       a_data.reshape(b // 2, 2, FFT_S, FFT_R, d_hy))
    hyn = _fft_slow_gate(w3, g_spec.reshape(b // 2, 2 * FFT_S, FFT_R, d_hy),
                         x0.reshape(b // 2, FFT_S, FFT_R, d_hy), beta_hy)

    x1 = _mm_out(na2d, hyn.reshape(m, d_hy), beta_na, w_out_bf, x2d, gt1, l)

    gl = _mm_up_glu(x1, g_ffn, sc2, sh2, w_up_bf, ffn_conv_w[0],
                    ffn_conv_b[0][None, :], l)
    out = _mm_down(gl, w_down_bf, x1, gt2, g_final[None, :], l)
    return out.reshape(b, l, d)
```

```python
import functools
import math

import numpy as np
import jax
import jax.numpy as jnp
from jax import lax
from jax.experimental import pallas as pl
from jax.experimental.pallas import tpu as pltpu

F32 = jnp.float32
BF16 = jnp.bfloat16

GRID_W = 64
NA_HEADS = 16
NA_HEAD_DIM = 64
D_NA = NA_HEADS * NA_HEAD_DIM
NA_WIN_ROWS = 8
NA_WIN_COLS = 16
NA_GROUP = 4
FILTER_EMB = 33
DECAY_TARGET = 1e-2
FAST_DECAY_PCT = 0.3
SLOW_DECAY_PCT = 1.5
EPS = 1e-6
NEG_BIG = -1e30
LOG2_E = math.log2(math.e)
NA_Q_SCALE = NA_HEAD_DIM ** -0.5 * LOG2_E

V7X_LANES = 128
V7X_VMEM_BYTES = 64 * 1024 * 1024
VMEM_LIMIT = 56 * 1024 * 1024

FFT_R = 128
FFT_S = 64

ROW_CHUNK = 64


def _cparams(sem, vmem=VMEM_LIMIT):
    return pltpu.CompilerParams(dimension_semantics=sem, vmem_limit_bytes=vmem)


def _for_row_chunks(n_rows, chunk, fn):
    def body(i, carry):
        fn(pl.ds(pl.multiple_of(i * chunk, chunk), chunk))
        return carry
    lax.fori_loop(0, n_rows // chunk, body, 0)


def _adaln_block(c_ref, w_ref, b_ref):
    c = c_ref[...]
    cond = c / (1.0 + jnp.exp(-c))
    return jnp.dot(cond.astype(BF16), w_ref[...].astype(BF16),
                   preferred_element_type=F32) + b_ref[...]


def _adaln_bias_kernel(c_ref, w_ref, b_ref, rpb_ref, o_ref, bias_ref):
    o_ref[...] = _adaln_block(c_ref, w_ref, b_ref)
    heads = bias_ref.shape[0]
    for hh in range(heads):
        _bias_head(rpb_ref, bias_ref, hh, pl.program_id(0) * heads + hh)


def _adaln_bias(c_pad, w_ada, b_ada, rpb_flat, n_cols, bn=1024):
    rows, d = c_pad.shape
    steps = n_cols // bn
    heads = NA_HEADS // steps
    assert heads * steps == NA_HEADS
    n_keys = NA_WIN_ROWS * GRID_W
    return pl.pallas_call(
        _adaln_bias_kernel,
        grid=(steps,),
        in_specs=[pl.BlockSpec((rows, d), lambda j: (0, 0)),
                  pl.BlockSpec((d, bn), lambda j: (0, j)),
                  pl.BlockSpec((1, bn), lambda j: (0, j)),
                  pl.BlockSpec(memory_space=pltpu.SMEM)],
        out_specs=[pl.BlockSpec((rows, bn), lambda j: (0, j)),
                   pl.BlockSpec((heads, NA_WIN_ROWS, GRID_W, n_keys), lambda j: (j, 0, 0, 0))],
        out_shape=[jax.ShapeDtypeStruct((rows, n_cols), F32),
                   jax.ShapeDtypeStruct((NA_HEADS, NA_WIN_ROWS, GRID_W, n_keys), F32)],
        compiler_params=_cparams(("parallel",)),
        name="adaln_bias",
    )(c_pad, w_ada, b_ada, rpb_flat)


def _skewed_steps(ha_ref, hb_ref, step):
    t = pl.program_id(0)

    @pl.when(t == 0)
    def _():
        hb_ref[...] = jnp.zeros_like(hb_ref)

    @pl.when(t % 2 == 0)
    def _():
        step(ha_ref, hb_ref)

    @pl.when(t % 2 == 1)
    def _():
        step(hb_ref, ha_ref)


def _resident(shape):
    return pl.BlockSpec(shape, lambda t: tuple(0 for _ in shape), pipeline_mode=pl.Buffered(1))


def _mm_norm_kernel(x_ref, g_ref, sc_ref, sh_ref, w_ref, c_ref, wt_ref, bt_ref,
                    o_ref, mt_ref, ha_ref, hb_ref):
    bm = x_ref.shape[0]

    def step(h_new, h_cur):
        gs = g_ref[...] * (1.0 + sc_ref[0])
        sh = sh_ref[0]
        for r0 in range(0, bm, ROW_CHUNK):
            x = x_ref[r0:r0 + ROW_CHUNK, :]
            ms = jnp.mean(x * x, axis=-1, keepdims=True)
            h_new[r0:r0 + ROW_CHUNK, :] = (x * lax.rsqrt(ms + EPS) * gs + sh).astype(BF16)
        o_ref[...] = jnp.dot(h_cur[...], w_ref[...],
                             preferred_element_type=F32).astype(o_ref.dtype)
        mt_ref[...] = _adaln_block(c_ref, wt_ref, bt_ref)

    _skewed_steps(ha_ref, hb_ref, step)


def _mm_norm(x2d, g, sc, sh, w, c_pad, w_ada, b_ada, tail_col0, seq_len, bm=256):
    m, k = x2d.shape
    n = w.shape[1]
    nb = m // bm
    blocks_per_seq = seq_len // bm
    rows = c_pad.shape[0]
    n_tail = w_ada.shape[1] - tail_col0
    bt = n_tail // nb
    assert bt * nb == n_tail and bt % V7X_LANES == 0 and tail_col0 % bt == 0
    new = lambda t: jnp.minimum(t, nb - 1)
    cur = lambda t: jnp.maximum(t - 1, 0)
    tail = lambda t: (0, tail_col0 // bt + new(t))
    return pl.pallas_call(
        _mm_norm_kernel,
        grid=(nb + 1,),
        in_specs=[pl.BlockSpec((bm, k), lambda t: (new(t), 0)),
                  _resident((1, k)),
                  pl.BlockSpec((1, 1, k), lambda t: (new(t) // blocks_per_seq, 0, 0)),
                  pl.BlockSpec((1, 1, k), lambda t: (new(t) // blocks_per_seq, 0, 0)),
                  _resident((k, n)),
                  _resident(c_pad.shape),
                  pl.BlockSpec((k, bt), tail),
                  pl.BlockSpec((1, bt), tail)],
        out_specs=[pl.BlockSpec((bm, n), lambda t: (cur(t), 0)),
                   pl.BlockSpec((rows, bt), lambda t: (0, new(t)))],
        out_shape=[jax.ShapeDtypeStruct((m, n), BF16),
                   jax.ShapeDtypeStruct((rows, n_tail), F32)],
        scratch_shapes=[pltpu.VMEM((bm, k), BF16), pltpu.VMEM((bm, k), BF16)],
        compiler_params=_cparams(("arbitrary",)),
        name="mm_norm",
    )(x2d, g, sc, sh, w, c_pad, w_ada, b_ada)


def _bias_head(rpb_ref, o_ref, slot, h):
    n_rows = 2 * NA_WIN_ROWS - 1
    n_cols = 2 * NA_WIN_COLS - 1
    shape = (GRID_W, 2 * GRID_W)
    lane = lax.broadcasted_iota(jnp.int32, shape, 1)
    cq = lax.broadcasted_iota(jnp.int32, shape, 0)
    ck = lane & (GRID_W - 1)
    first = lane < GRID_W
    cs = jnp.clip(cq - NA_WIN_COLS // 2, 0, GRID_W - NA_WIN_COLS)
    valid = (ck >= cs) & (ck < cs + NA_WIN_COLS)
    d = jnp.clip(ck - cq, -(NA_WIN_COLS - 1), NA_WIN_COLS - 1) + (NA_WIN_COLS - 1)
    pair = []
    for j in range(n_rows - 1):
        base0 = (h * n_rows + j) * n_cols
        base1 = base0 + n_cols
        acc = jnp.zeros(shape, F32)
        for dd in range(n_cols):
            val = jnp.where(first, rpb_ref[base0 + dd], rpb_ref[base1 + dd])
            acc = jnp.where(d == dd, val, acc)
        pair.append(jnp.where(valid, acc * LOG2_E, NEG_BIG))
    for w in range(NA_WIN_ROWS):
        for ip in range(NA_WIN_ROWS // 2):
            o_ref[slot, w, :, ip * 2 * GRID_W:(ip + 1) * 2 * GRID_W] = pair[w + 2 * ip]


NA_ROWS_PER_STEP = 4


def _na_row_start(r, rows):
    return jnp.clip(r - NA_WIN_ROWS // 2, 0, rows - NA_WIN_ROWS)


def _na_window_start(r0, rows):
    span = NA_WIN_ROWS + NA_ROWS_PER_STEP - 1
    return jnp.clip(r0 - NA_WIN_ROWS // 2, 0, rows - span)


def _cast_block(shape, n_steps):
    r, c = shape
    for f in (1, 2, 4, 8):
        row_blocks = n_steps // f
        if (n_steps % f == 0 and r % (2 * V7X_SUBLANES * row_blocks) == 0
                and c % (V7X_LANES * f) == 0):
            return (r // row_blocks, c // f), (lambda s, f=f: (s // f, s % f))
    raise ValueError(f"cannot walk {shape} in {n_steps} blocks")


def _na_kernel(q_ref, k_ref, v_ref, bias_ref, *rest, rows, n_cast):
    casts_in, o_ref, casts_out = rest[:n_cast], rest[n_cast], rest[n_cast + 1:]
    for src, dst in zip(casts_in, casts_out):
        dst[...] = src[...].astype(dst.dtype)
    _na_body(q_ref, k_ref, v_ref, bias_ref, o_ref, rows=rows)


def _na_body(q_ref, k_ref, v_ref, bias_ref, o_ref, *, rows):
    n_keys = NA_WIN_ROWS * GRID_W
    gw = NA_GROUP * NA_HEAD_DIM
    lane_head = lax.broadcasted_iota(jnp.int32, (GRID_W, gw), 1) // NA_HEAD_DIM
    r0 = pl.program_id(1) * NA_ROWS_PER_STEP
    win0 = _na_window_start(r0, rows)
    for j in range(NA_ROWS_PER_STEP):
        r = r0 + j
        rs = _na_row_start(r, rows)
        ks = pl.ds(pl.multiple_of((rs - win0) * GRID_W, GRID_W), n_keys)
        w = rs - r + (NA_WIN_ROWS - 1)
        qs = slice(j * GRID_W, (j + 1) * GRID_W)
        for g in range(NA_HEADS // NA_GROUP):
            cs = slice(g * gw, (g + 1) * gw)
            qg = q_ref[qs, cs]
            kg = k_ref[ks, cs]
            vg = v_ref[ks, cs]
            zero = jnp.zeros_like(qg)
            q4 = jnp.concatenate(
                [jnp.where(lane_head == h, qg, zero) for h in range(NA_GROUP)], axis=0)
            s = lax.dot_general(q4, kg, (((1,), (1,)), ((), ())), preferred_element_type=F32)
            bias = bias_ref[NA_GROUP * g:NA_GROUP * (g + 1), w]
            s = s + bias.reshape(NA_GROUP * GRID_W, n_keys)
            m = jnp.max(s, axis=-1, keepdims=True)
            p = jnp.exp2(s - m)
            l = jnp.sum(p, axis=-1, keepdims=True)
            o4 = jnp.dot(p.astype(BF16), vg, preferred_element_type=F32) / l
            o = o4[0:GRID_W]
            for h in range(1, NA_GROUP):
                o = jnp.where(lane_head == h, o4[h * GRID_W:(h + 1) * GRID_W], o)
            o_ref[qs, cs] = o.astype(o_ref.dtype)


def _na_attention(proj3, bias_tab, weights_f32):
    b, l, _ = proj3.shape
    rows = l // GRID_W
    rb = NA_ROWS_PER_STEP
    span = NA_WIN_ROWS + rb - 1
    steps_per_batch = rows // rb
    n_steps = b * steps_per_batch

    def kv_spec(col0):
        return pl.BlockSpec((None, pl.Element(span * GRID_W), pl.Element(D_NA)),
                            lambda bi, t: (bi, _na_window_start(t * rb, rows) * GRID_W, col0))

    cast_specs = []
    for w in weights_f32:
        blk, walk = _cast_block(w.shape, n_steps)
        cast_specs.append(pl.BlockSpec(
            blk, lambda bi, t, walk=walk: walk(bi * steps_per_batch + t)))

    outs = pl.pallas_call(
        functools.partial(_na_kernel, rows=rows, n_cast=len(weights_f32)),
        grid=(b, steps_per_batch),
        in_specs=[
            pl.BlockSpec((None, rb * GRID_W, D_NA), lambda bi, t: (bi, t, 0)),
            kv_spec(D_NA),
            kv_spec(2 * D_NA),
            pl.BlockSpec(bias_tab.shape, lambda bi, t: (0, 0, 0, 0),
                         pipeline_mode=pl.Buffered(1)),
        ] + cast_specs,
        out_specs=[pl.BlockSpec((None, rb * GRID_W, D_NA), lambda bi, t: (bi, t, 0))]
        + cast_specs,
        out_shape=[jax.ShapeDtypeStruct((b, l, D_NA), BF16)]
        + [jax.ShapeDtypeStruct(w.shape, BF16) for w in weights_f32],
        compiler_params=_cparams(("arbitrary", "arbitrary")),
        name="na_attn",
    )(proj3, proj3, proj3, bias_tab, *weights_f32)
    return outs[0], outs[1:]


def _conv3(u, w, b):
    n, c = u.shape
    t8 = 8
    row = lax.broadcasted_iota(jnp.int32, (t8, c), 0)
    prev = pltpu.roll(u, 1, axis=0)
    prev = jnp.concatenate([jnp.where(row == 0, 0.0, prev[0:t8]), prev[t8:]], axis=0)
    nxt = pltpu.roll(u, n - 1, axis=0)
    nxt = jnp.concatenate([nxt[0:n - t8], jnp.where(row == t8 - 1, 0.0, nxt[n - t8:])], axis=0)
    return prev * w[0:1] + u * w[1:2] + nxt * w[2:3] + b


def _hy_pre_kernel(u0_ref, u1_ref, u2_ref, w0_ref, w1_ref, w2_ref,
                   b0_ref, b1_ref, b2_ref, x0_ref, vx_ref):
    x0 = _conv3(u0_ref[...].astype(F32), w0_ref[...], b0_ref[...])
    x1 = _conv3(u1_ref[...].astype(F32), w1_ref[...], b1_ref[...])
    v = _conv3(u2_ref[...].astype(F32), w2_ref[...], b2_ref[...])
    x0_ref[...] = x0.astype(x0_ref.dtype)
    vx_ref[...] = (v * x1).astype(vx_ref.dtype)


def _hy_pre(proj3, short_w, short_b, d_hy, cb=128):
    b, l, _ = proj3.shape
    base = 3 * D_NA // cb
    per = d_hy // cb

    def u_spec(g):
        return pl.BlockSpec((None, l, cb), lambda bi, j: (bi, 0, base + g * per + j))

    def w_spec(g):
        return pl.BlockSpec((3, cb), lambda bi, j: (0, g * per + j))

    def b_spec(g):
        return pl.BlockSpec((1, cb), lambda bi, j: (0, g * per + j))

    out_spec = pl.BlockSpec((None, l, cb), lambda bi, j: (bi, 0, j))
    return pl.pallas_call(
        _hy_pre_kernel,
        grid=(b, per),
        in_specs=[u_spec(0), u_spec(1), u_spec(2), w_spec(0), w_spec(1), w_spec(2),
                  b_spec(0), b_spec(1), b_spec(2)],
        out_specs=[out_spec, out_spec],
        out_shape=[jax.ShapeDtypeStruct((b, l, d_hy), BF16),
                   jax.ShapeDtypeStruct((b, l, d_hy), BF16)],
        compiler_params=_cparams(("parallel", "parallel")),
        name="hy_pre",
    )(proj3, proj3, proj3, short_w, short_w, short_w, short_b, short_b, short_b)


def _filter_kernel(z_ref, w1_ref, b1_ref, w2_ref, b2_ref, w3_ref, b3_ref,
                   fq_ref, w4f_ref, w4b_ref, dl_ref, db_ref, o_ref, h_ref):
    hi = lax.Precision.HIGHEST
    seq = z_ref.shape[0]

    @pl.when(pl.program_id(0) == 0)
    def _():
        fq = fq_ref[...]

        def mlp(z):
            h = jnp.sin(fq * (jnp.dot(z, w1_ref[...], precision=hi,
                                      preferred_element_type=F32) + b1_ref[...]))
            h = jnp.sin(fq * (jnp.dot(h, w2_ref[...], precision=hi,
                                      preferred_element_type=F32) + b2_ref[...]))
            h = jnp.sin(fq * (jnp.dot(h, w3_ref[...], precision=hi,
                                      preferred_element_type=F32) + b3_ref[...]))
            return h

        def chunk(rs):
            h_ref[rs, :] = mlp(z_ref[rs, :]).astype(h_ref.dtype)

        _for_row_chunks(seq, 4 * ROW_CHUNK, chunk)

    decay = jnp.exp(-z_ref[:, 0:1] * dl_ref[...])
    h3 = h_ref[...]
    fwd = jnp.dot(h3, w4f_ref[...].astype(BF16), preferred_element_type=F32) * decay
    bwd = jnp.dot(h3, w4b_ref[...].astype(BF16), preferred_element_type=F32) * decay
    row = lax.broadcasted_iota(jnp.int32, bwd.shape, 0)
    bwd = jnp.where(row == 0, 0.0, bwd)
    fwd = jnp.where(row == 0, fwd + db_ref[...], fwd)
    o_ref[0:seq, :] = fwd.astype(o_ref.dtype)
    o_ref[seq:2 * seq, :] = bwd.astype(o_ref.dtype)


def _filter_taps(z, w1p, b1, w2, b2, w3, b3, freq, w4, deltas_abs, d_bias, d_hy, cb=256):
    seq, kpad = z.shape
    order = w2.shape[0]
    per = d_hy // cb
    full = lambda shape: pl.BlockSpec(shape, lambda j: tuple(0 for _ in shape))
    return pl.pallas_call(
        _filter_kernel,
        grid=(per,),
        in_specs=[full((seq, kpad)),
                  full((kpad, order)), full((1, order)),
                  full((order, order)), full((1, order)),
                  full((order, order)), full((1, order)),
                  full((1, order)),
                  pl.BlockSpec((order, cb), lambda j: (0, j)),
                  pl.BlockSpec((order, cb), lambda j: (0, per + j)),
                  pl.BlockSpec((1, cb), lambda j: (0, j)),
                  pl.BlockSpec((1, cb), lambda j: (0, j))],
        out_specs=pl.BlockSpec((2 * seq, cb), lambda j: (0, j)),
        out_shape=jax.ShapeDtypeStruct((2 * seq, d_hy), BF16),
        scratch_shapes=[pltpu.VMEM((seq, order), BF16)],
        compiler_params=_cparams(("arbitrary",)),
        name="hy_filter",
    )(z, w1p, b1, w2, b2, w3, b3, freq, w4, w4, deltas_abs, d_bias)


V7X_SUBLANES = 8


def _fft_slow_kernel(wk_ref, x_ref, o_ref):
    k, rb, c = x_ref.shape
    m = o_ref.shape[0]
    x = x_ref[...].astype(F32)
    outs = []
    for t in range(rb // V7X_SUBLANES):
        xs = x[:, t * V7X_SUBLANES:(t + 1) * V7X_SUBLANES, :]
        xs = xs.reshape(k * V7X_SUBLANES, c).astype(BF16)
        o = jnp.dot(wk_ref[...], xs, preferred_element_type=F32)
        outs.append(o.reshape(m, V7X_SUBLANES, c))
    o_ref[...] = jnp.concatenate(outs, axis=1).astype(o_ref.dtype)


def _fft_slow_gate_kernel(wk_ref, x_ref, x0_ref, beta_ref, o_ref):
    k, rb, c = x_ref.shape
    m = o_ref.shape[0]
    x = x_ref[...].astype(F32)
    outs = []
    for t in range(rb // V7X_SUBLANES):
        xs = x[:, t * V7X_SUBLANES:(t + 1) * V7X_SUBLANES, :]
        xs = xs.reshape(k * V7X_SUBLANES, c).astype(BF16)
        o = jnp.dot(wk_ref[...], xs, preferred_element_type=F32)
        outs.append(o.reshape(m, V7X_SUBLANES, c))
    hy = jnp.concatenate(outs, axis=1) * x0_ref[...].astype(F32)
    ms = jnp.mean(hy * hy, axis=-1, keepdims=True)
    o_ref[...] = (hy * lax.rsqrt(ms + EPS) * beta_ref[...]).astype(o_ref.dtype)


def _fft_slow_gate(wk, x4, x04, beta, rb=16):
    g, k, r, c = x4.shape
    m = wk.shape[0] // V7X_SUBLANES
    assert wk.shape[1] == k * V7X_SUBLANES and x04.shape == (g, m, r, c)
    return pl.pallas_call(
        _fft_slow_gate_kernel,
        grid=(g, r // rb),
        in_specs=[pl.BlockSpec(wk.shape, lambda gi, j: (0, 0)),
                  pl.BlockSpec((None, k, rb, c), lambda gi, j: (gi, 0, j, 0)),
                  pl.BlockSpec((None, m, rb, c), lambda gi, j: (gi, 0, j, 0)),
                  pl.BlockSpec((1, c), lambda gi, j: (0, 0))],
        out_specs=pl.BlockSpec((None, m, rb, c), lambda gi, j: (gi, 0, j, 0)),
        out_shape=jax.ShapeDtypeStruct((g, m, r, c), BF16),
        compiler_params=_cparams(("parallel", "parallel")),
        name="fft_slow_gate",
    )(wk, x4, x04, beta)


def _fft_slow(wk, x4, out_dtype, rb=16):
    g, k, r, c = x4.shape
    m = wk.shape[0] // V7X_SUBLANES
    assert wk.shape[1] == k * V7X_SUBLANES
    return pl.pallas_call(
        _fft_slow_kernel,
        grid=(g, r // rb),
        in_specs=[pl.BlockSpec(wk.shape, lambda gi, j: (0, 0)),
                  pl.BlockSpec((None, k, rb, c), lambda gi, j: (gi, 0, j, 0))],
        out_specs=pl.BlockSpec((None, m, rb, c), lambda gi, j: (gi, 0, j, 0)),
        out_shape=jax.ShapeDtypeStruct((g, m, r, c), out_dtype),
        compiler_params=_cparams(("parallel", "parallel")),
        name="fft_slow",
    )(wk, x4)


SPEC_SLABS = 2


def _fft_spec_kernel(e_ref, ei_ref, af_ref, a_ref, o_ref):
    pairs, two, slabs, r, c = a_ref.shape
    for q in range(slabs):
        e = e_ref[q]
        hf = jnp.dot(e, af_ref[0, :, q].reshape(two * r, c), preferred_element_type=F32)
        hb = jnp.dot(e, af_ref[1, :, q].reshape(two * r, c), preferred_element_type=F32)
        kr = hf[0:r] + hb[0:r]
        ki = hf[r:2 * r] - hb[r:2 * r]
        for p in range(pairs):
            b2 = jnp.dot(e, a_ref[p, :, q].reshape(two * r, c), preferred_element_type=F32)
            br, bi = b2[0:r], b2[r:2 * r]
            y2 = jnp.concatenate([br * kr - bi * ki, br * ki + bi * kr], axis=0).astype(BF16)
            g2 = jnp.dot(ei_ref[q], y2, preferred_element_type=F32)
            o_ref[p, :, q] = g2.reshape(two, r, c).astype(o_ref.dtype)


def _fft_spec(e_fwd, e_inv, a_filt, a_data):
    pairs, two, s, r, c = a_data.shape
    q = SPEC_SLABS
    return pl.pallas_call(
        _fft_spec_kernel,
        grid=(s // q,),
        in_specs=[pl.BlockSpec((q, two * r, two * r), lambda i: (i, 0, 0)),
                  pl.BlockSpec((q, two * r, two * r), lambda i: (i, 0, 0)),
                  pl.BlockSpec((2, two, q, r, c), lambda i: (0, 0, i, 0, 0)),
                  pl.BlockSpec((pairs, two, q, r, c), lambda i: (0, 0, i, 0, 0))],
        out_specs=pl.BlockSpec((pairs, two, q, r, c), lambda i: (0, 0, i, 0, 0)),
        out_shape=jax.ShapeDtypeStruct((pairs, two, s, r, c), BF16),
        compiler_params=_cparams(("parallel",)),
        name="fft_spec",
    )(e_fwd, e_inv, a_filt, a_data)


def _fft_constants(seq_len):
    n = 2 * seq_len
    r_, s_ = FFT_R, FFT_S
    assert r_ * s_ == n
    half = s_ // 2
    ks = np.arange(s_)[:, None]
    s = np.arange(s_)[None, :]
    ang = 2.0 * np.pi * ((ks * s) % s_) / s_
    fr, fi = np.cos(ang), -np.sin(ang)
    w1_data = np.block([[fr[:, :half], -fi[:, :half]], [fi[:, :half], fr[:, :half]]])
    w1_filt = np.block([[fr[:, :half]], [fi[:, :half]]])
    so = np.arange(half)[:, None]
    ko = np.arange(s_)[None, :]
    ang_i = 2.0 * np.pi * ((so * ko) % s_) / s_
    cr, ci = np.cos(ang_i), np.sin(ang_i)
    w3 = np.block([[cr, -ci], [ci, cr]]) / n
    ksv = np.arange(s_)[:, None, None]
    kr = np.arange(r_)[None, :, None]
    r = np.arange(r_)[None, None, :]
    ang_e = 2.0 * np.pi * ((r * (ksv + s_ * kr)) % n) / n
    er, ei = np.cos(ang_e), -np.sin(ang_e)
    e_fwd = np.concatenate([np.concatenate([er, -ei], axis=2),
                            np.concatenate([ei, er], axis=2)], axis=1)
    e_inv = np.transpose(e_fwd, (0, 2, 1))
    as_bf16 = lambda a: jnp.asarray(a.astype(np.float32)).astype(BF16)
    eye = np.eye(V7X_SUBLANES)
    slow = [as_bf16(np.kron(w, eye)) for w in (w1_data, w1_filt, w3)]
    return slow[0], slow[1], slow[2], as_bf16(e_fwd), as_bf16(e_inv)


def _mm_out_kernel(na_ref, hyn_ref, bna_ref, w_ref, x_ref, gt_ref, o_ref, ha_ref, hb_ref):
    bm, d_na = na_ref.shape

    def step(h_new, h_cur):
        for r0 in range(0, bm, ROW_CHUNK):
            rs = slice(r0, r0 + ROW_CHUNK)
            na = na_ref[rs, :].astype(F32)
            na_n = na * lax.rsqrt(jnp.mean(na * na, axis=-1, keepdims=True) + EPS) * bna_ref[...]
            h_new[rs, 0:d_na] = na_n.astype(BF16)
            h_new[rs, d_na:] = hyn_ref[rs, :]
        acc = jnp.dot(h_cur[...], w_ref[...], preferred_element_type=F32)
        o_ref[...] = x_ref[...] + gt_ref[0] * acc

    _skewed_steps(ha_ref, hb_ref, step)


def _mm_out(na2d, hyn2d, beta_na, w, x2d, gt, seq_len, bm=256):
    m, d_na = na2d.shape
    d_hy = hyn2d.shape[1]
    k, n = w.shape
    nb = m // bm
    blocks_per_seq = seq_len // bm
    new = lambda t: jnp.minimum(t, nb - 1)
    cur = lambda t: jnp.maximum(t - 1, 0)
    return pl.pallas_call(
        _mm_out_kernel,
        grid=(nb + 1,),
        in_specs=[pl.BlockSpec((bm, d_na), lambda t: (new(t), 0)),
                  pl.BlockSpec((bm, d_hy), lambda t: (new(t), 0)),
                  _resident((1, d_na)),
                  _resident((k, n)),
                  pl.BlockSpec((bm, n), lambda t: (cur(t), 0)),
                  pl.BlockSpec((1, 1, n), lambda t: (cur(t) // blocks_per_seq, 0, 0))],
        out_specs=pl.BlockSpec((bm, n), lambda t: (cur(t), 0)),
        out_shape=jax.ShapeDtypeStruct((m, n), F32),
        scratch_shapes=[pltpu.VMEM((bm, k), BF16), pltpu.VMEM((bm, k), BF16)],
        compiler_params=_cparams(("arbitrary",)),
        name="mm_out",
    )(na2d, hyn2d, beta_na, w, x2d, gt)


HALO = 16


def _mm_up_glu_kernel(x_ref, xp_ref, xn_ref, g_ref, sc_ref, sh_ref, wa_ref, wb_ref,
                      cw_ref, cb_ref, o_ref, ha_ref, hb_ref, *, blocks_per_seq, n_blocks):
    bm, k = x_ref.shape
    t = pl.program_id(0)
    j = pl.program_id(1)
    pos = jnp.minimum(t, n_blocks - 1) % blocks_per_seq

    def norm(x):
        gs = g_ref[...] * (1.0 + sc_ref[0])
        ms = jnp.mean(x * x, axis=-1, keepdims=True)
        return x * lax.rsqrt(ms + EPS) * gs + sh_ref[0]

    def build(h_new):
        zero = jnp.zeros((HALO, k), F32)
        h_new[0:HALO, :] = jnp.where(pos == 0, zero, norm(xp_ref[...])).astype(BF16)
        for r0 in range(0, bm, ROW_CHUNK):
            h_new[HALO + r0:HALO + r0 + ROW_CHUNK, :] = norm(
                x_ref[r0:r0 + ROW_CHUNK, :]).astype(BF16)
        h_new[HALO + bm:, :] = jnp.where(pos == blocks_per_seq - 1, zero,
                                         norm(xn_ref[...])).astype(BF16)

    def multiply(h_cur):
        a = jnp.dot(h_cur[...], wa_ref[...], preferred_element_type=F32)
        b = jnp.dot(h_cur[HALO:HALO + bm, :], wb_ref[...], preferred_element_type=F32)
        n_ext = a.shape[0]
        w = cw_ref[...]
        prev = pltpu.roll(a, 1, axis=0)[HALO:HALO + bm]
        nxt = pltpu.roll(a, n_ext - 1, axis=0)[HALO:HALO + bm]
        ac = prev * w[0:1] + a[HALO:HALO + bm] * w[1:2] + nxt * w[2:3] + cb_ref[...]
        gelu = 0.5 * ac * (1.0 + lax.erf(ac * (1.0 / math.sqrt(2.0))))
        o_ref[...] = (gelu * b).astype(o_ref.dtype)

    even = t % 2 == 0

    @pl.when((t == 0) & (j == 0))
    def _():
        build(ha_ref)

    @pl.when((t > 0) & even & (j == 0))
    def _():
        build(ha_ref)
        multiply(hb_ref)

    @pl.when((t > 0) & even & (j > 0))
    def _():
        multiply(hb_ref)

    @pl.when(jnp.logical_not(even) & (j == 0))
    def _():
        build(hb_ref)
        multiply(ha_ref)

    @pl.when(jnp.logical_not(even) & (j > 0))
    def _():
        multiply(ha_ref)


def _mm_up_glu(x2d, g, sc, sh, w_up, conv_w, conv_b, seq_len, bm=1024, bn=512):
    m, k = x2d.shape
    d_ff = w_up.shape[1] // 2
    nbn = d_ff // bn
    nb = m // bm
    blocks_per_seq = seq_len // bm
    hb = bm // HALO
    last = m // HALO - 1
    new = lambda t: jnp.minimum(t, nb - 1)
    cur = lambda t: jnp.maximum(t - 1, 0)
    col = lambda t, j: jnp.where(t == 0, 0, j)
    return pl.pallas_call(
        functools.partial(_mm_up_glu_kernel, blocks_per_seq=blocks_per_seq, n_blocks=nb),
        grid=(nb + 1, nbn),
        in_specs=[pl.BlockSpec((bm, k), lambda t, j: (new(t), 0)),
                  pl.BlockSpec((HALO, k), lambda t, j: (jnp.maximum(new(t) * hb - 1, 0), 0)),
                  pl.BlockSpec((HALO, k),
                               lambda t, j: (jnp.minimum((new(t) + 1) * hb, last), 0)),
                  pl.BlockSpec((1, k), lambda t, j: (0, 0)),
                  pl.BlockSpec((1, 1, k), lambda t, j: (new(t) // blocks_per_seq, 0, 0)),
                  pl.BlockSpec((1, 1, k), lambda t, j: (new(t) // blocks_per_seq, 0, 0)),
                  pl.BlockSpec((k, bn), lambda t, j: (0, col(t, j))),
                  pl.BlockSpec((k, bn), lambda t, j: (0, nbn + col(t, j))),
                  pl.BlockSpec((3, bn), lambda t, j: (0, col(t, j))),
                  pl.BlockSpec((1, bn), lambda t, j: (0, col(t, j)))],
        out_specs=pl.BlockSpec((bm, bn), lambda t, j: (cur(t), col(t, j))),
        out_shape=jax.ShapeDtypeStruct((m, d_ff), BF16),
        scratch_shapes=[pltpu.VMEM((bm + 2 * HALO, k), BF16),
                        pltpu.VMEM((bm + 2 * HALO, k), BF16)],
        compiler_params=_cparams(("arbitrary", "arbitrary")),
        name="mm_up_glu",
    )(x2d, x2d, x2d, g, sc, sh, w_up, w_up, conv_w, conv_b)


def _mm_down_kernel(g_ref, w_ref, x_ref, gt_ref, gf_ref, o_ref):
    acc = jnp.dot(g_ref[...], w_ref[...], preferred_element_type=F32)
    x = x_ref[...] + gt_ref[0] * acc
    ms = jnp.mean(x * x, axis=-1, keepdims=True)
    o_ref[...] = x * lax.rsqrt(ms + EPS) * gf_ref[...]


def _mm_down(g2d, w, x2d, gt, g_final, seq_len, bm=256):
    m, k = g2d.shape
    n = w.shape[1]
    blocks_per_seq = seq_len // bm
    return pl.pallas_call(
        _mm_down_kernel,
        grid=(m // bm,),
        in_specs=[pl.BlockSpec((bm, k), lambda i: (i, 0)),
                  _resident((k, n)),
                  pl.BlockSpec((bm, n), lambda i: (i, 0)),
                  pl.BlockSpec((1, 1, n), lambda i: (i // blocks_per_seq, 0, 0)),
                  _resident((1, n))],
        out_specs=pl.BlockSpec((bm, n), lambda i: (i, 0)),
        out_shape=jax.ShapeDtypeStruct((m, n), F32),
        compiler_params=_cparams(("parallel",)),
        name="mm_down",
    )(g2d, w, x2d, gt, g_final)


def _position_features(seq_len, kpad):
    t = np.linspace(0.0, 1.0, seq_len)[:, None]
    bands = (FILTER_EMB - 1) // 2
    w = 2.0 * np.pi * np.arange(seq_len)[:, None] / seq_len
    fr = np.linspace(1e-4, bands - 1, bands)[None, :]
    z = np.concatenate([t, np.cos(fr * w), -np.sin(fr * w)], axis=-1)
    z = np.pad(z, ((0, 0), (0, kpad - z.shape[1])))
    return z.astype(np.float32)


def kernel(x, c, w_ada, b_ada, g_mix, w_in, na_rpb, hy_short_w, hy_short_b,
           hy_filt_w1, hy_filt_b1, hy_filt_w2, hy_filt_b2, hy_filt_w3, hy_filt_b3,
           hy_filt_w4, hy_filt_freq, hy_bias, beta_na, beta_hy, w_out, g_ffn,
           w_up, ffn_conv_w, ffn_conv_b, w_down, g_final):
    b, l, d = x.shape
    depth = w_ada.shape[0]
    d_hy = d - D_NA
    d_ff = w_down.shape[1]
    rows = l // GRID_W
    m = b * l
    assert depth == 1 and b % 2 == 0 and 2 * l == FFT_R * FFT_S

    c_pad = jnp.pad(c, ((0, V7X_SUBLANES - b), (0, 0)))
    b_ada2 = b_ada[0][None, :]
    mod_head, bias_tab = _adaln_bias(c_pad, w_ada[0], b_ada2, na_rpb[0].reshape(-1), 2 * d)
    sh1, sc1 = [t[:b, None, :] for t in jnp.split(mod_head, 2, axis=-1)]

    x2d = x.reshape(m, d)

    col_scale = np.ones((w_in.shape[2],), np.float32)
    col_scale[:D_NA] = NA_Q_SCALE
    w_in_bf = (w_in[0] * jnp.asarray(col_scale)[None, :]).astype(BF16)
    proj, mod_tail = _mm_norm(x2d, g_mix, sc1, sh1, w_in_bf, c_pad, w_ada[0], b_ada2,
                              2 * d, l)
    gt1, sh2, sc2, gt2 = [t[:b, None, :] for t in jnp.split(mod_tail, 4, axis=-1)]

    na, (w_out_bf, w_up_bf, w_down_bf) = _na_attention(
        proj.reshape(b, l, proj.shape[1]), bias_tab, (w_out[0], w_up[0], w_down[0]))
    na2d = na.reshape(m, D_NA)

    x0, vx = _hy_pre(proj.reshape(b, l, proj.shape[1]), hy_short_w[0], hy_short_b[0][None, :], d_hy)

    kpad = V7X_LANES
    z_np = _position_features(l, kpad)
    w1p = jnp.pad(hy_filt_w1[0], ((0, kpad - FILTER_EMB), (0, 0)))
    max_decay = math.log(DECAY_TARGET) / FAST_DECAY_PCT
    min_decay = math.log(DECAY_TARGET) / SLOW_DECAY_PCT
    deltas_abs = np.abs(np.linspace(min_decay, max_decay, d_hy))[None, :].astype(np.float32)
    taps = _filter_taps(jnp.asarray(z_np), w1p, hy_filt_b1,
                        hy_filt_w2[0], hy_filt_b2, hy_filt_w3[0], hy_filt_b3,
                        hy_filt_freq, hy_filt_w4[0], jnp.asarray(deltas_abs), hy_bias, d_hy)

    w1_data, w1_filt, w3, e_fwd, e_inv = _fft_constants(l)
    a_filt = _fft_slow(w1_filt, taps.reshape(2, FFT_S // 2, FFT_R, d_hy), BF16)
    a_data = _fft_slow(w1_data, vx.reshape(b // 2, FFT_S, FFT_R, d_hy), BF16)
    g_spec = _fft_spec(e_fwd, e_inv, a_filt.reshape(2, 2, FFT_S, FFT_R, d_hy),
                       a_data.reshape(b // 2, 2, FFT_S, FFT_R, d_hy))
    hyn = _fft_slow_gate(w3, g_spec.reshape(b // 2, 2 * FFT_S, FFT_R, d_hy),
                         x0.reshape(b // 2, FFT_S, FFT_R, d_hy), beta_hy)

    x1 = _mm_out(na2d, hyn.reshape(m, d_hy), beta_na, w_out_bf, x2d, gt1, l)

    gl = _mm_up_glu(x1, g_ffn, sc2, sh2, w_up_bf, ffn_conv_w[0],
                    ffn_conv_b[0][None, :], l)
    out = _mm_down(gl, w_down_bf, x1, gt2, g_final[None, :], l)
    return out.reshape(b, l, d)
```

```python
import functools
import math

import numpy as np
import jax
import jax.numpy as jnp
from jax import lax
from jax.experimental import pallas as pl
from jax.experimental.pallas import tpu as pltpu

F32 = jnp.float32
BF16 = jnp.bfloat16

GRID_W = 64
NA_HEADS = 16
NA_HEAD_DIM = 64
D_NA = NA_HEADS * NA_HEAD_DIM
NA_WIN_ROWS = 8
NA_WIN_COLS = 16
NA_GROUP = 4
FILTER_EMB = 33
DECAY_TARGET = 1e-2
FAST_DECAY_PCT = 0.3
SLOW_DECAY_PCT = 1.5
EPS = 1e-6
NEG_BIG = -1e30
LOG2_E = math.log2(math.e)
NA_Q_SCALE = NA_HEAD_DIM ** -0.5 * LOG2_E

V7X_LANES = 128
V7X_VMEM_BYTES = 64 * 1024 * 1024
VMEM_LIMIT = 56 * 1024 * 1024

FFT_R = 128
FFT_S = 64

ROW_CHUNK = 64


def _cparams(sem, vmem=VMEM_LIMIT):
    return pltpu.CompilerParams(dimension_semantics=sem, vmem_limit_bytes=vmem)


def _for_row_chunks(n_rows, chunk, fn):
    def body(i, carry):
        fn(pl.ds(pl.multiple_of(i * chunk, chunk), chunk))
        return carry
    lax.fori_loop(0, n_rows // chunk, body, 0)


def _adaln_block(c_ref, w_ref, b_ref):
    c = c_ref[...]
    cond = c / (1.0 + jnp.exp(-c))
    return jnp.dot(cond.astype(BF16), w_ref[...].astype(BF16),
                   preferred_element_type=F32) + b_ref[...]


def _adaln_bias_kernel(c_ref, w_ref, b_ref, rpb_ref, o_ref, bias_ref):
    o_ref[...] = _adaln_block(c_ref, w_ref, b_ref)
    heads = bias_ref.shape[0]
    for hh in range(heads):
        _bias_head(rpb_ref, bias_ref, hh, pl.program_id(0) * heads + hh)


def _adaln_bias(c_pad, w_ada, b_ada, rpb_flat, n_cols, bn=1024):
    rows, d = c_pad.shape
    steps = n_cols // bn
    heads = NA_HEADS // steps
    assert heads * steps == NA_HEADS
    n_keys = NA_WIN_ROWS * GRID_W
    return pl.pallas_call(
        _adaln_bias_kernel,
        grid=(steps,),
        in_specs=[pl.BlockSpec((rows, d), lambda j: (0, 0)),
                  pl.BlockSpec((d, bn), lambda j: (0, j)),
                  pl.BlockSpec((1, bn), lambda j: (0, j)),
                  pl.BlockSpec(memory_space=pltpu.SMEM)],
        out_specs=[pl.BlockSpec((rows, bn), lambda j: (0, j)),
                   pl.BlockSpec((heads, NA_WIN_ROWS, GRID_W, n_keys), lambda j: (j, 0, 0, 0))],
        out_shape=[jax.ShapeDtypeStruct((rows, n_cols), F32),
                   jax.ShapeDtypeStruct((NA_HEADS, NA_WIN_ROWS, GRID_W, n_keys), F32)],
        compiler_params=_cparams(("parallel",)),
        name="adaln_bias",
    )(c_pad, w_ada, b_ada, rpb_flat)


def _skewed_steps(ha_ref, hb_ref, step):
    t = pl.program_id(0)

    @pl.when(t == 0)
    def _():
        hb_ref[...] = jnp.zeros_like(hb_ref)

    @pl.when(t % 2 == 0)
    def _():
        step(ha_ref, hb_ref)

    @pl.when(t % 2 == 1)
    def _():
        step(hb_ref, ha_ref)


def _resident(shape):
    return pl.BlockSpec(shape, lambda t: tuple(0 for _ in shape), pipeline_mode=pl.Buffered(1))


def _mm_norm_kernel(x_ref, g_ref, sc_ref, sh_ref, w_ref, c_ref, wt_ref, bt_ref,
                    o_ref, mt_ref, ha_ref, hb_ref):
    bm = x_ref.shape[0]

    def step(h_new, h_cur):
        gs = g_ref[...] * (1.0 + sc_ref[0])
        sh = sh_ref[0]
        for r0 in range(0, bm, ROW_CHUNK):
            x = x_ref[r0:r0 + ROW_CHUNK, :]
            ms = jnp.mean(x * x, axis=-1, keepdims=True)
            h_new[r0:r0 + ROW_CHUNK, :] = (x * lax.rsqrt(ms + EPS) * gs + sh).astype(BF16)
        o_ref[...] = jnp.dot(h_cur[...], w_ref[...],
                             preferred_element_type=F32).astype(o_ref.dtype)
        mt_ref[...] = _adaln_block(c_ref, wt_ref, bt_ref)

    _skewed_steps(ha_ref, hb_ref, step)


def _mm_norm(x2d, g, sc, sh, w, c_pad, w_ada, b_ada, tail_col0, seq_len, bm=256):
    m, k = x2d.shape
    n = w.shape[1]
    nb = m // bm
    blocks_per_seq = seq_len // bm
    rows = c_pad.shape[0]
    n_tail = w_ada.shape[1] - tail_col0
    bt = n_tail // nb
    assert bt * nb == n_tail and bt % V7X_LANES == 0 and tail_col0 % bt == 0
    new = lambda t: jnp.minimum(t, nb - 1)
    cur = lambda t: jnp.maximum(t - 1, 0)
    tail = lambda t: (0, tail_col0 // bt + new(t))
    return pl.pallas_call(
        _mm_norm_kernel,
        grid=(nb + 1,),
        in_specs=[pl.BlockSpec((bm, k), lambda t: (new(t), 0)),
                  _resident((1, k)),
                  pl.BlockSpec((1, 1, k), lambda t: (new(t) // blocks_per_seq, 0, 0)),
                  pl.BlockSpec((1, 1, k), lambda t: (new(t) // blocks_per_seq, 0, 0)),
                  _resident((k, n)),
                  _resident(c_pad.shape),
                  pl.BlockSpec((k, bt), tail),
                  pl.BlockSpec((1, bt), tail)],
        out_specs=[pl.BlockSpec((bm, n), lambda t: (cur(t), 0)),
                   pl.BlockSpec((rows, bt), lambda t: (0, new(t)))],
        out_shape=[jax.ShapeDtypeStruct((m, n), BF16),
                   jax.ShapeDtypeStruct((rows, n_tail), F32)],
        scratch_shapes=[pltpu.VMEM((bm, k), BF16), pltpu.VMEM((bm, k), BF16)],
        compiler_params=_cparams(("arbitrary",)),
        name="mm_norm",
    )(x2d, g, sc, sh, w, c_pad, w_ada, b_ada)


def _bias_head(rpb_ref, o_ref, slot, h):
    n_rows = 2 * NA_WIN_ROWS - 1
    n_cols = 2 * NA_WIN_COLS - 1
    shape = (GRID_W, 2 * GRID_W)
    lane = lax.broadcasted_iota(jnp.int32, shape, 1)
    cq = lax.broadcasted_iota(jnp.int32, shape, 0)
    ck = lane & (GRID_W - 1)
    first = lane < GRID_W
    cs = jnp.clip(cq - NA_WIN_COLS // 2, 0, GRID_W - NA_WIN_COLS)
    valid = (ck >= cs) & (ck < cs + NA_WIN_COLS)
    d = jnp.clip(ck - cq, -(NA_WIN_COLS - 1), NA_WIN_COLS - 1) + (NA_WIN_COLS - 1)
    pair = []
    for j in range(n_rows - 1):
        base0 = (h * n_rows + j) * n_cols
        base1 = base0 + n_cols
        acc = jnp.zeros(shape, F32)
        for dd in range(n_cols):
            val = jnp.where(first, rpb_ref[base0 + dd], rpb_ref[base1 + dd])
            acc = jnp.where(d == dd, val, acc)
        pair.append(jnp.where(valid, acc * LOG2_E, NEG_BIG))
    for w in range(NA_WIN_ROWS):
        for ip in range(NA_WIN_ROWS // 2):
            o_ref[slot, w, :, ip * 2 * GRID_W:(ip + 1) * 2 * GRID_W] = pair[w + 2 * ip]


NA_ROWS_PER_STEP = 8


def _na_row_start(r, rows):
    return jnp.clip(r - NA_WIN_ROWS // 2, 0, rows - NA_WIN_ROWS)


def _na_window_start(r0, rows):
    span = NA_WIN_ROWS + NA_ROWS_PER_STEP - 1
    return jnp.clip(r0 - NA_WIN_ROWS // 2, 0, rows - span)


def _cast_block(shape, n_steps):
    r, c = shape
    for f in (1, 2, 4, 8):
        row_blocks = n_steps // f
        if (n_steps % f == 0 and r % (2 * V7X_SUBLANES * row_blocks) == 0
                and c % (V7X_LANES * f) == 0):
            return (r // row_blocks, c // f), (lambda s, f=f: (s // f, s % f))
    raise ValueError(f"cannot walk {shape} in {n_steps} blocks")


def _na_kernel(q_ref, k_ref, v_ref, bias_ref, *rest, rows, n_cast):
    casts_in, o_ref, casts_out = rest[:n_cast], rest[n_cast], rest[n_cast + 1:]
    for src, dst in zip(casts_in, casts_out):
        dst[...] = src[...].astype(dst.dtype)
    _na_body(q_ref, k_ref, v_ref, bias_ref, o_ref, rows=rows)


def _na_body(q_ref, k_ref, v_ref, bias_ref, o_ref, *, rows):
    n_keys = NA_WIN_ROWS * GRID_W
    gw = NA_GROUP * NA_HEAD_DIM
    lane_head = lax.broadcasted_iota(jnp.int32, (GRID_W, gw), 1) // NA_HEAD_DIM
    r0 = pl.program_id(1) * NA_ROWS_PER_STEP
    win0 = _na_window_start(r0, rows)
    for j in range(NA_ROWS_PER_STEP):
        r = r0 + j
        rs = _na_row_start(r, rows)
        ks = pl.ds(pl.multiple_of((rs - win0) * GRID_W, GRID_W), n_keys)
        w = rs - r + (NA_WIN_ROWS - 1)
        qs = slice(j * GRID_W, (j + 1) * GRID_W)
        for g in range(NA_HEADS // NA_GROUP):
            cs = slice(g * gw, (g + 1) * gw)
            qg = q_ref[qs, cs]
            kg = k_ref[ks, cs]
            vg = v_ref[ks, cs]
            zero = jnp.zeros_like(qg)
            q4 = jnp.concatenate(
                [jnp.where(lane_head == h, qg, zero) for h in range(NA_GROUP)], axis=0)
            s = lax.dot_general(q4, kg, (((1,), (1,)), ((), ())), preferred_element_type=F32)
            bias = bias_ref[NA_GROUP * g:NA_GROUP * (g + 1), w]
            s = s + bias.reshape(NA_GROUP * GRID_W, n_keys)
            m = jnp.max(s, axis=-1, keepdims=True)
            p = jnp.exp2(s - m)
            l = jnp.sum(p, axis=-1, keepdims=True)
            o4 = jnp.dot(p.astype(BF16), vg, preferred_element_type=F32) / l
            o = o4[0:GRID_W]
            for h in range(1, NA_GROUP):
                o = jnp.where(lane_head == h, o4[h * GRID_W:(h + 1) * GRID_W], o)
            o_ref[qs, cs] = o.astype(o_ref.dtype)


def _na_attention(proj3, bias_tab, weights_f32):
    b, l, _ = proj3.shape
    rows = l // GRID_W
    rb = NA_ROWS_PER_STEP
    span = NA_WIN_ROWS + rb - 1
    steps_per_batch = rows // rb
    n_steps = b * steps_per_batch

    def kv_spec(col0):
        return pl.BlockSpec((None, pl.Element(span * GRID_W), pl.Element(D_NA)),
                            lambda bi, t: (bi, _na_window_start(t * rb, rows) * GRID_W, col0))

    cast_specs = []
    for w in weights_f32:
        blk, walk = _cast_block(w.shape, n_steps)
        cast_specs.append(pl.BlockSpec(
            blk, lambda bi, t, walk=walk: walk(bi * steps_per_batch + t)))

    outs = pl.pallas_call(
        functools.partial(_na_kernel, rows=rows, n_cast=len(weights_f32)),
        grid=(b, steps_per_batch),
        in_specs=[
            pl.BlockSpec((None, rb * GRID_W, D_NA), lambda bi, t: (bi, t, 0)),
            kv_spec(D_NA),
            kv_spec(2 * D_NA),
            pl.BlockSpec(bias_tab.shape, lambda bi, t: (0, 0, 0, 0),
                         pipeline_mode=pl.Buffered(1)),
        ] + cast_specs,
        out_specs=[pl.BlockSpec((None, rb * GRID_W, D_NA), lambda bi, t: (bi, t, 0))]
        + cast_specs,
        out_shape=[jax.ShapeDtypeStruct((b, l, D_NA), BF16)]
        + [jax.ShapeDtypeStruct(w.shape, BF16) for w in weights_f32],
        compiler_params=_cparams(("arbitrary", "arbitrary")),
        name="na_attn",
    )(proj3, proj3, proj3, bias_tab, *weights_f32)
    return outs[0], outs[1:]


def _conv3(u, w, b):
    n, c = u.shape
    t8 = 8
    row = lax.broadcasted_iota(jnp.int32, (t8, c), 0)
    prev = pltpu.roll(u, 1, axis=0)
    prev = jnp.concatenate([jnp.where(row == 0, 0.0, prev[0:t8]), prev[t8:]], axis=0)
    nxt = pltpu.roll(u, n - 1, axis=0)
    nxt = jnp.concatenate([nxt[0:n - t8], jnp.where(row == t8 - 1, 0.0, nxt[n - t8:])], axis=0)
    return prev * w[0:1] + u * w[1:2] + nxt * w[2:3] + b


def _hy_pre_kernel(u0_ref, u1_ref, u2_ref, w0_ref, w1_ref, w2_ref,
                   b0_ref, b1_ref, b2_ref, x0_ref, vx_ref):
    x0 = _conv3(u0_ref[...].astype(F32), w0_ref[...], b0_ref[...])
    x1 = _conv3(u1_ref[...].astype(F32), w1_ref[...], b1_ref[...])
    v = _conv3(u2_ref[...].astype(F32), w2_ref[...], b2_ref[...])
    x0_ref[...] = x0.astype(x0_ref.dtype)
    vx_ref[...] = (v * x1).astype(vx_ref.dtype)


def _hy_pre(proj3, short_w, short_b, d_hy, cb=256):
    b, l, _ = proj3.shape
    base = 3 * D_NA // cb
    per = d_hy // cb

    def u_spec(g):
        return pl.BlockSpec((None, l, cb), lambda bi, j: (bi, 0, base + g * per + j))

    def w_spec(g):
        return pl.BlockSpec((3, cb), lambda bi, j: (0, g * per + j))

    def b_spec(g):
        return pl.BlockSpec((1, cb), lambda bi, j: (0, g * per + j))

    out_spec = pl.BlockSpec((None, l, cb), lambda bi, j: (bi, 0, j))
    return pl.pallas_call(
        _hy_pre_kernel,
        grid=(b, per),
        in_specs=[u_spec(0), u_spec(1), u_spec(2), w_spec(0), w_spec(1), w_spec(2),
                  b_spec(0), b_spec(1), b_spec(2)],
        out_specs=[out_spec, out_spec],
        out_shape=[jax.ShapeDtypeStruct((b, l, d_hy), BF16),
                   jax.ShapeDtypeStruct((b, l, d_hy), BF16)],
        compiler_params=_cparams(("parallel", "parallel")),
        name="hy_pre",
    )(proj3, proj3, proj3, short_w, short_w, short_w, short_b, short_b, short_b)


def _filter_kernel(z_ref, w1_ref, b1_ref, w2_ref, b2_ref, w3_ref, b3_ref,
                   fq_ref, w4f_ref, w4b_ref, dl_ref, db_ref, o_ref, h_ref):
    hi = lax.Precision.HIGHEST
    seq = z_ref.shape[0]

    @pl.when(pl.program_id(0) == 0)
    def _():
        fq = fq_ref[...]

        def mlp(z):
            h = jnp.sin(fq * (jnp.dot(z, w1_ref[...], precision=hi,
                                      preferred_element_type=F32) + b1_ref[...]))
            h = jnp.sin(fq * (jnp.dot(h, w2_ref[...], precision=hi,
                                      preferred_element_type=F32) + b2_ref[...]))
            h = jnp.sin(fq * (jnp.dot(h, w3_ref[...], precision=hi,
                                      preferred_element_type=F32) + b3_ref[...]))
            return h

        def chunk(rs):
            h_ref[rs, :] = mlp(z_ref[rs, :]).astype(h_ref.dtype)

        _for_row_chunks(seq, 4 * ROW_CHUNK, chunk)

    decay = jnp.exp(-z_ref[:, 0:1] * dl_ref[...])
    h3 = h_ref[...]
    fwd = jnp.dot(h3, w4f_ref[...].astype(BF16), preferred_element_type=F32) * decay
    bwd = jnp.dot(h3, w4b_ref[...].astype(BF16), preferred_element_type=F32) * decay
    row = lax.broadcasted_iota(jnp.int32, bwd.shape, 0)
    bwd = jnp.where(row == 0, 0.0, bwd)
    fwd = jnp.where(row == 0, fwd + db_ref[...], fwd)
    o_ref[0:seq, :] = fwd.astype(o_ref.dtype)
    o_ref[seq:2 * seq, :] = bwd.astype(o_ref.dtype)


def _filter_taps(z, w1p, b1, w2, b2, w3, b3, freq, w4, deltas_abs, d_bias, d_hy, cb=256):
    seq, kpad = z.shape
    order = w2.shape[0]
    per = d_hy // cb
    full = lambda shape: pl.BlockSpec(shape, lambda j: tuple(0 for _ in shape))
    return pl.pallas_call(
        _filter_kernel,
        grid=(per,),
        in_specs=[full((seq, kpad)),
                  full((kpad, order)), full((1, order)),
                  full((order, order)), full((1, order)),
                  full((order, order)), full((1, order)),
                  full((1, order)),
                  pl.BlockSpec((order, cb), lambda j: (0, j)),
                  pl.BlockSpec((order, cb), lambda j: (0, per + j)),
                  pl.BlockSpec((1, cb), lambda j: (0, j)),
                  pl.BlockSpec((1, cb), lambda j: (0, j))],
        out_specs=pl.BlockSpec((2 * seq, cb), lambda j: (0, j)),
        out_shape=jax.ShapeDtypeStruct((2 * seq, d_hy), BF16),
        scratch_shapes=[pltpu.VMEM((seq, order), BF16)],
        compiler_params=_cparams(("arbitrary",)),
        name="hy_filter",
    )(z, w1p, b1, w2, b2, w3, b3, freq, w4, w4, deltas_abs, d_bias)


V7X_SUBLANES = 8


def _fft_slow_kernel(wk_ref, x_ref, o_ref):
    k, rb, c = x_ref.shape
    m = o_ref.shape[0]
    x = x_ref[...].astype(F32)
    outs = []
    for t in range(rb // V7X_SUBLANES):
        xs = x[:, t * V7X_SUBLANES:(t + 1) * V7X_SUBLANES, :]
        xs = xs.reshape(k * V7X_SUBLANES, c).astype(BF16)
        o = jnp.dot(wk_ref[...], xs, preferred_element_type=F32)
        outs.append(o.reshape(m, V7X_SUBLANES, c))
    o_ref[...] = jnp.concatenate(outs, axis=1).astype(o_ref.dtype)


def _fft_slow_gate_kernel(wk_ref, x_ref, x0_ref, beta_ref, o_ref):
    k, rb, c = x_ref.shape
    m = o_ref.shape[0]
    x = x_ref[...].astype(F32)
    outs = []
    for t in range(rb // V7X_SUBLANES):
        xs = x[:, t * V7X_SUBLANES:(t + 1) * V7X_SUBLANES, :]
        xs = xs.reshape(k * V7X_SUBLANES, c).astype(BF16)
        o = jnp.dot(wk_ref[...], xs, preferred_element_type=F32)
        outs.append(o.reshape(m, V7X_SUBLANES, c))
    hy = jnp.concatenate(outs, axis=1) * x0_ref[...].astype(F32)
    ms = jnp.mean(hy * hy, axis=-1, keepdims=True)
    o_ref[...] = (hy * lax.rsqrt(ms + EPS) * beta_ref[...]).astype(o_ref.dtype)


def _fft_slow_gate(wk, x4, x04, beta, rb=16):
    g, k, r, c = x4.shape
    m = wk.shape[0] // V7X_SUBLANES
    assert wk.shape[1] == k * V7X_SUBLANES and x04.shape == (g, m, r, c)
    return pl.pallas_call(
        _fft_slow_gate_kernel,
        grid=(g, r // rb),
        in_specs=[pl.BlockSpec(wk.shape, lambda gi, j: (0, 0)),
                  pl.BlockSpec((None, k, rb, c), lambda gi, j: (gi, 0, j, 0)),
                  pl.BlockSpec((None, m, rb, c), lambda gi, j: (gi, 0, j, 0)),
                  pl.BlockSpec((1, c), lambda gi, j: (0, 0))],
        out_specs=pl.BlockSpec((None, m, rb, c), lambda gi, j: (gi, 0, j, 0)),
        out_shape=jax.ShapeDtypeStruct((g, m, r, c), BF16),
        compiler_params=_cparams(("parallel", "parallel")),
        name="fft_slow_gate",
    )(wk, x4, x04, beta)


def _fft_slow(wk, x4, out_dtype, rb=16):
    g, k, r, c = x4.shape
    m = wk.shape[0] // V7X_SUBLANES
    assert wk.shape[1] == k * V7X_SUBLANES
    return pl.pallas_call(
        _fft_slow_kernel,
        grid=(g, r // rb),
        in_specs=[pl.BlockSpec(wk.shape, lambda gi, j: (0, 0)),
                  pl.BlockSpec((None, k, rb, c), lambda gi, j: (gi, 0, j, 0))],
        out_specs=pl.BlockSpec((None, m, rb, c), lambda gi, j: (gi, 0, j, 0)),
        out_shape=jax.ShapeDtypeStruct((g, m, r, c), out_dtype),
        compiler_params=_cparams(("parallel", "parallel")),
        name="fft_slow",
    )(wk, x4)


SPEC_SLABS = 2


def _fft_spec_kernel(e_ref, ei_ref, af_ref, a_ref, o_ref):
    pairs, two, slabs, r, c = a_ref.shape
    for q in range(slabs):
        e = e_ref[q]
        hf = jnp.dot(e, af_ref[0, :, q].reshape(two * r, c), preferred_element_type=F32)
        hb = jnp.dot(e, af_ref[1, :, q].reshape(two * r, c), preferred_element_type=F32)
        kr = hf[0:r] + hb[0:r]
        ki = hf[r:2 * r] - hb[r:2 * r]
        for p in range(pairs):
            b2 = jnp.dot(e, a_ref[p, :, q].reshape(two * r, c), preferred_element_type=F32)
            br, bi = b2[0:r], b2[r:2 * r]
            y2 = jnp.concatenate([br * kr - bi * ki, br * ki + bi * kr], axis=0).astype(BF16)
            g2 = jnp.dot(ei_ref[q], y2, preferred_element_type=F32)
            o_ref[p, :, q] = g2.reshape(two, r, c).astype(o_ref.dtype)


def _fft_spec(e_fwd, e_inv, a_filt, a_data):
    pairs, two, s, r, c = a_data.shape
    q = SPEC_SLABS
    return pl.pallas_call(
        _fft_spec_kernel,
        grid=(s // q,),
        in_specs=[pl.BlockSpec((q, two * r, two * r), lambda i: (i, 0, 0)),
                  pl.BlockSpec((q, two * r, two * r), lambda i: (i, 0, 0)),
                  pl.BlockSpec((2, two, q, r, c), lambda i: (0, 0, i, 0, 0)),
                  pl.BlockSpec((pairs, two, q, r, c), lambda i: (0, 0, i, 0, 0))],
        out_specs=pl.BlockSpec((pairs, two, q, r, c), lambda i: (0, 0, i, 0, 0)),
        out_shape=jax.ShapeDtypeStruct((pairs, two, s, r, c), BF16),
        compiler_params=_cparams(("parallel",)),
        name="fft_spec",
    )(e_fwd, e_inv, a_filt, a_data)


def _fft_constants(seq_len):
    n = 2 * seq_len
    r_, s_ = FFT_R, FFT_S
    assert r_ * s_ == n
    half = s_ // 2
    ks = np.arange(s_)[:, None]
    s = np.arange(s_)[None, :]
    ang = 2.0 * np.pi * ((ks * s) % s_) / s_
    fr, fi = np.cos(ang), -np.sin(ang)
    w1_data = np.block([[fr[:, :half], -fi[:, :half]], [fi[:, :half], fr[:, :half]]])
    w1_filt = np.block([[fr[:, :half]], [fi[:, :half]]])
    so = np.arange(half)[:, None]
    ko = np.arange(s_)[None, :]
    ang_i = 2.0 * np.pi * ((so * ko) % s_) / s_
    cr, ci = np.cos(ang_i), np.sin(ang_i)
    w3 = np.block([[cr, -ci], [ci, cr]]) / n
    ksv = np.arange(s_)[:, None, None]
    kr = np.arange(r_)[None, :, None]
    r = np.arange(r_)[None, None, :]
    ang_e = 2.0 * np.pi * ((r * (ksv + s_ * kr)) % n) / n
    er, ei = np.cos(ang_e), -np.sin(ang_e)
    e_fwd = np.concatenate([np.concatenate([er, -ei], axis=2),
                            np.concatenate([ei, er], axis=2)], axis=1)
    e_inv = np.transpose(e_fwd, (0, 2, 1))
    as_bf16 = lambda a: jnp.asarray(a.astype(np.float32)).astype(BF16)
    eye = np.eye(V7X_SUBLANES)
    slow = [as_bf16(np.kron(w, eye)) for w in (w1_data, w1_filt, w3)]
    return slow[0], slow[1], slow[2], as_bf16(e_fwd), as_bf16(e_inv)


def _mm_out_kernel(na_ref, hyn_ref, bna_ref, w_ref, x_ref, gt_ref, o_ref, ha_ref, hb_ref):
    bm, d_na = na_ref.shape

    def step(h_new, h_cur):
        for r0 in range(0, bm, ROW_CHUNK):
            rs = slice(r0, r0 + ROW_CHUNK)
            na = na_ref[rs, :].astype(F32)
            na_n = na * lax.rsqrt(jnp.mean(na * na, axis=-1, keepdims=True) + EPS) * bna_ref[...]
            h_new[rs, 0:d_na] = na_n.astype(BF16)
            h_new[rs, d_na:] = hyn_ref[rs, :]
        acc = jnp.dot(h_cur[...], w_ref[...], preferred_element_type=F32)
        o_ref[...] = x_ref[...] + gt_ref[0] * acc

    _skewed_steps(ha_ref, hb_ref, step)


def _mm_out(na2d, hyn2d, beta_na, w, x2d, gt, seq_len, bm=256):
    m, d_na = na2d.shape
    d_hy = hyn2d.shape[1]
    k, n = w.shape
    nb = m // bm
    blocks_per_seq = seq_len // bm
    new = lambda t: jnp.minimum(t, nb - 1)
    cur = lambda t: jnp.maximum(t - 1, 0)
    return pl.pallas_call(
        _mm_out_kernel,
        grid=(nb + 1,),
        in_specs=[pl.BlockSpec((bm, d_na), lambda t: (new(t), 0)),
                  pl.BlockSpec((bm, d_hy), lambda t: (new(t), 0)),
                  _resident((1, d_na)),
                  _resident((k, n)),
                  pl.BlockSpec((bm, n), lambda t: (cur(t), 0)),
                  pl.BlockSpec((1, 1, n), lambda t: (cur(t) // blocks_per_seq, 0, 0))],
        out_specs=pl.BlockSpec((bm, n), lambda t: (cur(t), 0)),
        out_shape=jax.ShapeDtypeStruct((m, n), F32),
        scratch_shapes=[pltpu.VMEM((bm, k), BF16), pltpu.VMEM((bm, k), BF16)],
        compiler_params=_cparams(("arbitrary",)),
        name="mm_out",
    )(na2d, hyn2d, beta_na, w, x2d, gt)


HALO = 16


def _mm_up_glu_kernel(x_ref, xp_ref, xn_ref, g_ref, sc_ref, sh_ref, wa_ref, wb_ref,
                      cw_ref, cb_ref, o_ref, ha_ref, hb_ref, *, blocks_per_seq, n_blocks):
    bm, k = x_ref.shape
    t = pl.program_id(0)
    j = pl.program_id(1)
    pos = jnp.minimum(t, n_blocks - 1) % blocks_per_seq

    def norm(x):
        gs = g_ref[...] * (1.0 + sc_ref[0])
        ms = jnp.mean(x * x, axis=-1, keepdims=True)
        return x * lax.rsqrt(ms + EPS) * gs + sh_ref[0]

    def build(h_new):
        zero = jnp.zeros((HALO, k), F32)
        h_new[0:HALO, :] = jnp.where(pos == 0, zero, norm(xp_ref[...])).astype(BF16)
        for r0 in range(0, bm, ROW_CHUNK):
            h_new[HALO + r0:HALO + r0 + ROW_CHUNK, :] = norm(
                x_ref[r0:r0 + ROW_CHUNK, :]).astype(BF16)
        h_new[HALO + bm:, :] = jnp.where(pos == blocks_per_seq - 1, zero,
                                         norm(xn_ref[...])).astype(BF16)

    def multiply(h_cur):
        a = jnp.dot(h_cur[...], wa_ref[...], preferred_element_type=F32)
        b = jnp.dot(h_cur[HALO:HALO + bm, :], wb_ref[...], preferred_element_type=F32)
        n_ext = a.shape[0]
        w = cw_ref[...]
        prev = pltpu.roll(a, 1, axis=0)[HALO:HALO + bm]
        nxt = pltpu.roll(a, n_ext - 1, axis=0)[HALO:HALO + bm]
        ac = prev * w[0:1] + a[HALO:HALO + bm] * w[1:2] + nxt * w[2:3] + cb_ref[...]
        gelu = 0.5 * ac * (1.0 + lax.erf(ac * (1.0 / math.sqrt(2.0))))
        o_ref[...] = (gelu * b).astype(o_ref.dtype)

    even = t % 2 == 0

    @pl.when((t == 0) & (j == 0))
    def _():
        build(ha_ref)

    @pl.when((t > 0) & even & (j == 0))
    def _():
        build(ha_ref)
        multiply(hb_ref)

    @pl.when((t > 0) & even & (j > 0))
    def _():
        multiply(hb_ref)

    @pl.when(jnp.logical_not(even) & (j == 0))
    def _():
        build(hb_ref)
        multiply(ha_ref)

    @pl.when(jnp.logical_not(even) & (j > 0))
    def _():
        multiply(ha_ref)


def _mm_up_glu(x2d, g, sc, sh, w_up, conv_w, conv_b, seq_len, bm=1024, bn=512):
    m, k = x2d.shape
    d_ff = w_up.shape[1] // 2
    nbn = d_ff // bn
    nb = m // bm
    blocks_per_seq = seq_len // bm
    hb = bm // HALO
    last = m // HALO - 1
    new = lambda t: jnp.minimum(t, nb - 1)
    cur = lambda t: jnp.maximum(t - 1, 0)
    col = lambda t, j: jnp.where(t == 0, 0, j)
    return pl.pallas_call(
        functools.partial(_mm_up_glu_kernel, blocks_per_seq=blocks_per_seq, n_blocks=nb),
        grid=(nb + 1, nbn),
        in_specs=[pl.BlockSpec((bm, k), lambda t, j: (new(t), 0)),
                  pl.BlockSpec((HALO, k), lambda t, j: (jnp.maximum(new(t) * hb - 1, 0), 0)),
                  pl.BlockSpec((HALO, k),
                               lambda t, j: (jnp.minimum((new(t) + 1) * hb, last), 0)),
                  pl.BlockSpec((1, k), lambda t, j: (0, 0)),
                  pl.BlockSpec((1, 1, k), lambda t, j: (new(t) // blocks_per_seq, 0, 0)),
                  pl.BlockSpec((1, 1, k), lambda t, j: (new(t) // blocks_per_seq, 0, 0)),
                  pl.BlockSpec((k, bn), lambda t, j: (0, col(t, j))),
                  pl.BlockSpec((k, bn), lambda t, j: (0, nbn + col(t, j))),
                  pl.BlockSpec((3, bn), lambda t, j: (0, col(t, j))),
                  pl.BlockSpec((1, bn), lambda t, j: (0, col(t, j)))],
        out_specs=pl.BlockSpec((bm, bn), lambda t, j: (cur(t), col(t, j))),
        out_shape=jax.ShapeDtypeStruct((m, d_ff), BF16),
        scratch_shapes=[pltpu.VMEM((bm + 2 * HALO, k), BF16),
                        pltpu.VMEM((bm + 2 * HALO, k), BF16)],
        compiler_params=_cparams(("arbitrary", "arbitrary")),
        name="mm_up_glu",
    )(x2d, x2d, x2d, g, sc, sh, w_up, w_up, conv_w, conv_b)


def _mm_down_kernel(g_ref, w_ref, x_ref, gt_ref, gf_ref, o_ref):
    acc = jnp.dot(g_ref[...], w_ref[...], preferred_element_type=F32)
    x = x_ref[...] + gt_ref[0] * acc
    ms = jnp.mean(x * x, axis=-1, keepdims=True)
    o_ref[...] = x * lax.rsqrt(ms + EPS) * gf_ref[...]


def _mm_down(g2d, w, x2d, gt, g_final, seq_len, bm=256):
    m, k = g2d.shape
    n = w.shape[1]
    blocks_per_seq = seq_len // bm
    return pl.pallas_call(
        _mm_down_kernel,
        grid=(m // bm,),
        in_specs=[pl.BlockSpec((bm, k), lambda i: (i, 0)),
                  _resident((k, n)),
                  pl.BlockSpec((bm, n), lambda i: (i, 0)),
                  pl.BlockSpec((1, 1, n), lambda i: (i // blocks_per_seq, 0, 0)),
                  _resident((1, n))],
        out_specs=pl.BlockSpec((bm, n), lambda i: (i, 0)),
        out_shape=jax.ShapeDtypeStruct((m, n), F32),
        compiler_params=_cparams(("parallel",)),
        name="mm_down",
    )(g2d, w, x2d, gt, g_final)


def _position_features(seq_len, kpad):
    t = np.linspace(0.0, 1.0, seq_len)[:, None]
    bands = (FILTER_EMB - 1) // 2
    w = 2.0 * np.pi * np.arange(seq_len)[:, None] / seq_len
    fr = np.linspace(1e-4, bands - 1, bands)[None, :]
    z = np.concatenate([t, np.cos(fr * w), -np.sin(fr * w)], axis=-1)
    z = np.pad(z, ((0, 0), (0, kpad - z.shape[1])))
    return z.astype(np.float32)


def kernel(x, c, w_ada, b_ada, g_mix, w_in, na_rpb, hy_short_w, hy_short_b,
           hy_filt_w1, hy_filt_b1, hy_filt_w2, hy_filt_b2, hy_filt_w3, hy_filt_b3,
           hy_filt_w4, hy_filt_freq, hy_bias, beta_na, beta_hy, w_out, g_ffn,
           w_up, ffn_conv_w, ffn_conv_b, w_down, g_final):
    b, l, d = x.shape
    depth = w_ada.shape[0]
    d_hy = d - D_NA
    d_ff = w_down.shape[1]
    rows = l // GRID_W
    m = b * l
    assert depth == 1 and b % 2 == 0 and 2 * l == FFT_R * FFT_S

    c_pad = jnp.pad(c, ((0, V7X_SUBLANES - b), (0, 0)))
    b_ada2 = b_ada[0][None, :]
    mod_head, bias_tab = _adaln_bias(c_pad, w_ada[0], b_ada2, na_rpb[0].reshape(-1), 2 * d)
    sh1, sc1 = [t[:b, None, :] for t in jnp.split(mod_head, 2, axis=-1)]

    x2d = x.reshape(m, d)

    col_scale = np.ones((w_in.shape[2],), np.float32)
    col_scale[:D_NA] = NA_Q_SCALE
    w_in_bf = (w_in[0] * jnp.asarray(col_scale)[None, :]).astype(BF16)
    proj, mod_tail = _mm_norm(x2d, g_mix, sc1, sh1, w_in_bf, c_pad, w_ada[0], b_ada2,
                              2 * d, l)
    gt1, sh2, sc2, gt2 = [t[:b, None, :] for t in jnp.split(mod_tail, 4, axis=-1)]

    na, (w_out_bf, w_up_bf, w_down_bf) = _na_attention(
        proj.reshape(b, l, proj.shape[1]), bias_tab, (w_out[0], w_up[0], w_down[0]))
    na2d = na.reshape(m, D_NA)

    x0, vx = _hy_pre(proj.reshape(b, l, proj.shape[1]), hy_short_w[0], hy_short_b[0][None, :], d_hy)

    kpad = V7X_LANES
    z_np = _position_features(l, kpad)
    w1p = jnp.pad(hy_filt_w1[0], ((0, kpad - FILTER_EMB), (0, 0)))
    max_decay = math.log(DECAY_TARGET) / FAST_DECAY_PCT
    min_decay = math.log(DECAY_TARGET) / SLOW_DECAY_PCT
    deltas_abs = np.abs(np.linspace(min_decay, max_decay, d_hy))[None, :].astype(np.float32)
    taps = _filter_taps(jnp.asarray(z_np), w1p, hy_filt_b1,
                        hy_filt_w2[0], hy_filt_b2, hy_filt_w3[0], hy_filt_b3,
                        hy_filt_freq, hy_filt_w4[0], jnp.asarray(deltas_abs), hy_bias, d_hy)

    w1_data, w1_filt, w3, e_fwd, e_inv = _fft_constants(l)
    a_filt = _fft_slow(w1_filt, taps.reshape(2, FFT_S // 2, FFT_R, d_hy), BF16)
    a_data = _fft_slow(w1_data, vx.reshape(b // 2, FFT_S, FFT_R, d_hy), BF16)
    g_spec = _fft_spec(e_fwd, e_inv, a_filt.reshape(2, 2, FFT_S, FFT_R, d_hy),
                       a_data.reshape(b // 2, 2, FFT_S, FFT_R, d_hy))
    hyn = _fft_slow_gate(w3, g_spec.reshape(b // 2, 2 * FFT_S, FFT_R, d_hy),
                         x0.reshape(b // 2, FFT_S, FFT_R, d_hy), beta_hy)

    x1 = _mm_out(na2d, hyn.reshape(m, d_hy), beta_na, w_out_bf, x2d, gt1, l)

    gl = _mm_up_glu(x1, g_ffn, sc2, sh2, w_up_bf, ffn_conv_w[0],
                    ffn_conv_b[0][None, :], l)
    out = _mm_down(gl, w_down_bf, x1, gt2, g_final[None, :], l)
    return out.reshape(b, l, d)
```

```python
import functools
import math

import numpy as np
import jax
import jax.numpy as jnp
from jax import lax
from jax.experimental import pallas as pl
from jax.experimental.pallas import tpu as pltpu

F32 = jnp.float32
BF16 = jnp.bfloat16

GRID_W = 64
NA_HEADS = 16
NA_HEAD_DIM = 64
D_NA = NA_HEADS * NA_HEAD_DIM
NA_WIN_ROWS = 8
NA_WIN_COLS = 16
NA_GROUP = 4
FILTER_EMB = 33
DECAY_TARGET = 1e-2
FAST_DECAY_PCT = 0.3
SLOW_DECAY_PCT = 1.5
EPS = 1e-6
NEG_BIG = -1e30
LOG2_E = math.log2(math.e)
NA_Q_SCALE = NA_HEAD_DIM ** -0.5 * LOG2_E

V7X_LANES = 128
V7X_VMEM_BYTES = 64 * 1024 * 1024
VMEM_LIMIT = 56 * 1024 * 1024

FFT_R = 128
FFT_S = 64

ROW_CHUNK = 64


def _cparams(sem, vmem=VMEM_LIMIT):
    return pltpu.CompilerParams(dimension_semantics=sem, vmem_limit_bytes=vmem)


def _for_row_chunks(n_rows, chunk, fn):
    def body(i, carry):
        fn(pl.ds(pl.multiple_of(i * chunk, chunk), chunk))
        return carry
    lax.fori_loop(0, n_rows // chunk, body, 0)


def _adaln_block(c_ref, w_ref, b_ref):
    c = c_ref[...]
    cond = c / (1.0 + jnp.exp(-c))
    return jnp.dot(cond.astype(BF16), w_ref[...].astype(BF16),
                   preferred_element_type=F32) + b_ref[...]


def _adaln_bias_kernel(c_ref, w_ref, b_ref, rpb_ref, win_ref, cs_ref, o_ref, bias_ref,
                       winb_ref):
    o_ref[...] = _adaln_block(c_ref, w_ref, b_ref)
    winb_ref[...] = (win_ref[...] * cs_ref[...]).astype(winb_ref.dtype)
    heads = bias_ref.shape[0]
    for hh in range(heads):
        _bias_head(rpb_ref, bias_ref, hh, pl.program_id(0) * heads + hh)


def _adaln_bias(c_pad, w_ada, b_ada, rpb_flat, w_in, col_scale, n_cols, bn=512):
    rows, d = c_pad.shape
    steps = n_cols // bn
    heads = NA_HEADS // steps
    kin, nin = w_in.shape
    rin = kin // steps
    assert heads * steps == NA_HEADS and rin * steps == kin and rin % (2 * V7X_SUBLANES) == 0
    n_keys = NA_WIN_ROWS * GRID_W
    return pl.pallas_call(
        _adaln_bias_kernel,
        grid=(steps,),
        in_specs=[pl.BlockSpec((rows, d), lambda j: (0, 0)),
                  pl.BlockSpec((d, bn), lambda j: (0, j)),
                  pl.BlockSpec((1, bn), lambda j: (0, j)),
                  pl.BlockSpec(memory_space=pltpu.SMEM),
                  pl.BlockSpec((rin, nin), lambda j: (j, 0)),
                  pl.BlockSpec((1, nin), lambda j: (0, 0))],
        out_specs=[pl.BlockSpec((rows, bn), lambda j: (0, j)),
                   pl.BlockSpec((heads, NA_WIN_ROWS, GRID_W, n_keys), lambda j: (j, 0, 0, 0)),
                   pl.BlockSpec((rin, nin), lambda j: (j, 0))],
        out_shape=[jax.ShapeDtypeStruct((rows, n_cols), F32),
                   jax.ShapeDtypeStruct((NA_HEADS, NA_WIN_ROWS, GRID_W, n_keys), F32),
                   jax.ShapeDtypeStruct((kin, nin), BF16)],
        compiler_params=_cparams(("parallel",)),
        name="adaln_bias",
    )(c_pad, w_ada, b_ada, rpb_flat, w_in, col_scale)


def _skewed_steps(ha_ref, hb_ref, step):
    t = pl.program_id(0)

    @pl.when(t == 0)
    def _():
        hb_ref[...] = jnp.zeros_like(hb_ref)

    @pl.when(t % 2 == 0)
    def _():
        step(ha_ref, hb_ref)

    @pl.when(t % 2 == 1)
    def _():
        step(hb_ref, ha_ref)


def _resident(shape):
    return pl.BlockSpec(shape, lambda t: tuple(0 for _ in shape), pipeline_mode=pl.Buffered(1))


def _mm_norm_kernel(x_ref, g_ref, sc_ref, sh_ref, w_ref, c_ref, wt_ref, bt_ref,
                    o_ref, mt_ref, ha_ref, hb_ref):
    bm = x_ref.shape[0]

    def step(h_new, h_cur):
        gs = g_ref[...] * (1.0 + sc_ref[0])
        sh = sh_ref[0]
        for r0 in range(0, bm, ROW_CHUNK):
            x = x_ref[r0:r0 + ROW_CHUNK, :]
            ms = jnp.mean(x * x, axis=-1, keepdims=True)
            h_new[r0:r0 + ROW_CHUNK, :] = (x * lax.rsqrt(ms + EPS) * gs + sh).astype(BF16)
        o_ref[...] = jnp.dot(h_cur[...], w_ref[...],
                             preferred_element_type=F32).astype(o_ref.dtype)
        mt_ref[...] = _adaln_block(c_ref, wt_ref, bt_ref)

    _skewed_steps(ha_ref, hb_ref, step)


def _mm_norm(x2d, g, sc, sh, w, c_pad, w_ada, b_ada, tail_col0, seq_len, bm=256):
    m, k = x2d.shape
    n = w.shape[1]
    nb = m // bm
    blocks_per_seq = seq_len // bm
    rows = c_pad.shape[0]
    n_tail = w_ada.shape[1] - tail_col0
    bt = n_tail // nb
    assert bt * nb == n_tail and bt % V7X_LANES == 0 and tail_col0 % bt == 0
    new = lambda t: jnp.minimum(t, nb - 1)
    cur = lambda t: jnp.maximum(t - 1, 0)
    tail = lambda t: (0, tail_col0 // bt + new(t))
    return pl.pallas_call(
        _mm_norm_kernel,
        grid=(nb + 1,),
        in_specs=[pl.BlockSpec((bm, k), lambda t: (new(t), 0)),
                  _resident((1, k)),
                  pl.BlockSpec((1, 1, k), lambda t: (new(t) // blocks_per_seq, 0, 0)),
                  pl.BlockSpec((1, 1, k), lambda t: (new(t) // blocks_per_seq, 0, 0)),
                  _resident((k, n)),
                  _resident(c_pad.shape),
                  pl.BlockSpec((k, bt), tail),
                  pl.BlockSpec((1, bt), tail)],
        out_specs=[pl.BlockSpec((bm, n), lambda t: (cur(t), 0)),
                   pl.BlockSpec((rows, bt), lambda t: (0, new(t)))],
        out_shape=[jax.ShapeDtypeStruct((m, n), BF16),
                   jax.ShapeDtypeStruct((rows, n_tail), F32)],
        scratch_shapes=[pltpu.VMEM((bm, k), BF16), pltpu.VMEM((bm, k), BF16)],
        compiler_params=_cparams(("arbitrary",)),
        name="mm_norm",
    )(x2d, g, sc, sh, w, c_pad, w_ada, b_ada)


def _bias_head(rpb_ref, o_ref, slot, h):
    n_rows = 2 * NA_WIN_ROWS - 1
    n_cols = 2 * NA_WIN_COLS - 1
    shape = (GRID_W, 2 * GRID_W)
    lane = lax.broadcasted_iota(jnp.int32, shape, 1)
    cq = lax.broadcasted_iota(jnp.int32, shape, 0)
    ck = lane & (GRID_W - 1)
    first = lane < GRID_W
    cs = jnp.clip(cq - NA_WIN_COLS // 2, 0, GRID_W - NA_WIN_COLS)
    valid = (ck >= cs) & (ck < cs + NA_WIN_COLS)
    d = jnp.clip(ck - cq, -(NA_WIN_COLS - 1), NA_WIN_COLS - 1) + (NA_WIN_COLS - 1)
    pair = []
    for j in range(n_rows - 1):
        base0 = (h * n_rows + j) * n_cols
        base1 = base0 + n_cols
        acc = jnp.zeros(shape, F32)
        for dd in range(n_cols):
            val = jnp.where(first, rpb_ref[base0 + dd], rpb_ref[base1 + dd])
            acc = jnp.where(d == dd, val, acc)
        pair.append(jnp.where(valid, acc * LOG2_E, NEG_BIG))
    for w in range(NA_WIN_ROWS):
        for ip in range(NA_WIN_ROWS // 2):
            o_ref[slot, w, :, ip * 2 * GRID_W:(ip + 1) * 2 * GRID_W] = pair[w + 2 * ip]


NA_ROWS_PER_STEP = 8


def _na_row_start(r, rows):
    return jnp.clip(r - NA_WIN_ROWS // 2, 0, rows - NA_WIN_ROWS)


def _na_window_start(r0, rows):
    span = NA_WIN_ROWS + NA_ROWS_PER_STEP - 1
    return jnp.clip(r0 - NA_WIN_ROWS // 2, 0, rows - span)


def _cast_block(shape, n_steps):
    r, c = shape
    for f in (1, 2, 4, 8):
        row_blocks = n_steps // f
        if (n_steps % f == 0 and r % (2 * V7X_SUBLANES * row_blocks) == 0
                and c % (V7X_LANES * f) == 0):
            return (r // row_blocks, c // f), (lambda s, f=f: (s // f, s % f))
    raise ValueError(f"cannot walk {shape} in {n_steps} blocks")


def _na_kernel(q_ref, k_ref, v_ref, bias_ref, *rest, rows, n_cast):
    casts_in, o_ref, casts_out = rest[:n_cast], rest[n_cast], rest[n_cast + 1:]
    for src, dst in zip(casts_in, casts_out):
        dst[...] = src[...].astype(dst.dtype)
    _na_body(q_ref, k_ref, v_ref, bias_ref, o_ref, rows=rows)


def _na_body(q_ref, k_ref, v_ref, bias_ref, o_ref, *, rows):
    n_keys = NA_WIN_ROWS * GRID_W
    gw = NA_GROUP * NA_HEAD_DIM
    lane_head = lax.broadcasted_iota(jnp.int32, (GRID_W, gw), 1) // NA_HEAD_DIM
    r0 = pl.program_id(1) * NA_ROWS_PER_STEP
    win0 = _na_window_start(r0, rows)
    for j in range(NA_ROWS_PER_STEP):
        r = r0 + j
        rs = _na_row_start(r, rows)
        ks = pl.ds(pl.multiple_of((rs - win0) * GRID_W, GRID_W), n_keys)
        w = rs - r + (NA_WIN_ROWS - 1)
        qs = slice(j * GRID_W, (j + 1) * GRID_W)
        for g in range(NA_HEADS // NA_GROUP):
            cs = slice(g * gw, (g + 1) * gw)
            qg = q_ref[qs, cs]
            kg = k_ref[ks, cs]
            vg = v_ref[ks, cs]
            zero = jnp.zeros_like(qg)
            q4 = jnp.concatenate(
                [jnp.where(lane_head == h, qg, zero) for h in range(NA_GROUP)], axis=0)
            s = lax.dot_general(q4, kg, (((1,), (1,)), ((), ())), preferred_element_type=F32)
            bias = bias_ref[NA_GROUP * g:NA_GROUP * (g + 1), w]
            s = s + bias.reshape(NA_GROUP * GRID_W, n_keys)
            m = jnp.max(s, axis=-1, keepdims=True)
            p = jnp.exp2(s - m)
            l = jnp.sum(p, axis=-1, keepdims=True)
            o4 = jnp.dot(p.astype(BF16), vg, preferred_element_type=F32) / l
            o = o4[0:GRID_W]
            for h in range(1, NA_GROUP):
                o = jnp.where(lane_head == h, o4[h * GRID_W:(h + 1) * GRID_W], o)
            o_ref[qs, cs] = o.astype(o_ref.dtype)


def _na_attention(proj3, bias_tab, weights_f32):
    b, l, _ = proj3.shape
    rows = l // GRID_W
    rb = NA_ROWS_PER_STEP
    span = NA_WIN_ROWS + rb - 1
    steps_per_batch = rows // rb
    n_steps = b * steps_per_batch
    step = lambda bi, t: bi * steps_per_batch + t

    def kv_spec(col0):
        return pl.BlockSpec((None, pl.Element(span * GRID_W), pl.Element(D_NA)),
                            lambda bi, t: (bi, _na_window_start(t * rb, rows) * GRID_W, col0))

    cast_specs = []
    for w in weights_f32:
        blk, walk = _cast_block(w.shape, n_steps)
        cast_specs.append(pl.BlockSpec(blk, lambda bi, t, walk=walk: walk(step(bi, t))))

    outs = pl.pallas_call(
        functools.partial(_na_kernel, rows=rows, n_cast=len(weights_f32)),
        grid=(b, steps_per_batch),
        in_specs=[
            pl.BlockSpec((None, rb * GRID_W, D_NA), lambda bi, t: (bi, t, 0)),
            kv_spec(D_NA),
            kv_spec(2 * D_NA),
            pl.BlockSpec(bias_tab.shape, lambda bi, t: (0, 0, 0, 0),
                         pipeline_mode=pl.Buffered(1)),
        ] + cast_specs,
        out_specs=[pl.BlockSpec((None, rb * GRID_W, D_NA), lambda bi, t: (bi, t, 0))]
        + cast_specs,
        out_shape=[jax.ShapeDtypeStruct((b, l, D_NA), BF16)]
        + [jax.ShapeDtypeStruct(w.shape, BF16) for w in weights_f32],
        compiler_params=_cparams(("arbitrary", "arbitrary")),
        name="na_attn",
    )(proj3, proj3, proj3, bias_tab, *weights_f32)
    return outs[0], outs[1:]


def _conv3(u, w, b):
    n, c = u.shape
    t8 = 8
    row = lax.broadcasted_iota(jnp.int32, (t8, c), 0)
    prev = pltpu.roll(u, 1, axis=0)
    prev = jnp.concatenate([jnp.where(row == 0, 0.0, prev[0:t8]), prev[t8:]], axis=0)
    nxt = pltpu.roll(u, n - 1, axis=0)
    nxt = jnp.concatenate([nxt[0:n - t8], jnp.where(row == t8 - 1, 0.0, nxt[n - t8:])], axis=0)
    return prev * w[0:1] + u * w[1:2] + nxt * w[2:3] + b


def _hy_pre_kernel(u0_ref, u1_ref, u2_ref, w0_ref, w1_ref, w2_ref,
                   b0_ref, b1_ref, b2_ref, x0_ref, vx_ref):
    x0 = _conv3(u0_ref[...].astype(F32), w0_ref[...], b0_ref[...])
    x1 = _conv3(u1_ref[...].astype(F32), w1_ref[...], b1_ref[...])
    v = _conv3(u2_ref[...].astype(F32), w2_ref[...], b2_ref[...])
    x0_ref[...] = x0.astype(x0_ref.dtype)
    vx_ref[...] = (v * x1).astype(vx_ref.dtype)


def _hy_pre(proj3, short_w, short_b, d_hy, cb=256):
    b, l, _ = proj3.shape
    base = 3 * D_NA // cb
    per = d_hy // cb

    def u_spec(g):
        return pl.BlockSpec((None, l, cb), lambda bi, j: (bi, 0, base + g * per + j))

    def w_spec(g):
        return pl.BlockSpec((3, cb), lambda bi, j: (0, g * per + j))

    def b_spec(g):
        return pl.BlockSpec((1, cb), lambda bi, j: (0, g * per + j))

    out_spec = pl.BlockSpec((None, l, cb), lambda bi, j: (bi, 0, j))
    return pl.pallas_call(
        _hy_pre_kernel,
        grid=(b, per),
        in_specs=[u_spec(0), u_spec(1), u_spec(2), w_spec(0), w_spec(1), w_spec(2),
                  b_spec(0), b_spec(1), b_spec(2)],
        out_specs=[out_spec, out_spec],
        out_shape=[jax.ShapeDtypeStruct((b, l, d_hy), BF16),
                   jax.ShapeDtypeStruct((b, l, d_hy), BF16)],
        compiler_params=_cparams(("parallel", "parallel")),
        name="hy_pre",
    )(proj3, proj3, proj3, short_w, short_w, short_w, short_b, short_b, short_b)


def _filter_kernel(z_ref, w1_ref, b1_ref, w2_ref, b2_ref, w3_ref, b3_ref,
                   fq_ref, w4f_ref, w4b_ref, dl_ref, db_ref, o_ref, h_ref):
    hi = lax.Precision.HIGHEST
    seq = z_ref.shape[0]

    @pl.when(pl.program_id(0) == 0)
    def _():
        fq = fq_ref[...]

        def mlp(z):
            h = jnp.sin(fq * (jnp.dot(z, w1_ref[...], precision=hi,
                                      preferred_element_type=F32) + b1_ref[...]))
            h = jnp.sin(fq * (jnp.dot(h, w2_ref[...], precision=hi,
                                      preferred_element_type=F32) + b2_ref[...]))
            h = jnp.sin(fq * (jnp.dot(h, w3_ref[...], precision=hi,
                                      preferred_element_type=F32) + b3_ref[...]))
            return h

        def chunk(rs):
            h_ref[rs, :] = mlp(z_ref[rs, :]).astype(h_ref.dtype)

        _for_row_chunks(seq, 4 * ROW_CHUNK, chunk)

    decay = jnp.exp(-z_ref[:, 0:1] * dl_ref[...])
    h3 = h_ref[...]
    fwd = jnp.dot(h3, w4f_ref[...].astype(BF16), preferred_element_type=F32) * decay
    bwd = jnp.dot(h3, w4b_ref[...].astype(BF16), preferred_element_type=F32) * decay
    row = lax.broadcasted_iota(jnp.int32, bwd.shape, 0)
    bwd = jnp.where(row == 0, 0.0, bwd)
    fwd = jnp.where(row == 0, fwd + db_ref[...], fwd)
    o_ref[0:seq, :] = fwd.astype(o_ref.dtype)
    o_ref[seq:2 * seq, :] = bwd.astype(o_ref.dtype)


def _filter_taps(z, w1p, b1, w2, b2, w3, b3, freq, w4, deltas_abs, d_bias, d_hy, cb=256):
    seq, kpad = z.shape
    order = w2.shape[0]
    per = d_hy // cb
    full = lambda shape: pl.BlockSpec(shape, lambda j: tuple(0 for _ in shape))
    return pl.pallas_call(
        _filter_kernel,
        grid=(per,),
        in_specs=[full((seq, kpad)),
                  full((kpad, order)), full((1, order)),
                  full((order, order)), full((1, order)),
                  full((order, order)), full((1, order)),
                  full((1, order)),
                  pl.BlockSpec((order, cb), lambda j: (0, j)),
                  pl.BlockSpec((order, cb), lambda j: (0, per + j)),
                  pl.BlockSpec((1, cb), lambda j: (0, j)),
                  pl.BlockSpec((1, cb), lambda j: (0, j))],
        out_specs=pl.BlockSpec((2 * seq, cb), lambda j: (0, j)),
        out_shape=jax.ShapeDtypeStruct((2 * seq, d_hy), BF16),
        scratch_shapes=[pltpu.VMEM((seq, order), BF16)],
        compiler_params=_cparams(("arbitrary",)),
        name="hy_filter",
    )(z, w1p, b1, w2, b2, w3, b3, freq, w4, w4, deltas_abs, d_bias)


V7X_SUBLANES = 8


def _fft_slow_kernel(wk_ref, x_ref, o_ref):
    k, rb, c = x_ref.shape
    m = o_ref.shape[0]
    x = x_ref[...].astype(F32)
    outs = []
    for t in range(rb // V7X_SUBLANES):
        xs = x[:, t * V7X_SUBLANES:(t + 1) * V7X_SUBLANES, :]
        xs = xs.reshape(k * V7X_SUBLANES, c).astype(BF16)
        o = jnp.dot(wk_ref[...], xs, preferred_element_type=F32)
        outs.append(o.reshape(m, V7X_SUBLANES, c))
    o_ref[...] = jnp.concatenate(outs, axis=1).astype(o_ref.dtype)


def _fft_slow_gate_kernel(wk_ref, x_ref, x0_ref, beta_ref, o_ref):
    k, rb, c = x_ref.shape
    m = o_ref.shape[0]
    x = x_ref[...].astype(F32)
    outs = []
    for t in range(rb // V7X_SUBLANES):
        xs = x[:, t * V7X_SUBLANES:(t + 1) * V7X_SUBLANES, :]
        xs = xs.reshape(k * V7X_SUBLANES, c).astype(BF16)
        o = jnp.dot(wk_ref[...], xs, preferred_element_type=F32)
        outs.append(o.reshape(m, V7X_SUBLANES, c))
    hy = jnp.concatenate(outs, axis=1) * x0_ref[...].astype(F32)
    ms = jnp.mean(hy * hy, axis=-1, keepdims=True)
    o_ref[...] = (hy * lax.rsqrt(ms + EPS) * beta_ref[...]).astype(o_ref.dtype)


def _fft_slow_gate(wk, x4, x04, beta, rb=16):
    g, k, r, c = x4.shape
    m = wk.shape[0] // V7X_SUBLANES
    assert wk.shape[1] == k * V7X_SUBLANES and x04.shape == (g, m, r, c)
    return pl.pallas_call(
        _fft_slow_gate_kernel,
        grid=(g, r // rb),
        in_specs=[pl.BlockSpec(wk.shape, lambda gi, j: (0, 0)),
                  pl.BlockSpec((None, k, rb, c), lambda gi, j: (gi, 0, j, 0)),
                  pl.BlockSpec((None, m, rb, c), lambda gi, j: (gi, 0, j, 0)),
                  pl.BlockSpec((1, c), lambda gi, j: (0, 0))],
        out_specs=pl.BlockSpec((None, m, rb, c), lambda gi, j: (gi, 0, j, 0)),
        out_shape=jax.ShapeDtypeStruct((g, m, r, c), BF16),
        compiler_params=_cparams(("parallel", "parallel")),
        name="fft_slow_gate",
    )(wk, x4, x04, beta)


def _fft_slow(wk, x4, out_dtype, rb=16):
    g, k, r, c = x4.shape
    m = wk.shape[0] // V7X_SUBLANES
    assert wk.shape[1] == k * V7X_SUBLANES
    return pl.pallas_call(
        _fft_slow_kernel,
        grid=(g, r // rb),
        in_specs=[pl.BlockSpec(wk.shape, lambda gi, j: (0, 0)),
                  pl.BlockSpec((None, k, rb, c), lambda gi, j: (gi, 0, j, 0))],
        out_specs=pl.BlockSpec((None, m, rb, c), lambda gi, j: (gi, 0, j, 0)),
        out_shape=jax.ShapeDtypeStruct((g, m, r, c), out_dtype),
        compiler_params=_cparams(("parallel", "parallel")),
        name="fft_slow",
    )(wk, x4)


SPEC_SLABS = 4


def _fft_spec_kernel(e_ref, ei_ref, af_ref, a_ref, o_ref):
    pairs, two, slabs, r, c = a_ref.shape
    for q in range(slabs):
        e = e_ref[q]
        hf = jnp.dot(e, af_ref[0, :, q].reshape(two * r, c), preferred_element_type=F32)
        hb = jnp.dot(e, af_ref[1, :, q].reshape(two * r, c), preferred_element_type=F32)
        kr = hf[0:r] + hb[0:r]
        ki = hf[r:2 * r] - hb[r:2 * r]
        for p in range(pairs):
            b2 = jnp.dot(e, a_ref[p, :, q].reshape(two * r, c), preferred_element_type=F32)
            br, bi = b2[0:r], b2[r:2 * r]
            y2 = jnp.concatenate([br * kr - bi * ki, br * ki + bi * kr], axis=0).astype(BF16)
            g2 = jnp.dot(ei_ref[q], y2, preferred_element_type=F32)
            o_ref[p, :, q] = g2.reshape(two, r, c).astype(o_ref.dtype)


def _fft_spec(e_fwd, e_inv, a_filt, a_data):
    pairs, two, s, r, c = a_data.shape
    q = SPEC_SLABS
    return pl.pallas_call(
        _fft_spec_kernel,
        grid=(s // q,),
        in_specs=[pl.BlockSpec((q, two * r, two * r), lambda i: (i, 0, 0)),
                  pl.BlockSpec((q, two * r, two * r), lambda i: (i, 0, 0)),
                  pl.BlockSpec((2, two, q, r, c), lambda i: (0, 0, i, 0, 0)),
                  pl.BlockSpec((pairs, two, q, r, c), lambda i: (0, 0, i, 0, 0))],
        out_specs=pl.BlockSpec((pairs, two, q, r, c), lambda i: (0, 0, i, 0, 0)),
        out_shape=jax.ShapeDtypeStruct((pairs, two, s, r, c), BF16),
        compiler_params=_cparams(("parallel",)),
        name="fft_spec",
    )(e_fwd, e_inv, a_filt, a_data)


def _fft_constants(seq_len):
    n = 2 * seq_len
    r_, s_ = FFT_R, FFT_S
    assert r_ * s_ == n
    half = s_ // 2
    ks = np.arange(s_)[:, None]
    s = np.arange(s_)[None, :]
    ang = 2.0 * np.pi * ((ks * s) % s_) / s_
    fr, fi = np.cos(ang), -np.sin(ang)
    w1_data = np.block([[fr[:, :half], -fi[:, :half]], [fi[:, :half], fr[:, :half]]])
    w1_filt = np.block([[fr[:, :half]], [fi[:, :half]]])
    so = np.arange(half)[:, None]
    ko = np.arange(s_)[None, :]
    ang_i = 2.0 * np.pi * ((so * ko) % s_) / s_
    cr, ci = np.cos(ang_i), np.sin(ang_i)
    w3 = np.block([[cr, -ci], [ci, cr]]) / n
    ksv = np.arange(s_)[:, None, None]
    kr = np.arange(r_)[None, :, None]
    r = np.arange(r_)[None, None, :]
    ang_e = 2.0 * np.pi * ((r * (ksv + s_ * kr)) % n) / n
    er, ei = np.cos(ang_e), -np.sin(ang_e)
    e_fwd = np.concatenate([np.concatenate([er, -ei], axis=2),
                            np.concatenate([ei, er], axis=2)], axis=1)
    e_inv = np.transpose(e_fwd, (0, 2, 1))
    as_bf16 = lambda a: jnp.asarray(a.astype(np.float32)).astype(BF16)
    eye = np.eye(V7X_SUBLANES)
    slow = [as_bf16(np.kron(w, eye)) for w in (w1_data, w1_filt, w3)]
    return slow[0], slow[1], slow[2], as_bf16(e_fwd), as_bf16(e_inv)


def _mm_out_kernel(na_ref, hyn_ref, bna_ref, w_ref, x_ref, gt_ref, o_ref, ha_ref, hb_ref):
    bm, d_na = na_ref.shape

    def step(h_new, h_cur):
        for r0 in range(0, bm, ROW_CHUNK):
            rs = slice(r0, r0 + ROW_CHUNK)
            na = na_ref[rs, :].astype(F32)
            na_n = na * lax.rsqrt(jnp.mean(na * na, axis=-1, keepdims=True) + EPS) * bna_ref[...]
            h_new[rs, 0:d_na] = na_n.astype(BF16)
            h_new[rs, d_na:] = hyn_ref[rs, :]
        acc = jnp.dot(h_cur[...], w_ref[...], preferred_element_type=F32)
        o_ref[...] = x_ref[...] + gt_ref[0] * acc

    _skewed_steps(ha_ref, hb_ref, step)


def _mm_out(na2d, hyn2d, beta_na, w, x2d, gt, seq_len, bm=256):
    m, d_na = na2d.shape
    d_hy = hyn2d.shape[1]
    k, n = w.shape
    nb = m // bm
    blocks_per_seq = seq_len // bm
    new = lambda t: jnp.minimum(t, nb - 1)
    cur = lambda t: jnp.maximum(t - 1, 0)
    return pl.pallas_call(
        _mm_out_kernel,
        grid=(nb + 1,),
        in_specs=[pl.BlockSpec((bm, d_na), lambda t: (new(t), 0)),
                  pl.BlockSpec((bm, d_hy), lambda t: (new(t), 0)),
                  _resident((1, d_na)),
                  _resident((k, n)),
                  pl.BlockSpec((bm, n), lambda t: (cur(t), 0)),
                  pl.BlockSpec((1, 1, n), lambda t: (cur(t) // blocks_per_seq, 0, 0))],
        out_specs=pl.BlockSpec((bm, n), lambda t: (cur(t), 0)),
        out_shape=jax.ShapeDtypeStruct((m, n), F32),
        scratch_shapes=[pltpu.VMEM((bm, k), BF16), pltpu.VMEM((bm, k), BF16)],
        compiler_params=_cparams(("arbitrary",)),
        name="mm_out",
    )(na2d, hyn2d, beta_na, w, x2d, gt)


HALO = 16


def _mm_up_glu_kernel(x_ref, xp_ref, xn_ref, g_ref, sc_ref, sh_ref, wa_ref, wb_ref,
                      cw_ref, cb_ref, o_ref, ha_ref, hb_ref, *, blocks_per_seq, n_blocks):
    bm, k = x_ref.shape
    t = pl.program_id(0)
    j = pl.program_id(1)
    pos = jnp.minimum(t, n_blocks - 1) % blocks_per_seq

    def norm(x):
        gs = g_ref[...] * (1.0 + sc_ref[0])
        ms = jnp.mean(x * x, axis=-1, keepdims=True)
        return x * lax.rsqrt(ms + EPS) * gs + sh_ref[0]

    def build(h_new):
        zero = jnp.zeros((HALO, k), F32)
        h_new[0:HALO, :] = jnp.where(pos == 0, zero, norm(xp_ref[...])).astype(BF16)
        for r0 in range(0, bm, ROW_CHUNK):
            h_new[HALO + r0:HALO + r0 + ROW_CHUNK, :] = norm(
                x_ref[r0:r0 + ROW_CHUNK, :]).astype(BF16)
        h_new[HALO + bm:, :] = jnp.where(pos == blocks_per_seq - 1, zero,
                                         norm(xn_ref[...])).astype(BF16)

    def multiply(h_cur):
        a = jnp.dot(h_cur[...], wa_ref[...], preferred_element_type=F32)
        b = jnp.dot(h_cur[HALO:HALO + bm, :], wb_ref[...], preferred_element_type=F32)
        n_ext = a.shape[0]
        w = cw_ref[...]
        prev = pltpu.roll(a, 1, axis=0)[HALO:HALO + bm]
        nxt = pltpu.roll(a, n_ext - 1, axis=0)[HALO:HALO + bm]
        ac = prev * w[0:1] + a[HALO:HALO + bm] * w[1:2] + nxt * w[2:3] + cb_ref[...]
        gelu = 0.5 * ac * (1.0 + lax.erf(ac * (1.0 / math.sqrt(2.0))))
        o_ref[...] = (gelu * b).astype(o_ref.dtype)

    even = t % 2 == 0

    @pl.when((t == 0) & (j == 0))
    def _():
        build(ha_ref)

    @pl.when((t > 0) & even & (j == 0))
    def _():
        build(ha_ref)
        multiply(hb_ref)

    @pl.when((t > 0) & even & (j > 0))
    def _():
        multiply(hb_ref)

    @pl.when(jnp.logical_not(even) & (j == 0))
    def _():
        build(hb_ref)
        multiply(ha_ref)

    @pl.when(jnp.logical_not(even) & (j > 0))
    def _():
        multiply(ha_ref)


def _mm_up_glu(x2d, g, sc, sh, w_up, conv_w, conv_b, seq_len, bm=1024, bn=512):
    m, k = x2d.shape
    d_ff = w_up.shape[1] // 2
    nbn = d_ff // bn
    nb = m // bm
    blocks_per_seq = seq_len // bm
    hb = bm // HALO
    last = m // HALO - 1
    new = lambda t: jnp.minimum(t, nb - 1)
    cur = lambda t: jnp.maximum(t - 1, 0)
    col = lambda t, j: jnp.where(t == 0, 0, j)
    return pl.pallas_call(
        functools.partial(_mm_up_glu_kernel, blocks_per_seq=blocks_per_seq, n_blocks=nb),
        grid=(nb + 1, nbn),
        in_specs=[pl.BlockSpec((bm, k), lambda t, j: (new(t), 0)),
                  pl.BlockSpec((HALO, k), lambda t, j: (jnp.maximum(new(t) * hb - 1, 0), 0)),
                  pl.BlockSpec((HALO, k),
                               lambda t, j: (jnp.minimum((new(t) + 1) * hb, last), 0)),
                  pl.BlockSpec((1, k), lambda t, j: (0, 0)),
                  pl.BlockSpec((1, 1, k), lambda t, j: (new(t) // blocks_per_seq, 0, 0)),
                  pl.BlockSpec((1, 1, k), lambda t, j: (new(t) // blocks_per_seq, 0, 0)),
                  pl.BlockSpec((k, bn), lambda t, j: (0, col(t, j))),
                  pl.BlockSpec((k, bn), lambda t, j: (0, nbn + col(t, j))),
                  pl.BlockSpec((3, bn), lambda t, j: (0, col(t, j))),
                  pl.BlockSpec((1, bn), lambda t, j: (0, col(t, j)))],
        out_specs=pl.BlockSpec((bm, bn), lambda t, j: (cur(t), col(t, j))),
        out_shape=jax.ShapeDtypeStruct((m, d_ff), BF16),
        scratch_shapes=[pltpu.VMEM((bm + 2 * HALO, k), BF16),
                        pltpu.VMEM((bm + 2 * HALO, k), BF16)],
        compiler_params=_cparams(("arbitrary", "arbitrary")),
        name="mm_up_glu",
    )(x2d, x2d, x2d, g, sc, sh, w_up, w_up, conv_w, conv_b)


def _mm_down_kernel(g_ref, w_ref, x_ref, gt_ref, gf_ref, o_ref):
    acc = jnp.dot(g_ref[...], w_ref[...], preferred_element_type=F32)
    x = x_ref[...] + gt_ref[0] * acc
    ms = jnp.mean(x * x, axis=-1, keepdims=True)
    o_ref[...] = x * lax.rsqrt(ms + EPS) * gf_ref[...]


def _mm_down(g2d, w, x2d, gt, g_final, seq_len, bm=256):
    m, k = g2d.shape
    n = w.shape[1]
    blocks_per_seq = seq_len // bm
    return pl.pallas_call(
        _mm_down_kernel,
        grid=(m // bm,),
        in_specs=[pl.BlockSpec((bm, k), lambda i: (i, 0)),
                  _resident((k, n)),
                  pl.BlockSpec((bm, n), lambda i: (i, 0)),
                  pl.BlockSpec((1, 1, n), lambda i: (i // blocks_per_seq, 0, 0)),
                  _resident((1, n))],
        out_specs=pl.BlockSpec((bm, n), lambda i: (i, 0)),
        out_shape=jax.ShapeDtypeStruct((m, n), F32),
        compiler_params=_cparams(("parallel",)),
        name="mm_down",
    )(g2d, w, x2d, gt, g_final)


def _position_features(seq_len, kpad):
    t = np.linspace(0.0, 1.0, seq_len)[:, None]
    bands = (FILTER_EMB - 1) // 2
    w = 2.0 * np.pi * np.arange(seq_len)[:, None] / seq_len
    fr = np.linspace(1e-4, bands - 1, bands)[None, :]
    z = np.concatenate([t, np.cos(fr * w), -np.sin(fr * w)], axis=-1)
    z = np.pad(z, ((0, 0), (0, kpad - z.shape[1])))
    return z.astype(np.float32)


def kernel(x, c, w_ada, b_ada, g_mix, w_in, na_rpb, hy_short_w, hy_short_b,
           hy_filt_w1, hy_filt_b1, hy_filt_w2, hy_filt_b2, hy_filt_w3, hy_filt_b3,
           hy_filt_w4, hy_filt_freq, hy_bias, beta_na, beta_hy, w_out, g_ffn,
           w_up, ffn_conv_w, ffn_conv_b, w_down, g_final):
    b, l, d = x.shape
    depth = w_ada.shape[0]
    d_hy = d - D_NA
    d_ff = w_down.shape[1]
    rows = l // GRID_W
    m = b * l
    assert depth == 1 and b % 2 == 0 and 2 * l == FFT_R * FFT_S

    c_pad = jnp.pad(c, ((0, V7X_SUBLANES - b), (0, 0)))
    b_ada2 = b_ada[0][None, :]
    col_scale = np.ones((1, w_in.shape[2]), np.float32)
    col_scale[:, :D_NA] = NA_Q_SCALE
    mod_head, bias_tab, w_in_bf = _adaln_bias(c_pad, w_ada[0], b_ada2, na_rpb[0].reshape(-1),
                                              w_in[0], jnp.asarray(col_scale), 2 * d)
    sh1, sc1 = [t[:b, None, :] for t in jnp.split(mod_head, 2, axis=-1)]

    x2d = x.reshape(m, d)

    proj, mod_tail = _mm_norm(x2d, g_mix, sc1, sh1, w_in_bf, c_pad, w_ada[0], b_ada2,
                              2 * d, l)
    gt1, sh2, sc2, gt2 = [t[:b, None, :] for t in jnp.split(mod_tail, 4, axis=-1)]

    na, (w_out_bf, w_up_bf, w_down_bf) = _na_attention(
        proj.reshape(b, l, proj.shape[1]), bias_tab, (w_out[0], w_up[0], w_down[0]))
    na2d = na.reshape(m, D_NA)

    x0, vx = _hy_pre(proj.reshape(b, l, proj.shape[1]), hy_short_w[0], hy_short_b[0][None, :], d_hy)

    kpad = V7X_LANES
    z_np = _position_features(l, kpad)
    w1p = jnp.pad(hy_filt_w1[0], ((0, kpad - FILTER_EMB), (0, 0)))
    max_decay = math.log(DECAY_TARGET) / FAST_DECAY_PCT
    min_decay = math.log(DECAY_TARGET) / SLOW_DECAY_PCT
    deltas_abs = np.abs(np.linspace(min_decay, max_decay, d_hy))[None, :].astype(np.float32)
    taps = _filter_taps(jnp.asarray(z_np), w1p, hy_filt_b1,
                        hy_filt_w2[0], hy_filt_b2, hy_filt_w3[0], hy_filt_b3,
                        hy_filt_freq, hy_filt_w4[0], jnp.asarray(deltas_abs), hy_bias, d_hy)

    w1_data, w1_filt, w3, e_fwd, e_inv = _fft_constants(l)
    a_filt = _fft_slow(w1_filt, taps.reshape(2, FFT_S // 2, FFT_R, d_hy), BF16)
    a_data = _fft_slow(w1_data, vx.reshape(b // 2, FFT_S, FFT_R, d_hy), BF16)
    g_spec = _fft_spec(e_fwd, e_inv, a_filt.reshape(2, 2, FFT_S, FFT_R, d_hy),
                       a_data.reshape(b // 2, 2, FFT_S, FFT_R, d_hy))
    hyn = _fft_slow_gate(w3, g_spec.reshape(b // 2, 2 * FFT_S, FFT_R, d_hy),
                         x0.reshape(b // 2, FFT_S, FFT_R, d_hy), beta_hy)

    x1 = _mm_out(na2d, hyn.reshape(m, d_hy), beta_na, w_out_bf, x2d, gt1, l)

    gl = _mm_up_glu(x1, g_ffn, sc2, sh2, w_up_bf, ffn_conv_w[0],
                    ffn_conv_b[0][None, :], l)
    out = _mm_down(gl, w_down_bf, x1, gt2, g_final[None, :], l)
    return out.reshape(b, l, d)
```

```python
import functools
import math

import numpy as np
import jax
import jax.numpy as jnp
from jax import lax
from jax.experimental import pallas as pl
from jax.experimental.pallas import tpu as pltpu

F32 = jnp.float32
BF16 = jnp.bfloat16

GRID_W = 64
NA_HEADS = 16
NA_HEAD_DIM = 64
D_NA = NA_HEADS * NA_HEAD_DIM
NA_WIN_ROWS = 8
NA_WIN_COLS = 16
NA_GROUP = 4
FILTER_EMB = 33
DECAY_TARGET = 1e-2
FAST_DECAY_PCT = 0.3
SLOW_DECAY_PCT = 1.5
EPS = 1e-6
NEG_BIG = -1e30
LOG2_E = math.log2(math.e)
NA_Q_SCALE = NA_HEAD_DIM ** -0.5 * LOG2_E

V7X_LANES = 128
V7X_VMEM_BYTES = 64 * 1024 * 1024
VMEM_LIMIT = 56 * 1024 * 1024

FFT_R = 128
FFT_S = 64

ROW_CHUNK = 64


def _cparams(sem, vmem=VMEM_LIMIT):
    return pltpu.CompilerParams(dimension_semantics=sem, vmem_limit_bytes=vmem)


def _for_row_chunks(n_rows, chunk, fn):
    def body(i, carry):
        fn(pl.ds(pl.multiple_of(i * chunk, chunk), chunk))
        return carry
    lax.fori_loop(0, n_rows // chunk, body, 0)


def _adaln_block(c_ref, w_ref, b_ref):
    c = c_ref[...]
    cond = c / (1.0 + jnp.exp(-c))
    return jnp.dot(cond.astype(BF16), w_ref[...].astype(BF16),
                   preferred_element_type=F32) + b_ref[...]


def _adaln_bias_kernel(c_ref, w_ref, b_ref, rpb_ref, win_ref, cs_ref, o_ref, bias_ref,
                       winb_ref):
    o_ref[...] = _adaln_block(c_ref, w_ref, b_ref)
    winb_ref[...] = (win_ref[...] * cs_ref[...]).astype(winb_ref.dtype)
    heads = bias_ref.shape[0]
    for hh in range(heads):
        _bias_head(rpb_ref, bias_ref, hh, pl.program_id(0) * heads + hh)


def _adaln_bias(c_pad, w_ada, b_ada, rpb_flat, w_in, col_scale, n_cols, bn=512):
    rows, d = c_pad.shape
    steps = n_cols // bn
    heads = NA_HEADS // steps
    kin, nin = w_in.shape
    rin = kin // steps
    assert heads * steps == NA_HEADS and rin * steps == kin and rin % (2 * V7X_SUBLANES) == 0
    n_keys = NA_WIN_ROWS * GRID_W
    return pl.pallas_call(
        _adaln_bias_kernel,
        grid=(steps,),
        in_specs=[pl.BlockSpec((rows, d), lambda j: (0, 0)),
                  pl.BlockSpec((d, bn), lambda j: (0, j)),
                  pl.BlockSpec((1, bn), lambda j: (0, j)),
                  pl.BlockSpec(memory_space=pltpu.SMEM),
                  pl.BlockSpec((rin, nin), lambda j: (j, 0)),
                  pl.BlockSpec((1, nin), lambda j: (0, 0))],
        out_specs=[pl.BlockSpec((rows, bn), lambda j: (0, j)),
                   pl.BlockSpec((heads, NA_WIN_ROWS, GRID_W, n_keys), lambda j: (j, 0, 0, 0)),
                   pl.BlockSpec((rin, nin), lambda j: (j, 0))],
        out_shape=[jax.ShapeDtypeStruct((rows, n_cols), F32),
                   jax.ShapeDtypeStruct((NA_HEADS, NA_WIN_ROWS, GRID_W, n_keys), F32),
                   jax.ShapeDtypeStruct((kin, nin), BF16)],
        compiler_params=_cparams(("parallel",)),
        name="adaln_bias",
    )(c_pad, w_ada, b_ada, rpb_flat, w_in, col_scale)


def _skewed_steps(ha_ref, hb_ref, step):
    t = pl.program_id(0)

    @pl.when(t == 0)
    def _():
        hb_ref[...] = jnp.zeros_like(hb_ref)

    @pl.when(t % 2 == 0)
    def _():
        step(ha_ref, hb_ref)

    @pl.when(t % 2 == 1)
    def _():
        step(hb_ref, ha_ref)


def _resident(shape):
    return pl.BlockSpec(shape, lambda t: tuple(0 for _ in shape), pipeline_mode=pl.Buffered(1))


def _mm_norm_kernel(x_ref, g_ref, sc_ref, sh_ref, w_ref, c_ref, wt_ref, bt_ref,
                    o_ref, mt_ref, ha_ref, hb_ref):
    bm = x_ref.shape[0]

    def step(h_new, h_cur):
        gs = g_ref[...] * (1.0 + sc_ref[0])
        sh = sh_ref[0]
        for r0 in range(0, bm, ROW_CHUNK):
            x = x_ref[r0:r0 + ROW_CHUNK, :]
            ms = jnp.mean(x * x, axis=-1, keepdims=True)
            h_new[r0:r0 + ROW_CHUNK, :] = (x * lax.rsqrt(ms + EPS) * gs + sh).astype(BF16)
        o_ref[...] = jnp.dot(h_cur[...], w_ref[...],
                             preferred_element_type=F32).astype(o_ref.dtype)
        mt_ref[...] = _adaln_block(c_ref, wt_ref, bt_ref)

    _skewed_steps(ha_ref, hb_ref, step)


def _mm_norm(x2d, g, sc, sh, w, c_pad, w_ada, b_ada, tail_col0, seq_len, bm=256):
    m, k = x2d.shape
    n = w.shape[1]
    nb = m // bm
    blocks_per_seq = seq_len // bm
    rows = c_pad.shape[0]
    n_tail = w_ada.shape[1] - tail_col0
    bt = n_tail // nb
    assert bt * nb == n_tail and bt % V7X_LANES == 0 and tail_col0 % bt == 0
    new = lambda t: jnp.minimum(t, nb - 1)
    cur = lambda t: jnp.maximum(t - 1, 0)
    tail = lambda t: (0, tail_col0 // bt + new(t))
    return pl.pallas_call(
        _mm_norm_kernel,
        grid=(nb + 1,),
        in_specs=[pl.BlockSpec((bm, k), lambda t: (new(t), 0)),
                  _resident((1, k)),
                  pl.BlockSpec((1, 1, k), lambda t: (new(t) // blocks_per_seq, 0, 0)),
                  pl.BlockSpec((1, 1, k), lambda t: (new(t) // blocks_per_seq, 0, 0)),
                  _resident((k, n)),
                  _resident(c_pad.shape),
                  pl.BlockSpec((k, bt), tail),
                  pl.BlockSpec((1, bt), tail)],
        out_specs=[pl.BlockSpec((bm, n), lambda t: (cur(t), 0)),
                   pl.BlockSpec((rows, bt), lambda t: (0, new(t)))],
        out_shape=[jax.ShapeDtypeStruct((m, n), BF16),
                   jax.ShapeDtypeStruct((rows, n_tail), F32)],
        scratch_shapes=[pltpu.VMEM((bm, k), BF16), pltpu.VMEM((bm, k), BF16)],
        compiler_params=_cparams(("arbitrary",)),
        name="mm_norm",
    )(x2d, g, sc, sh, w, c_pad, w_ada, b_ada)


def _bias_head(rpb_ref, o_ref, slot, h):
    n_rows = 2 * NA_WIN_ROWS - 1
    n_cols = 2 * NA_WIN_COLS - 1
    shape = (GRID_W, 2 * GRID_W)
    lane = lax.broadcasted_iota(jnp.int32, shape, 1)
    cq = lax.broadcasted_iota(jnp.int32, shape, 0)
    ck = lane & (GRID_W - 1)
    first = lane < GRID_W
    cs = jnp.clip(cq - NA_WIN_COLS // 2, 0, GRID_W - NA_WIN_COLS)
    valid = (ck >= cs) & (ck < cs + NA_WIN_COLS)
    d = jnp.clip(ck - cq, -(NA_WIN_COLS - 1), NA_WIN_COLS - 1) + (NA_WIN_COLS - 1)
    pair = []
    for j in range(n_rows - 1):
        base0 = (h * n_rows + j) * n_cols
        base1 = base0 + n_cols
        acc = jnp.zeros(shape, F32)
        for dd in range(n_cols):
            val = jnp.where(first, rpb_ref[base0 + dd], rpb_ref[base1 + dd])
            acc = jnp.where(d == dd, val, acc)
        pair.append(jnp.where(valid, acc * LOG2_E, NEG_BIG))
    for w in range(NA_WIN_ROWS):
        for ip in range(NA_WIN_ROWS // 2):
            o_ref[slot, w, :, ip * 2 * GRID_W:(ip + 1) * 2 * GRID_W] = pair[w + 2 * ip]


NA_ROWS_PER_STEP = 8


def _na_row_start(r, rows):
    return jnp.clip(r - NA_WIN_ROWS // 2, 0, rows - NA_WIN_ROWS)


def _na_window_start(r0, rows):
    span = NA_WIN_ROWS + NA_ROWS_PER_STEP - 1
    return jnp.clip(r0 - NA_WIN_ROWS // 2, 0, rows - span)


def _cast_block(shape, n_steps):
    r, c = shape
    for f in (1, 2, 4, 8):
        row_blocks = n_steps // f
        if (n_steps % f == 0 and r % (2 * V7X_SUBLANES * row_blocks) == 0
                and c % (V7X_LANES * f) == 0):
            return (r // row_blocks, c // f), (lambda s, f=f: (s // f, s % f))
    raise ValueError(f"cannot walk {shape} in {n_steps} blocks")


def _na_kernel(q_ref, k_ref, v_ref, bias_ref, *rest, rows, n_cast):
    casts_in, o_ref, casts_out = rest[:n_cast], rest[n_cast], rest[n_cast + 1:]
    for src, dst in zip(casts_in, casts_out):
        dst[...] = src[...].astype(dst.dtype)
    _na_body(q_ref, k_ref, v_ref, bias_ref, o_ref, rows=rows)


def _na_body(q_ref, k_ref, v_ref, bias_ref, o_ref, *, rows):
    n_keys = NA_WIN_ROWS * GRID_W
    gw = NA_GROUP * NA_HEAD_DIM
    lane_head = lax.broadcasted_iota(jnp.int32, (GRID_W, gw), 1) // NA_HEAD_DIM
    r0 = pl.program_id(1) * NA_ROWS_PER_STEP
    win0 = _na_window_start(r0, rows)
    for j in range(NA_ROWS_PER_STEP):
        r = r0 + j
        rs = _na_row_start(r, rows)
        ks = pl.ds(pl.multiple_of((rs - win0) * GRID_W, GRID_W), n_keys)
        w = rs - r + (NA_WIN_ROWS - 1)
        qs = slice(j * GRID_W, (j + 1) * GRID_W)
        for g in range(NA_HEADS // NA_GROUP):
            cs = slice(g * gw, (g + 1) * gw)
            qg = q_ref[qs, cs]
            kg = k_ref[ks, cs]
            vg = v_ref[ks, cs]
            zero = jnp.zeros_like(qg)
            q4 = jnp.concatenate(
                [jnp.where(lane_head == h, qg, zero) for h in range(NA_GROUP)], axis=0)
            s = lax.dot_general(q4, kg, (((1,), (1,)), ((), ())), preferred_element_type=F32)
            bias = bias_ref[NA_GROUP * g:NA_GROUP * (g + 1), w]
            s = s + bias.reshape(NA_GROUP * GRID_W, n_keys)
            m = jnp.max(s, axis=-1, keepdims=True)
            p = jnp.exp2(s - m)
            l = jnp.sum(p, axis=-1, keepdims=True)
            o4 = jnp.dot(p.astype(BF16), vg, preferred_element_type=F32) / l
            o = o4[0:GRID_W]
            for h in range(1, NA_GROUP):
                o = jnp.where(lane_head == h, o4[h * GRID_W:(h + 1) * GRID_W], o)
            o_ref[qs, cs] = o.astype(o_ref.dtype)


def _na_attention(proj3, bias_tab, weights_f32):
    b, l, _ = proj3.shape
    rows = l // GRID_W
    rb = NA_ROWS_PER_STEP
    span = NA_WIN_ROWS + rb - 1
    steps_per_batch = rows // rb
    n_steps = b * steps_per_batch
    step = lambda bi, t: bi * steps_per_batch + t

    def kv_spec(col0):
        return pl.BlockSpec((None, pl.Element(span * GRID_W), pl.Element(D_NA)),
                            lambda bi, t: (bi, _na_window_start(t * rb, rows) * GRID_W, col0))

    cast_specs = []
    for w in weights_f32:
        blk, walk = _cast_block(w.shape, n_steps)
        cast_specs.append(pl.BlockSpec(blk, lambda bi, t, walk=walk: walk(step(bi, t))))

    outs = pl.pallas_call(
        functools.partial(_na_kernel, rows=rows, n_cast=len(weights_f32)),
        grid=(b, steps_per_batch),
        in_specs=[
            pl.BlockSpec((None, rb * GRID_W, D_NA), lambda bi, t: (bi, t, 0)),
            kv_spec(D_NA),
            kv_spec(2 * D_NA),
            pl.BlockSpec(bias_tab.shape, lambda bi, t: (0, 0, 0, 0),
                         pipeline_mode=pl.Buffered(1)),
        ] + cast_specs,
        out_specs=[pl.BlockSpec((None, rb * GRID_W, D_NA), lambda bi, t: (bi, t, 0))]
        + cast_specs,
        out_shape=[jax.ShapeDtypeStruct((b, l, D_NA), BF16)]
        + [jax.ShapeDtypeStruct(w.shape, BF16) for w in weights_f32],
        compiler_params=_cparams(("arbitrary", "arbitrary")),
        name="na_attn",
    )(proj3, proj3, proj3, bias_tab, *weights_f32)
    return outs[0], outs[1:]


HY_ROWS = 256


def _shift_matrix(n):
    s = np.zeros((2 * n, n), np.float32)
    s[np.arange(1, n), np.arange(0, n - 1)] = 1.0
    s[n + np.arange(0, n - 1), np.arange(1, n)] = 1.0
    return s


def _hy_pre_kernel(u0_ref, u1_ref, u2_ref, w0_ref, w1_ref, w2_ref,
                   b0_ref, b1_ref, b2_ref, sh_ref, x0_ref, vx_ref):
    n, c = u0_ref.shape
    t8 = V7X_SUBLANES
    halo = 2 * t8
    row8 = lax.broadcasted_iota(jnp.int32, (t8, c), 0)
    shift = sh_ref[...]

    def conv_block(u_ref, w_ref, b_ref, r0):
        ub = u_ref[r0:r0 + HY_ROWS, :]
        pn = jnp.dot(shift, ub, preferred_element_type=F32)
        prev, nxt = pn[0:HY_ROWS], pn[HY_ROWS:]
        if r0 > 0:
            up = u_ref[r0 - halo:r0, :].astype(F32)[t8:]
            first = jnp.where(row8 == 0, pltpu.roll(up, 1, axis=0), prev[0:t8])
            prev = jnp.concatenate([first, prev[t8:]], axis=0)
        if r0 + HY_ROWS < n:
            dn = u_ref[r0 + HY_ROWS:r0 + HY_ROWS + halo, :].astype(F32)[0:t8]
            last = jnp.where(row8 == t8 - 1, pltpu.roll(dn, t8 - 1, axis=0),
                             nxt[HY_ROWS - t8:])
            nxt = jnp.concatenate([nxt[0:HY_ROWS - t8], last], axis=0)
        w = w_ref[...]
        return prev * w[0:1] + ub.astype(F32) * w[1:2] + nxt * w[2:3] + b_ref[...]

    for r0 in range(0, n, HY_ROWS):
        rs = slice(r0, r0 + HY_ROWS)
        x0 = conv_block(u0_ref, w0_ref, b0_ref, r0)
        x1 = conv_block(u1_ref, w1_ref, b1_ref, r0)
        v = conv_block(u2_ref, w2_ref, b2_ref, r0)
        x0_ref[rs, :] = x0.astype(x0_ref.dtype)
        vx_ref[rs, :] = (v * x1).astype(vx_ref.dtype)


def _hy_pre(proj3, short_w, short_b, d_hy, cb=256):
    b, l, _ = proj3.shape
    base = 3 * D_NA // cb
    per = d_hy // cb
    shift = jnp.asarray(_shift_matrix(HY_ROWS)).astype(BF16)

    def u_spec(g):
        return pl.BlockSpec((None, l, cb), lambda bi, j: (bi, 0, base + g * per + j))

    def w_spec(g):
        return pl.BlockSpec((3, cb), lambda bi, j: (0, g * per + j))

    def b_spec(g):
        return pl.BlockSpec((1, cb), lambda bi, j: (0, g * per + j))

    out_spec = pl.BlockSpec((None, l, cb), lambda bi, j: (bi, 0, j))
    return pl.pallas_call(
        _hy_pre_kernel,
        grid=(b, per),
        in_specs=[u_spec(0), u_spec(1), u_spec(2), w_spec(0), w_spec(1), w_spec(2),
                  b_spec(0), b_spec(1), b_spec(2),
                  pl.BlockSpec(shift.shape, lambda bi, j: (0, 0))],
        out_specs=[out_spec, out_spec],
        out_shape=[jax.ShapeDtypeStruct((b, l, d_hy), BF16),
                   jax.ShapeDtypeStruct((b, l, d_hy), BF16)],
        compiler_params=_cparams(("parallel", "parallel")),
        name="hy_pre",
    )(proj3, proj3, proj3, short_w, short_w, short_w, short_b, short_b, short_b, shift)


def _filter_kernel(z_ref, w1_ref, b1_ref, w2_ref, b2_ref, w3_ref, b3_ref,
                   fq_ref, w4f_ref, w4b_ref, dl_ref, db_ref, o_ref, h_ref):
    hi = lax.Precision.HIGHEST
    seq = z_ref.shape[0]

    @pl.when(pl.program_id(0) == 0)
    def _():
        fq = fq_ref[...]

        def mlp(z):
            h = jnp.sin(fq * (jnp.dot(z, w1_ref[...], precision=hi,
                                      preferred_element_type=F32) + b1_ref[...]))
            h = jnp.sin(fq * (jnp.dot(h, w2_ref[...], precision=hi,
                                      preferred_element_type=F32) + b2_ref[...]))
            h = jnp.sin(fq * (jnp.dot(h, w3_ref[...], precision=hi,
                                      preferred_element_type=F32) + b3_ref[...]))
            return h

        def chunk(rs):
            h_ref[rs, :] = mlp(z_ref[rs, :]).astype(h_ref.dtype)

        _for_row_chunks(seq, 4 * ROW_CHUNK, chunk)

    decay = jnp.exp(-z_ref[:, 0:1] * dl_ref[...])
    h3 = h_ref[...]
    fwd = jnp.dot(h3, w4f_ref[...].astype(BF16), preferred_element_type=F32) * decay
    bwd = jnp.dot(h3, w4b_ref[...].astype(BF16), preferred_element_type=F32) * decay
    row = lax.broadcasted_iota(jnp.int32, bwd.shape, 0)
    bwd = jnp.where(row == 0, 0.0, bwd)
    fwd = jnp.where(row == 0, fwd + db_ref[...], fwd)
    o_ref[0:seq, :] = fwd.astype(o_ref.dtype)
    o_ref[seq:2 * seq, :] = bwd.astype(o_ref.dtype)


def _filter_taps(z, w1p, b1, w2, b2, w3, b3, freq, w4, deltas_abs, d_bias, d_hy, cb=256):
    seq, kpad = z.shape
    order = w2.shape[0]
    per = d_hy // cb
    full = lambda shape: pl.BlockSpec(shape, lambda j: tuple(0 for _ in shape))
    return pl.pallas_call(
        _filter_kernel,
        grid=(per,),
        in_specs=[full((seq, kpad)),
                  full((kpad, order)), full((1, order)),
                  full((order, order)), full((1, order)),
                  full((order, order)), full((1, order)),
                  full((1, order)),
                  pl.BlockSpec((order, cb), lambda j: (0, j)),
                  pl.BlockSpec((order, cb), lambda j: (0, per + j)),
                  pl.BlockSpec((1, cb), lambda j: (0, j)),
                  pl.BlockSpec((1, cb), lambda j: (0, j))],
        out_specs=pl.BlockSpec((2 * seq, cb), lambda j: (0, j)),
        out_shape=jax.ShapeDtypeStruct((2 * seq, d_hy), BF16),
        scratch_shapes=[pltpu.VMEM((seq, order), BF16)],
        compiler_params=_cparams(("arbitrary",)),
        name="hy_filter",
    )(z, w1p, b1, w2, b2, w3, b3, freq, w4, w4, deltas_abs, d_bias)


V7X_SUBLANES = 8


def _fft_slow_kernel(wk_ref, x_ref, o_ref):
    k, rb, c = x_ref.shape
    m = o_ref.shape[0]
    x = x_ref[...].astype(F32)
    outs = []
    for t in range(rb // V7X_SUBLANES):
        xs = x[:, t * V7X_SUBLANES:(t + 1) * V7X_SUBLANES, :]
        xs = xs.reshape(k * V7X_SUBLANES, c).astype(BF16)
        o = jnp.dot(wk_ref[...], xs, preferred_element_type=F32)
        outs.append(o.reshape(m, V7X_SUBLANES, c))
    o_ref[...] = jnp.concatenate(outs, axis=1).astype(o_ref.dtype)


def _fft_slow_gate_kernel(wk_ref, x_ref, x0_ref, beta_ref, o_ref):
    k, rb, c = x_ref.shape
    m = o_ref.shape[0]
    x = x_ref[...].astype(F32)
    outs = []
    for t in range(rb // V7X_SUBLANES):
        xs = x[:, t * V7X_SUBLANES:(t + 1) * V7X_SUBLANES, :]
        xs = xs.reshape(k * V7X_SUBLANES, c).astype(BF16)
        o = jnp.dot(wk_ref[...], xs, preferred_element_type=F32)
        outs.append(o.reshape(m, V7X_SUBLANES, c))
    hy = jnp.concatenate(outs, axis=1) * x0_ref[...].astype(F32)
    ms = jnp.mean(hy * hy, axis=-1, keepdims=True)
    o_ref[...] = (hy * lax.rsqrt(ms + EPS) * beta_ref[...]).astype(o_ref.dtype)


def _fft_slow_gate(wk, x4, x04, beta, rb=16):
    g, k, r, c = x4.shape
    m = wk.shape[0] // V7X_SUBLANES
    assert wk.shape[1] == k * V7X_SUBLANES and x04.shape == (g, m, r, c)
    return pl.pallas_call(
        _fft_slow_gate_kernel,
        grid=(g, r // rb),
        in_specs=[pl.BlockSpec(wk.shape, lambda gi, j: (0, 0)),
                  pl.BlockSpec((None, k, rb, c), lambda gi, j: (gi, 0, j, 0)),
                  pl.BlockSpec((None, m, rb, c), lambda gi, j: (gi, 0, j, 0)),
                  pl.BlockSpec((1, c), lambda gi, j: (0, 0))],
        out_specs=pl.BlockSpec((None, m, rb, c), lambda gi, j: (gi, 0, j, 0)),
        out_shape=jax.ShapeDtypeStruct((g, m, r, c), BF16),
        compiler_params=_cparams(("parallel", "parallel")),
        name="fft_slow_gate",
    )(wk, x4, x04, beta)


def _fft_slow(wk, x4, out_dtype, rb=16):
    g, k, r, c = x4.shape
    m = wk.shape[0] // V7X_SUBLANES
    assert wk.shape[1] == k * V7X_SUBLANES
    return pl.pallas_call(
        _fft_slow_kernel,
        grid=(g, r // rb),
        in_specs=[pl.BlockSpec(wk.shape, lambda gi, j: (0, 0)),
                  pl.BlockSpec((None, k, rb, c), lambda gi, j: (gi, 0, j, 0))],
        out_specs=pl.BlockSpec((None, m, rb, c), lambda gi, j: (gi, 0, j, 0)),
        out_shape=jax.ShapeDtypeStruct((g, m, r, c), out_dtype),
        compiler_params=_cparams(("parallel", "parallel")),
        name="fft_slow",
    )(wk, x4)


SPEC_SLABS = 4


def _fft_spec_kernel(e_ref, ei_ref, af_ref, a_ref, o_ref):
    pairs, two, slabs, r, c = a_ref.shape
    for q in range(slabs):
        e = e_ref[q]
        hf = jnp.dot(e, af_ref[0, :, q].reshape(two * r, c), preferred_element_type=F32)
        hb = jnp.dot(e, af_ref[1, :, q].reshape(two * r, c), preferred_element_type=F32)
        kr = hf[0:r] + hb[0:r]
        ki = hf[r:2 * r] - hb[r:2 * r]
        for p in range(pairs):
            b2 = jnp.dot(e, a_ref[p, :, q].reshape(two * r, c), preferred_element_type=F32)
            br, bi = b2[0:r], b2[r:2 * r]
            y2 = jnp.concatenate([br * kr - bi * ki, br * ki + bi * kr], axis=0).astype(BF16)
            g2 = jnp.dot(ei_ref[q], y2, preferred_element_type=F32)
            o_ref[p, :, q] = g2.reshape(two, r, c).astype(o_ref.dtype)


def _fft_spec(e_fwd, e_inv, a_filt, a_data):
    pairs, two, s, r, c = a_data.shape
    q = SPEC_SLABS
    return pl.pallas_call(
        _fft_spec_kernel,
        grid=(s // q,),
        in_specs=[pl.BlockSpec((q, two * r, two * r), lambda i: (i, 0, 0)),
                  pl.BlockSpec((q, two * r, two * r), lambda i: (i, 0, 0)),
                  pl.BlockSpec((2, two, q, r, c), lambda i: (0, 0, i, 0, 0)),
                  pl.BlockSpec((pairs, two, q, r, c), lambda i: (0, 0, i, 0, 0))],
        out_specs=pl.BlockSpec((pairs, two, q, r, c), lambda i: (0, 0, i, 0, 0)),
        out_shape=jax.ShapeDtypeStruct((pairs, two, s, r, c), BF16),
        compiler_params=_cparams(("parallel",)),
        name="fft_spec",
    )(e_fwd, e_inv, a_filt, a_data)


def _fft_constants(seq_len):
    n = 2 * seq_len
    r_, s_ = FFT_R, FFT_S
    assert r_ * s_ == n
    half = s_ // 2
    ks = np.arange(s_)[:, None]
    s = np.arange(s_)[None, :]
    ang = 2.0 * np.pi * ((ks * s) % s_) / s_
    fr, fi = np.cos(ang), -np.sin(ang)
    w1_data = np.block([[fr[:, :half], -fi[:, :half]], [fi[:, :half], fr[:, :half]]])
    w1_filt = np.block([[fr[:, :half]], [fi[:, :half]]])
    so = np.arange(half)[:, None]
    ko = np.arange(s_)[None, :]
    ang_i = 2.0 * np.pi * ((so * ko) % s_) / s_
    cr, ci = np.cos(ang_i), np.sin(ang_i)
    w3 = np.block([[cr, -ci], [ci, cr]]) / n
    ksv = np.arange(s_)[:, None, None]
    kr = np.arange(r_)[None, :, None]
    r = np.arange(r_)[None, None, :]
    ang_e = 2.0 * np.pi * ((r * (ksv + s_ * kr)) % n) / n
    er, ei = np.cos(ang_e), -np.sin(ang_e)
    e_fwd = np.concatenate([np.concatenate([er, -ei], axis=2),
                            np.concatenate([ei, er], axis=2)], axis=1)
    e_inv = np.transpose(e_fwd, (0, 2, 1))
    as_bf16 = lambda a: jnp.asarray(a.astype(np.float32)).astype(BF16)
    eye = np.eye(V7X_SUBLANES)
    slow = [as_bf16(np.kron(w, eye)) for w in (w1_data, w1_filt, w3)]
    return slow[0], slow[1], slow[2], as_bf16(e_fwd), as_bf16(e_inv)


def _mm_out_kernel(na_ref, hyn_ref, bna_ref, w_ref, x_ref, gt_ref, o_ref, ha_ref, hb_ref):
    bm, d_na = na_ref.shape

    def step(h_new, h_cur):
        for r0 in range(0, bm, ROW_CHUNK):
            rs = slice(r0, r0 + ROW_CHUNK)
            na = na_ref[rs, :].astype(F32)
            na_n = na * lax.rsqrt(jnp.mean(na * na, axis=-1, keepdims=True) + EPS) * bna_ref[...]
            h_new[rs, 0:d_na] = na_n.astype(BF16)
            h_new[rs, d_na:] = hyn_ref[rs, :]
        acc = jnp.dot(h_cur[...], w_ref[...], preferred_element_type=F32)
        o_ref[...] = x_ref[...] + gt_ref[0] * acc

    _skewed_steps(ha_ref, hb_ref, step)


def _mm_out(na2d, hyn2d, beta_na, w, x2d, gt, seq_len, bm=256):
    m, d_na = na2d.shape
    d_hy = hyn2d.shape[1]
    k, n = w.shape
    nb = m // bm
    blocks_per_seq = seq_len // bm
    new = lambda t: jnp.minimum(t, nb - 1)
    cur = lambda t: jnp.maximum(t - 1, 0)
    return pl.pallas_call(
        _mm_out_kernel,
        grid=(nb + 1,),
        in_specs=[pl.BlockSpec((bm, d_na), lambda t: (new(t), 0)),
                  pl.BlockSpec((bm, d_hy), lambda t: (new(t), 0)),
                  _resident((1, d_na)),
                  _resident((k, n)),
                  pl.BlockSpec((bm, n), lambda t: (cur(t), 0)),
                  pl.BlockSpec((1, 1, n), lambda t: (cur(t) // blocks_per_seq, 0, 0))],
        out_specs=pl.BlockSpec((bm, n), lambda t: (cur(t), 0)),
        out_shape=jax.ShapeDtypeStruct((m, n), F32),
        scratch_shapes=[pltpu.VMEM((bm, k), BF16), pltpu.VMEM((bm, k), BF16)],
        compiler_params=_cparams(("arbitrary",)),
        name="mm_out",
    )(na2d, hyn2d, beta_na, w, x2d, gt)


HALO = 16


def _mm_up_glu_kernel(x_ref, xp_ref, xn_ref, g_ref, sc_ref, sh_ref, wa_ref, wb_ref,
                      cw_ref, cb_ref, o_ref, ha_ref, hb_ref, *, blocks_per_seq, n_blocks):
    bm, k = x_ref.shape
    t = pl.program_id(0)
    j = pl.program_id(1)
    pos = jnp.minimum(t, n_blocks - 1) % blocks_per_seq

    def norm(x):
        gs = g_ref[...] * (1.0 + sc_ref[0])
        ms = jnp.mean(x * x, axis=-1, keepdims=True)
        return x * lax.rsqrt(ms + EPS) * gs + sh_ref[0]

    def build(h_new):
        zero = jnp.zeros((HALO, k), F32)
        h_new[0:HALO, :] = jnp.where(pos == 0, zero, norm(xp_ref[...])).astype(BF16)
        for r0 in range(0, bm, ROW_CHUNK):
            h_new[HALO + r0:HALO + r0 + ROW_CHUNK, :] = norm(
                x_ref[r0:r0 + ROW_CHUNK, :]).astype(BF16)
        h_new[HALO + bm:, :] = jnp.where(pos == blocks_per_seq - 1, zero,
                                         norm(xn_ref[...])).astype(BF16)

    def multiply(h_cur):
        a = jnp.dot(h_cur[...], wa_ref[...], preferred_element_type=F32)
        b = jnp.dot(h_cur[HALO:HALO + bm, :], wb_ref[...], preferred_element_type=F32)
        n_ext = a.shape[0]
        w = cw_ref[...]
        prev = pltpu.roll(a, 1, axis=0)[HALO:HALO + bm]
        nxt = pltpu.roll(a, n_ext - 1, axis=0)[HALO:HALO + bm]
        ac = prev * w[0:1] + a[HALO:HALO + bm] * w[1:2] + nxt * w[2:3] + cb_ref[...]
        gelu = 0.5 * ac * (1.0 + lax.erf(ac * (1.0 / math.sqrt(2.0))))
        o_ref[...] = (gelu * b).astype(o_ref.dtype)

    even = t % 2 == 0

    @pl.when((t == 0) & (j == 0))
    def _():
        build(ha_ref)

    @pl.when((t > 0) & even & (j == 0))
    def _():
        build(ha_ref)
        multiply(hb_ref)

    @pl.when((t > 0) & even & (j > 0))
    def _():
        multiply(hb_ref)

    @pl.when(jnp.logical_not(even) & (j == 0))
    def _():
        build(hb_ref)
        multiply(ha_ref)

    @pl.when(jnp.logical_not(even) & (j > 0))
    def _():
        multiply(ha_ref)


def _mm_up_glu(x2d, g, sc, sh, w_up, conv_w, conv_b, seq_len, bm=1024, bn=512):
    m, k = x2d.shape
    d_ff = w_up.shape[1] // 2
    nbn = d_ff // bn
    nb = m // bm
    blocks_per_seq = seq_len // bm
    hb = bm // HALO
    last = m // HALO - 1
    new = lambda t: jnp.minimum(t, nb - 1)
    cur = lambda t: jnp.maximum(t - 1, 0)
    col = lambda t, j: jnp.where(t == 0, 0, j)
    return pl.pallas_call(
        functools.partial(_mm_up_glu_kernel, blocks_per_seq=blocks_per_seq, n_blocks=nb),
        grid=(nb + 1, nbn),
        in_specs=[pl.BlockSpec((bm, k), lambda t, j: (new(t), 0)),
                  pl.BlockSpec((HALO, k), lambda t, j: (jnp.maximum(new(t) * hb - 1, 0), 0)),
                  pl.BlockSpec((HALO, k),
                               lambda t, j: (jnp.minimum((new(t) + 1) * hb, last), 0)),
                  pl.BlockSpec((1, k), lambda t, j: (0, 0)),
                  pl.BlockSpec((1, 1, k), lambda t, j: (new(t) // blocks_per_seq, 0, 0)),
                  pl.BlockSpec((1, 1, k), lambda t, j: (new(t) // blocks_per_seq, 0, 0)),
                  pl.BlockSpec((k, bn), lambda t, j: (0, col(t, j))),
                  pl.BlockSpec((k, bn), lambda t, j: (0, nbn + col(t, j))),
                  pl.BlockSpec((3, bn), lambda t, j: (0, col(t, j))),
                  pl.BlockSpec((1, bn), lambda t, j: (0, col(t, j)))],
        out_specs=pl.BlockSpec((bm, bn), lambda t, j: (cur(t), col(t, j))),
        out_shape=jax.ShapeDtypeStruct((m, d_ff), BF16),
        scratch_shapes=[pltpu.VMEM((bm + 2 * HALO, k), BF16),
                        pltpu.VMEM((bm + 2 * HALO, k), BF16)],
        compiler_params=_cparams(("arbitrary", "arbitrary")),
        name="mm_up_glu",
    )(x2d, x2d, x2d, g, sc, sh, w_up, w_up, conv_w, conv_b)


def _mm_down_kernel(g_ref, w_ref, x_ref, gt_ref, gf_ref, o_ref):
    acc = jnp.dot(g_ref[...], w_ref[...], preferred_element_type=F32)
    x = x_ref[...] + gt_ref[0] * acc
    ms = jnp.mean(x * x, axis=-1, keepdims=True)
    o_ref[...] = x * lax.rsqrt(ms + EPS) * gf_ref[...]


def _mm_down(g2d, w, x2d, gt, g_final, seq_len, bm=256):
    m, k = g2d.shape
    n = w.shape[1]
    blocks_per_seq = seq_len // bm
    return pl.pallas_call(
        _mm_down_kernel,
        grid=(m // bm,),
        in_specs=[pl.BlockSpec((bm, k), lambda i: (i, 0)),
                  _resident((k, n)),
                  pl.BlockSpec((bm, n), lambda i: (i, 0)),
                  pl.BlockSpec((1, 1, n), lambda i: (i // blocks_per_seq, 0, 0)),
                  _resident((1, n))],
        out_specs=pl.BlockSpec((bm, n), lambda i: (i, 0)),
        out_shape=jax.ShapeDtypeStruct((m, n), F32),
        compiler_params=_cparams(("parallel",)),
        name="mm_down",
    )(g2d, w, x2d, gt, g_final)


def _position_features(seq_len, kpad):
    t = np.linspace(0.0, 1.0, seq_len)[:, None]
    bands = (FILTER_EMB - 1) // 2
    w = 2.0 * np.pi * np.arange(seq_len)[:, None] / seq_len
    fr = np.linspace(1e-4, bands - 1, bands)[None, :]
    z = np.concatenate([t, np.cos(fr * w), -np.sin(fr * w)], axis=-1)
    z = np.pad(z, ((0, 0), (0, kpad - z.shape[1])))
    return z.astype(np.float32)


def kernel(x, c, w_ada, b_ada, g_mix, w_in, na_rpb, hy_short_w, hy_short_b,
           hy_filt_w1, hy_filt_b1, hy_filt_w2, hy_filt_b2, hy_filt_w3, hy_filt_b3,
           hy_filt_w4, hy_filt_freq, hy_bias, beta_na, beta_hy, w_out, g_ffn,
           w_up, ffn_conv_w, ffn_conv_b, w_down, g_final):
    b, l, d = x.shape
    depth = w_ada.shape[0]
    d_hy = d - D_NA
    d_ff = w_down.shape[1]
    rows = l // GRID_W
    m = b * l
    assert depth == 1 and b % 2 == 0 and 2 * l == FFT_R * FFT_S

    c_pad = jnp.pad(c, ((0, V7X_SUBLANES - b), (0, 0)))
    b_ada2 = b_ada[0][None, :]
    col_scale = np.ones((1, w_in.shape[2]), np.float32)
    col_scale[:, :D_NA] = NA_Q_SCALE
    mod_head, bias_tab, w_in_bf = _adaln_bias(c_pad, w_ada[0], b_ada2, na_rpb[0].reshape(-1),
                                              w_in[0], jnp.asarray(col_scale), 2 * d)
    sh1, sc1 = [t[:b, None, :] for t in jnp.split(mod_head, 2, axis=-1)]

    x2d = x.reshape(m, d)

    proj, mod_tail = _mm_norm(x2d, g_mix, sc1, sh1, w_in_bf, c_pad, w_ada[0], b_ada2,
                              2 * d, l)
    gt1, sh2, sc2, gt2 = [t[:b, None, :] for t in jnp.split(mod_tail, 4, axis=-1)]

    na, (w_out_bf, w_up_bf, w_down_bf) = _na_attention(
        proj.reshape(b, l, proj.shape[1]), bias_tab, (w_out[0], w_up[0], w_down[0]))
    na2d = na.reshape(m, D_NA)

    x0, vx = _hy_pre(proj.reshape(b, l, proj.shape[1]), hy_short_w[0], hy_short_b[0][None, :], d_hy)

    kpad = V7X_LANES
    z_np = _position_features(l, kpad)
    w1p = jnp.pad(hy_filt_w1[0], ((0, kpad - FILTER_EMB), (0, 0)))
    max_decay = math.log(DECAY_TARGET) / FAST_DECAY_PCT
    min_decay = math.log(DECAY_TARGET) / SLOW_DECAY_PCT
    deltas_abs = np.abs(np.linspace(min_decay, max_decay, d_hy))[None, :].astype(np.float32)
    taps = _filter_taps(jnp.asarray(z_np), w1p, hy_filt_b1,
                        hy_filt_w2[0], hy_filt_b2, hy_filt_w3[0], hy_filt_b3,
                        hy_filt_freq, hy_filt_w4[0], jnp.asarray(deltas_abs), hy_bias, d_hy)

    w1_data, w1_filt, w3, e_fwd, e_inv = _fft_constants(l)
    a_filt = _fft_slow(w1_filt, taps.reshape(2, FFT_S // 2, FFT_R, d_hy), BF16)
    a_data = _fft_slow(w1_data, vx.reshape(b // 2, FFT_S, FFT_R, d_hy), BF16)
    g_spec = _fft_spec(e_fwd, e_inv, a_filt.reshape(2, 2, FFT_S, FFT_R, d_hy),
                       a_data.reshape(b // 2, 2, FFT_S, FFT_R, d_hy))
    hyn = _fft_slow_gate(w3, g_spec.reshape(b // 2, 2 * FFT_S, FFT_R, d_hy),
                         x0.reshape(b // 2, FFT_S, FFT_R, d_hy), beta_hy)

    x1 = _mm_out(na2d, hyn.reshape(m, d_hy), beta_na, w_out_bf, x2d, gt1, l)

    gl = _mm_up_glu(x1, g_ffn, sc2, sh2, w_up_bf, ffn_conv_w[0],
                    ffn_conv_b[0][None, :], l)
    out = _mm_down(gl, w_down_bf, x1, gt2, g_final[None, :], l)
    return out.reshape(b, l, d)
```

```python
import functools
import math

import numpy as np
import jax
import jax.numpy as jnp
from jax import lax
from jax.experimental import pallas as pl
from jax.experimental.pallas import tpu as pltpu

F32 = jnp.float32
BF16 = jnp.bfloat16

GRID_W = 64
NA_HEADS = 16
NA_HEAD_DIM = 64
D_NA = NA_HEADS * NA_HEAD_DIM
NA_WIN_ROWS = 8
NA_WIN_COLS = 16
NA_GROUP = 4
FILTER_EMB = 33
DECAY_TARGET = 1e-2
FAST_DECAY_PCT = 0.3
SLOW_DECAY_PCT = 1.5
EPS = 1e-6
NEG_BIG = -1e30
LOG2_E = math.log2(math.e)
NA_Q_SCALE = NA_HEAD_DIM ** -0.5 * LOG2_E

V7X_LANES = 128
V7X_VMEM_BYTES = 64 * 1024 * 1024
VMEM_LIMIT = 56 * 1024 * 1024

FFT_R = 128
FFT_S = 64

ROW_CHUNK = 64


def _cparams(sem, vmem=VMEM_LIMIT):
    return pltpu.CompilerParams(dimension_semantics=sem, vmem_limit_bytes=vmem)


def _for_row_chunks(n_rows, chunk, fn):
    def body(i, carry):
        fn(pl.ds(pl.multiple_of(i * chunk, chunk), chunk))
        return carry
    lax.fori_loop(0, n_rows // chunk, body, 0)


def _adaln_block(c_ref, w_ref, b_ref):
    c = c_ref[...]
    cond = c / (1.0 + jnp.exp(-c))
    return jnp.dot(cond.astype(BF16), w_ref[...].astype(BF16),
                   preferred_element_type=F32) + b_ref[...]


def _adaln_bias_kernel(c_ref, w_ref, b_ref, rpb_ref, win_ref, cs_ref, o_ref, bias_ref,
                       winb_ref):
    o_ref[...] = _adaln_block(c_ref, w_ref, b_ref)
    winb_ref[...] = (win_ref[...] * cs_ref[...]).astype(winb_ref.dtype)
    heads = bias_ref.shape[0]
    for hh in range(heads):
        _bias_head(rpb_ref, bias_ref, hh, pl.program_id(0) * heads + hh)


def _adaln_bias(c_pad, w_ada, b_ada, rpb_flat, w_in, col_scale, n_cols, bn=512):
    rows, d = c_pad.shape
    steps = n_cols // bn
    heads = NA_HEADS // steps
    kin, nin = w_in.shape
    rin = kin // steps
    assert heads * steps == NA_HEADS and rin * steps == kin and rin % (2 * V7X_SUBLANES) == 0
    n_keys = NA_WIN_ROWS * GRID_W
    return pl.pallas_call(
        _adaln_bias_kernel,
        grid=(steps,),
        in_specs=[pl.BlockSpec((rows, d), lambda j: (0, 0)),
                  pl.BlockSpec((d, bn), lambda j: (0, j)),
                  pl.BlockSpec((1, bn), lambda j: (0, j)),
                  pl.BlockSpec(memory_space=pltpu.SMEM),
                  pl.BlockSpec((rin, nin), lambda j: (j, 0)),
                  pl.BlockSpec((1, nin), lambda j: (0, 0))],
        out_specs=[pl.BlockSpec((rows, bn), lambda j: (0, j)),
                   pl.BlockSpec((heads, NA_WIN_ROWS, GRID_W, n_keys), lambda j: (j, 0, 0, 0)),
                   pl.BlockSpec((rin, nin), lambda j: (j, 0))],
        out_shape=[jax.ShapeDtypeStruct((rows, n_cols), F32),
                   jax.ShapeDtypeStruct((NA_HEADS, NA_WIN_ROWS, GRID_W, n_keys), F32),
                   jax.ShapeDtypeStruct((kin, nin), BF16)],
        compiler_params=_cparams(("parallel",)),
        name="adaln_bias",
    )(c_pad, w_ada, b_ada, rpb_flat, w_in, col_scale)


def _skewed_steps(ha_ref, hb_ref, step):
    t = pl.program_id(0)

    @pl.when(t == 0)
    def _():
        hb_ref[...] = jnp.zeros_like(hb_ref)

    @pl.when(t % 2 == 0)
    def _():
        step(ha_ref, hb_ref)

    @pl.when(t % 2 == 1)
    def _():
        step(hb_ref, ha_ref)


def _resident(shape):
    return pl.BlockSpec(shape, lambda t: tuple(0 for _ in shape), pipeline_mode=pl.Buffered(1))


def _mm_norm_kernel(x_ref, g_ref, sc_ref, sh_ref, w_ref, c_ref, wt_ref, bt_ref,
                    o_ref, mt_ref, ha_ref, hb_ref):
    bm = x_ref.shape[0]

    def step(h_new, h_cur):
        gs = g_ref[...] * (1.0 + sc_ref[0])
        sh = sh_ref[0]
        for r0 in range(0, bm, ROW_CHUNK):
            x = x_ref[r0:r0 + ROW_CHUNK, :]
            ms = jnp.mean(x * x, axis=-1, keepdims=True)
            h_new[r0:r0 + ROW_CHUNK, :] = (x * lax.rsqrt(ms + EPS) * gs + sh).astype(BF16)
        o_ref[...] = jnp.dot(h_cur[...], w_ref[...],
                             preferred_element_type=F32).astype(o_ref.dtype)
        mt_ref[...] = _adaln_block(c_ref, wt_ref, bt_ref)

    _skewed_steps(ha_ref, hb_ref, step)


def _mm_norm(x2d, g, sc, sh, w, c_pad, w_ada, b_ada, tail_col0, seq_len, bm=256):
    m, k = x2d.shape
    n = w.shape[1]
    nb = m // bm
    blocks_per_seq = seq_len // bm
    rows = c_pad.shape[0]
    n_tail = w_ada.shape[1] - tail_col0
    bt = n_tail // nb
    assert bt * nb == n_tail and bt % V7X_LANES == 0 and tail_col0 % bt == 0
    new = lambda t: jnp.minimum(t, nb - 1)
    cur = lambda t: jnp.maximum(t - 1, 0)
    tail = lambda t: (0, tail_col0 // bt + new(t))
    return pl.pallas_call(
        _mm_norm_kernel,
        grid=(nb + 1,),
        in_specs=[pl.BlockSpec((bm, k), lambda t: (new(t), 0)),
                  _resident((1, k)),
                  pl.BlockSpec((1, 1, k), lambda t: (new(t) // blocks_per_seq, 0, 0)),
                  pl.BlockSpec((1, 1, k), lambda t: (new(t) // blocks_per_seq, 0, 0)),
                  _resident((k, n)),
                  _resident(c_pad.shape),
                  pl.BlockSpec((k, bt), tail),
                  pl.BlockSpec((1, bt), tail)],
        out_specs=[pl.BlockSpec((bm, n), lambda t: (cur(t), 0)),
                   pl.BlockSpec((rows, bt), lambda t: (0, new(t)))],
        out_shape=[jax.ShapeDtypeStruct((m, n), BF16),
                   jax.ShapeDtypeStruct((rows, n_tail), F32)],
        scratch_shapes=[pltpu.VMEM((bm, k), BF16), pltpu.VMEM((bm, k), BF16)],
        compiler_params=_cparams(("arbitrary",)),
        name="mm_norm",
    )(x2d, g, sc, sh, w, c_pad, w_ada, b_ada)


def _bias_head(rpb_ref, o_ref, slot, h):
    n_rows = 2 * NA_WIN_ROWS - 1
    n_cols = 2 * NA_WIN_COLS - 1
    shape = (GRID_W, 2 * GRID_W)
    lane = lax.broadcasted_iota(jnp.int32, shape, 1)
    cq = lax.broadcasted_iota(jnp.int32, shape, 0)
    ck = lane & (GRID_W - 1)
    first = lane < GRID_W
    cs = jnp.clip(cq - NA_WIN_COLS // 2, 0, GRID_W - NA_WIN_COLS)
    valid = (ck >= cs) & (ck < cs + NA_WIN_COLS)
    d = jnp.clip(ck - cq, -(NA_WIN_COLS - 1), NA_WIN_COLS - 1) + (NA_WIN_COLS - 1)
    pair = []
    for j in range(n_rows - 1):
        base0 = (h * n_rows + j) * n_cols
        base1 = base0 + n_cols
        acc = jnp.zeros(shape, F32)
        for dd in range(n_cols):
            val = jnp.where(first, rpb_ref[base0 + dd], rpb_ref[base1 + dd])
            acc = jnp.where(d == dd, val, acc)
        pair.append(jnp.where(valid, acc * LOG2_E, NEG_BIG))
    for w in range(NA_WIN_ROWS):
        for ip in range(NA_WIN_ROWS // 2):
            o_ref[slot, w, :, ip * 2 * GRID_W:(ip + 1) * 2 * GRID_W] = pair[w + 2 * ip]


NA_ROWS_PER_STEP = 8


def _na_row_start(r, rows):
    return jnp.clip(r - NA_WIN_ROWS // 2, 0, rows - NA_WIN_ROWS)


def _na_window_start(r0, rows):
    span = NA_WIN_ROWS + NA_ROWS_PER_STEP - 1
    return jnp.clip(r0 - NA_WIN_ROWS // 2, 0, rows - span)


def _cast_block(shape, n_steps):
    r, c = shape
    for f in (1, 2, 4, 8):
        row_blocks = n_steps // f
        if (n_steps % f == 0 and r % (2 * V7X_SUBLANES * row_blocks) == 0
                and c % (V7X_LANES * f) == 0):
            return (r // row_blocks, c // f), (lambda s, f=f: (s // f, s % f))
    raise ValueError(f"cannot walk {shape} in {n_steps} blocks")


def _na_kernel(q_ref, k_ref, v_ref, bias_ref, *rest, rows, n_cast):
    casts_in, o_ref, casts_out = rest[:n_cast], rest[n_cast], rest[n_cast + 1:]
    for src, dst in zip(casts_in, casts_out):
        dst[...] = src[...].astype(dst.dtype)
    _na_body(q_ref, k_ref, v_ref, bias_ref, o_ref, rows=rows)


def _na_body(q_ref, k_ref, v_ref, bias_ref, o_ref, *, rows):
    n_keys = NA_WIN_ROWS * GRID_W
    gw = NA_GROUP * NA_HEAD_DIM
    lane_head = lax.broadcasted_iota(jnp.int32, (GRID_W, gw), 1) // NA_HEAD_DIM
    r0 = pl.program_id(1) * NA_ROWS_PER_STEP
    win0 = _na_window_start(r0, rows)
    for j in range(NA_ROWS_PER_STEP):
        r = r0 + j
        rs = _na_row_start(r, rows)
        ks = pl.ds(pl.multiple_of((rs - win0) * GRID_W, GRID_W), n_keys)
        w = rs - r + (NA_WIN_ROWS - 1)
        qs = slice(j * GRID_W, (j + 1) * GRID_W)
        for g in range(NA_HEADS // NA_GROUP):
            cs = slice(g * gw, (g + 1) * gw)
            qg = q_ref[qs, cs]
            kg = k_ref[ks, cs]
            vg = v_ref[ks, cs]
            zero = jnp.zeros_like(qg)
            q4 = jnp.concatenate(
                [jnp.where(lane_head == h, qg, zero) for h in range(NA_GROUP)], axis=0)
            s = lax.dot_general(q4, kg, (((1,), (1,)), ((), ())), preferred_element_type=F32)
            bias = bias_ref[NA_GROUP * g:NA_GROUP * (g + 1), w]
            s = s + bias.reshape(NA_GROUP * GRID_W, n_keys)
            m = jnp.max(s, axis=-1, keepdims=True)
            p = jnp.exp2(s - m)
            l = jnp.sum(p, axis=-1, keepdims=True)
            o4 = jnp.dot(p.astype(BF16), vg, preferred_element_type=F32) / l
            o = o4[0:GRID_W]
            for h in range(1, NA_GROUP):
                o = jnp.where(lane_head == h, o4[h * GRID_W:(h + 1) * GRID_W], o)
            o_ref[qs, cs] = o.astype(o_ref.dtype)


def _na_attention(proj3, bias_tab, weights_f32):
    b, l, _ = proj3.shape
    rows = l // GRID_W
    rb = NA_ROWS_PER_STEP
    span = NA_WIN_ROWS + rb - 1
    steps_per_batch = rows // rb
    n_steps = b * steps_per_batch
    step = lambda bi, t: bi * steps_per_batch + t

    def kv_spec(col0):
        return pl.BlockSpec((None, pl.Element(span * GRID_W), pl.Element(D_NA)),
                            lambda bi, t: (bi, _na_window_start(t * rb, rows) * GRID_W, col0))

    cast_specs = []
    for w in weights_f32:
        blk, walk = _cast_block(w.shape, n_steps)
        cast_specs.append(pl.BlockSpec(blk, lambda bi, t, walk=walk: walk(step(bi, t))))

    outs = pl.pallas_call(
        functools.partial(_na_kernel, rows=rows, n_cast=len(weights_f32)),
        grid=(b, steps_per_batch),
        in_specs=[
            pl.BlockSpec((None, rb * GRID_W, D_NA), lambda bi, t: (bi, t, 0)),
            kv_spec(D_NA),
            kv_spec(2 * D_NA),
            pl.BlockSpec(bias_tab.shape, lambda bi, t: (0, 0, 0, 0),
                         pipeline_mode=pl.Buffered(1)),
        ] + cast_specs,
        out_specs=[pl.BlockSpec((None, rb * GRID_W, D_NA), lambda bi, t: (bi, t, 0))]
        + cast_specs,
        out_shape=[jax.ShapeDtypeStruct((b, l, D_NA), BF16)]
        + [jax.ShapeDtypeStruct(w.shape, BF16) for w in weights_f32],
        compiler_params=_cparams(("arbitrary", "arbitrary")),
        name="na_attn",
    )(proj3, proj3, proj3, bias_tab, *weights_f32)
    return outs[0], outs[1:]


HY_ROWS = 256


def _shift_matrix(n):
    s = np.zeros((2 * n, n), np.float32)
    s[np.arange(1, n), np.arange(0, n - 1)] = 1.0
    s[n + np.arange(0, n - 1), np.arange(1, n)] = 1.0
    return s


def _hy_pre_kernel(u0_ref, u1_ref, u2_ref, w0_ref, w1_ref, w2_ref,
                   b0_ref, b1_ref, b2_ref, sh_ref, x0_ref, vx_ref):
    n, c = u0_ref.shape
    t8 = V7X_SUBLANES
    halo = 2 * t8
    row8 = lax.broadcasted_iota(jnp.int32, (t8, c), 0)
    shift = sh_ref[...]

    def conv_block(u_ref, w_ref, b_ref, r0):
        ub = u_ref[r0:r0 + HY_ROWS, :]
        pn = jnp.dot(shift, ub, preferred_element_type=F32)
        prev, nxt = pn[0:HY_ROWS], pn[HY_ROWS:]
        if r0 > 0:
            up = u_ref[r0 - halo:r0, :].astype(F32)[t8:]
            first = jnp.where(row8 == 0, pltpu.roll(up, 1, axis=0), prev[0:t8])
            prev = jnp.concatenate([first, prev[t8:]], axis=0)
        if r0 + HY_ROWS < n:
            dn = u_ref[r0 + HY_ROWS:r0 + HY_ROWS + halo, :].astype(F32)[0:t8]
            last = jnp.where(row8 == t8 - 1, pltpu.roll(dn, t8 - 1, axis=0),
                             nxt[HY_ROWS - t8:])
            nxt = jnp.concatenate([nxt[0:HY_ROWS - t8], last], axis=0)
        w = w_ref[...]
        return prev * w[0:1] + ub.astype(F32) * w[1:2] + nxt * w[2:3] + b_ref[...]

    for r0 in range(0, n, HY_ROWS):
        rs = slice(r0, r0 + HY_ROWS)
        x0 = conv_block(u0_ref, w0_ref, b0_ref, r0)
        x1 = conv_block(u1_ref, w1_ref, b1_ref, r0)
        v = conv_block(u2_ref, w2_ref, b2_ref, r0)
        x0_ref[rs, :] = x0.astype(x0_ref.dtype)
        vx_ref[rs, :] = (v * x1).astype(vx_ref.dtype)


def _hy_pre(proj3, short_w, short_b, d_hy, cb=256):
    b, l, _ = proj3.shape
    base = 3 * D_NA // cb
    per = d_hy // cb
    shift = jnp.asarray(_shift_matrix(HY_ROWS)).astype(BF16)

    def u_spec(g):
        return pl.BlockSpec((None, l, cb), lambda bi, j: (bi, 0, base + g * per + j))

    def w_spec(g):
        return pl.BlockSpec((3, cb), lambda bi, j: (0, g * per + j))

    def b_spec(g):
        return pl.BlockSpec((1, cb), lambda bi, j: (0, g * per + j))

    out_spec = pl.BlockSpec((None, l, cb), lambda bi, j: (bi, 0, j))
    return pl.pallas_call(
        _hy_pre_kernel,
        grid=(b, per),
        in_specs=[u_spec(0), u_spec(1), u_spec(2), w_spec(0), w_spec(1), w_spec(2),
                  b_spec(0), b_spec(1), b_spec(2),
                  pl.BlockSpec(shift.shape, lambda bi, j: (0, 0))],
        out_specs=[out_spec, out_spec],
        out_shape=[jax.ShapeDtypeStruct((b, l, d_hy), BF16),
                   jax.ShapeDtypeStruct((b, l, d_hy), BF16)],
        compiler_params=_cparams(("parallel", "parallel")),
        name="hy_pre",
    )(proj3, proj3, proj3, short_w, short_w, short_w, short_b, short_b, short_b, shift)


def _filter_kernel(z_ref, w1_ref, b1_ref, w2_ref, b2_ref, w3_ref, b3_ref,
                   fq_ref, w4f_ref, w4b_ref, dl_ref, db_ref, o_ref, h_ref):
    hi = lax.Precision.HIGHEST
    half, kp2 = z_ref.shape
    seq = 2 * half
    kp = kp2 // 2
    order = h_ref.shape[1]

    @pl.when(pl.program_id(0) == 0)
    def _():
        fq = fq_ref[...]

        def mlp(z):
            h = jnp.sin(fq * (jnp.dot(z, w1_ref[...], precision=hi,
                                      preferred_element_type=F32) + b1_ref[...]))
            h = jnp.sin(fq * (jnp.dot(h, w2_ref[...], precision=hi,
                                      preferred_element_type=F32) + b2_ref[...]))
            h = jnp.sin(fq * (jnp.dot(h, w3_ref[...], precision=hi,
                                      preferred_element_type=F32) + b3_ref[...]))
            return h

        def chunk(rs):
            h = mlp(z_ref[rs, :]).astype(h_ref.dtype)
            h_ref[rs, :] = h[:, 0:order]
            h_ref[pl.ds(half + rs.start, rs.size), :] = h[:, order:]

        _for_row_chunks(half, 4 * ROW_CHUNK, chunk)

    t_pos = jnp.concatenate([z_ref[:, 0:1], z_ref[:, kp:kp + 1]], axis=0)
    decay = jnp.exp(-t_pos * dl_ref[...])
    h3 = h_ref[...]
    fwd = jnp.dot(h3, w4f_ref[...].astype(BF16), preferred_element_type=F32) * decay
    bwd = jnp.dot(h3, w4b_ref[...].astype(BF16), preferred_element_type=F32) * decay
    row = lax.broadcasted_iota(jnp.int32, bwd.shape, 0)
    bwd = jnp.where(row == 0, 0.0, bwd)
    fwd = jnp.where(row == 0, fwd + db_ref[...], fwd)
    o_ref[0:seq, :] = fwd.astype(o_ref.dtype)
    o_ref[seq:2 * seq, :] = bwd.astype(o_ref.dtype)


def _block_diag2(w):
    z = jnp.zeros_like(w)
    return jnp.concatenate([jnp.concatenate([w, z], axis=1),
                            jnp.concatenate([z, w], axis=1)], axis=0)


def _filter_taps(z_np, w1p, b1, w2, b2, w3, b3, freq, w4, deltas_abs, d_bias, d_hy, cb=256):
    seq, kpad = z_np.shape
    order = w2.shape[0]
    per = d_hy // cb
    z = jnp.asarray(np.concatenate([z_np[:seq // 2], z_np[seq // 2:]], axis=1))
    w1p, w2, w3 = _block_diag2(w1p), _block_diag2(w2), _block_diag2(w3)
    b1, b2, b3, freq = [jnp.concatenate([a, a], axis=1) for a in (b1, b2, b3, freq)]
    full = lambda shape: pl.BlockSpec(shape, lambda j: tuple(0 for _ in shape))
    return pl.pallas_call(
        _filter_kernel,
        grid=(per,),
        in_specs=[full(z.shape),
                  full(w1p.shape), full(b1.shape),
                  full(w2.shape), full(b2.shape),
                  full(w3.shape), full(b3.shape),
                  full(freq.shape),
                  pl.BlockSpec((order, cb), lambda j: (0, j)),
                  pl.BlockSpec((order, cb), lambda j: (0, per + j)),
                  pl.BlockSpec((1, cb), lambda j: (0, j)),
                  pl.BlockSpec((1, cb), lambda j: (0, j))],
        out_specs=pl.BlockSpec((2 * seq, cb), lambda j: (0, j)),
        out_shape=jax.ShapeDtypeStruct((2 * seq, d_hy), BF16),
        scratch_shapes=[pltpu.VMEM((seq, order), BF16)],
        compiler_params=_cparams(("arbitrary",)),
        name="hy_filter",
    )(z, w1p, b1, w2, b2, w3, b3, freq, w4, w4, deltas_abs, d_bias)


V7X_SUBLANES = 8


def _fft_slow_kernel(wk_ref, x_ref, o_ref):
    k, rb, c = x_ref.shape
    m = o_ref.shape[0]
    x = x_ref[...].astype(F32)
    outs = []
    for t in range(rb // V7X_SUBLANES):
        xs = x[:, t * V7X_SUBLANES:(t + 1) * V7X_SUBLANES, :]
        xs = xs.reshape(k * V7X_SUBLANES, c).astype(BF16)
        o = jnp.dot(wk_ref[...], xs, preferred_element_type=F32)
        outs.append(o.reshape(m, V7X_SUBLANES, c))
    o_ref[...] = jnp.concatenate(outs, axis=1).astype(o_ref.dtype)


def _fft_slow_gate_kernel(wk_ref, x_ref, x0_ref, beta_ref, o_ref):
    k, rb, c = x_ref.shape
    m = o_ref.shape[0]
    x = x_ref[...].astype(F32)
    outs = []
    for t in range(rb // V7X_SUBLANES):
        xs = x[:, t * V7X_SUBLANES:(t + 1) * V7X_SUBLANES, :]
        xs = xs.reshape(k * V7X_SUBLANES, c).astype(BF16)
        o = jnp.dot(wk_ref[...], xs, preferred_element_type=F32)
        outs.append(o.reshape(m, V7X_SUBLANES, c))
    hy = jnp.concatenate(outs, axis=1) * x0_ref[...].astype(F32)
    ms = jnp.mean(hy * hy, axis=-1, keepdims=True)
    o_ref[...] = (hy * lax.rsqrt(ms + EPS) * beta_ref[...]).astype(o_ref.dtype)


def _fft_slow_gate(wk, x4, x04, beta, rb=16):
    g, k, r, c = x4.shape
    m = wk.shape[0] // V7X_SUBLANES
    assert wk.shape[1] == k * V7X_SUBLANES and x04.shape == (g, m, r, c)
    return pl.pallas_call(
        _fft_slow_gate_kernel,
        grid=(g, r // rb),
        in_specs=[pl.BlockSpec(wk.shape, lambda gi, j: (0, 0)),
                  pl.BlockSpec((None, k, rb, c), lambda gi, j: (gi, 0, j, 0)),
                  pl.BlockSpec((None, m, rb, c), lambda gi, j: (gi, 0, j, 0)),
                  pl.BlockSpec((1, c), lambda gi, j: (0, 0))],
        out_specs=pl.BlockSpec((None, m, rb, c), lambda gi, j: (gi, 0, j, 0)),
        out_shape=jax.ShapeDtypeStruct((g, m, r, c), BF16),
        compiler_params=_cparams(("parallel", "parallel")),
        name="fft_slow_gate",
    )(wk, x4, x04, beta)


def _fft_slow(wk, x4, out_dtype, rb=16):
    g, k, r, c = x4.shape
    m = wk.shape[0] // V7X_SUBLANES
    assert wk.shape[1] == k * V7X_SUBLANES
    return pl.pallas_call(
        _fft_slow_kernel,
        grid=(g, r // rb),
        in_specs=[pl.BlockSpec(wk.shape, lambda gi, j: (0, 0)),
                  pl.BlockSpec((None, k, rb, c), lambda gi, j: (gi, 0, j, 0))],
        out_specs=pl.BlockSpec((None, m, rb, c), lambda gi, j: (gi, 0, j, 0)),
        out_shape=jax.ShapeDtypeStruct((g, m, r, c), out_dtype),
        compiler_params=_cparams(("parallel", "parallel")),
        name="fft_slow",
    )(wk, x4)


SPEC_SLABS = 4


def _fft_spec_kernel(e_ref, ei_ref, af_ref, a_ref, o_ref):
    pairs, two, slabs, r, c = a_ref.shape
    for q in range(slabs):
        e = e_ref[q]
        hf = jnp.dot(e, af_ref[0, :, q].reshape(two * r, c), preferred_element_type=F32)
        hb = jnp.dot(e, af_ref[1, :, q].reshape(two * r, c), preferred_element_type=F32)
        kr = hf[0:r] + hb[0:r]
        ki = hf[r:2 * r] - hb[r:2 * r]
        for p in range(pairs):
            b2 = jnp.dot(e, a_ref[p, :, q].reshape(two * r, c), preferred_element_type=F32)
            br, bi = b2[0:r], b2[r:2 * r]
            y2 = jnp.concatenate([br * kr - bi * ki, br * ki + bi * kr], axis=0).astype(BF16)
            g2 = jnp.dot(ei_ref[q], y2, preferred_element_type=F32)
            o_ref[p, :, q] = g2.reshape(two, r, c).astype(o_ref.dtype)


def _fft_spec(e_fwd, e_inv, a_filt, a_data):
    pairs, two, s, r, c = a_data.shape
    q = SPEC_SLABS
    return pl.pallas_call(
        _fft_spec_kernel,
        grid=(s // q,),
        in_specs=[pl.BlockSpec((q, two * r, two * r), lambda i: (i, 0, 0)),
                  pl.BlockSpec((q, two * r, two * r), lambda i: (i, 0, 0)),
                  pl.BlockSpec((2, two, q, r, c), lambda i: (0, 0, i, 0, 0)),
                  pl.BlockSpec((pairs, two, q, r, c), lambda i: (0, 0, i, 0, 0))],
        out_specs=pl.BlockSpec((pairs, two, q, r, c), lambda i: (0, 0, i, 0, 0)),
        out_shape=jax.ShapeDtypeStruct((pairs, two, s, r, c), BF16),
        compiler_params=_cparams(("parallel",)),
        name="fft_spec",
    )(e_fwd, e_inv, a_filt, a_data)


def _fft_constants(seq_len):
    n = 2 * seq_len
    r_, s_ = FFT_R, FFT_S
    assert r_ * s_ == n
    half = s_ // 2
    ks = np.arange(s_)[:, None]
    s = np.arange(s_)[None, :]
    ang = 2.0 * np.pi * ((ks * s) % s_) / s_
    fr, fi = np.cos(ang), -np.sin(ang)
    w1_data = np.block([[fr[:, :half], -fi[:, :half]], [fi[:, :half], fr[:, :half]]])
    w1_filt = np.block([[fr[:, :half]], [fi[:, :half]]])
    so = np.arange(half)[:, None]
    ko = np.arange(s_)[None, :]
    ang_i = 2.0 * np.pi * ((so * ko) % s_) / s_
    cr, ci = np.cos(ang_i), np.sin(ang_i)
    w3 = np.block([[cr, -ci], [ci, cr]]) / n
    ksv = np.arange(s_)[:, None, None]
    kr = np.arange(r_)[None, :, None]
    r = np.arange(r_)[None, None, :]
    ang_e = 2.0 * np.pi * ((r * (ksv + s_ * kr)) % n) / n
    er, ei = np.cos(ang_e), -np.sin(ang_e)
    e_fwd = np.concatenate([np.concatenate([er, -ei], axis=2),
                            np.concatenate([ei, er], axis=2)], axis=1)
    e_inv = np.transpose(e_fwd, (0, 2, 1))
    as_bf16 = lambda a: jnp.asarray(a.astype(np.float32)).astype(BF16)
    eye = np.eye(V7X_SUBLANES)
    slow = [as_bf16(np.kron(w, eye)) for w in (w1_data, w1_filt, w3)]
    return slow[0], slow[1], slow[2], as_bf16(e_fwd), as_bf16(e_inv)


def _mm_out_kernel(na_ref, hyn_ref, bna_ref, w_ref, x_ref, gt_ref, o_ref, ha_ref, hb_ref):
    bm, d_na = na_ref.shape

    def step(h_new, h_cur):
        for r0 in range(0, bm, ROW_CHUNK):
            rs = slice(r0, r0 + ROW_CHUNK)
            na = na_ref[rs, :].astype(F32)
            na_n = na * lax.rsqrt(jnp.mean(na * na, axis=-1, keepdims=True) + EPS) * bna_ref[...]
            h_new[rs, 0:d_na] = na_n.astype(BF16)
            h_new[rs, d_na:] = hyn_ref[rs, :]
        acc = jnp.dot(h_cur[...], w_ref[...], preferred_element_type=F32)
        o_ref[...] = x_ref[...] + gt_ref[0] * acc

    _skewed_steps(ha_ref, hb_ref, step)


def _mm_out(na2d, hyn2d, beta_na, w, x2d, gt, seq_len, bm=256):
    m, d_na = na2d.shape
    d_hy = hyn2d.shape[1]
    k, n = w.shape
    nb = m // bm
    blocks_per_seq = seq_len // bm
    new = lambda t: jnp.minimum(t, nb - 1)
    cur = lambda t: jnp.maximum(t - 1, 0)
    return pl.pallas_call(
        _mm_out_kernel,
        grid=(nb + 1,),
        in_specs=[pl.BlockSpec((bm, d_na), lambda t: (new(t), 0)),
                  pl.BlockSpec((bm, d_hy), lambda t: (new(t), 0)),
                  _resident((1, d_na)),
                  _resident((k, n)),
                  pl.BlockSpec((bm, n), lambda t: (cur(t), 0)),
                  pl.BlockSpec((1, 1, n), lambda t: (cur(t) // blocks_per_seq, 0, 0))],
        out_specs=pl.BlockSpec((bm, n), lambda t: (cur(t), 0)),
        out_shape=jax.ShapeDtypeStruct((m, n), F32),
        scratch_shapes=[pltpu.VMEM((bm, k), BF16), pltpu.VMEM((bm, k), BF16)],
        compiler_params=_cparams(("arbitrary",)),
        name="mm_out",
    )(na2d, hyn2d, beta_na, w, x2d, gt)


HALO = 16


def _mm_up_glu_kernel(x_ref, xp_ref, xn_ref, g_ref, sc_ref, sh_ref, wa_ref, wb_ref,
                      cw_ref, cb_ref, o_ref, ha_ref, hb_ref, *, blocks_per_seq, n_blocks):
    bm, k = x_ref.shape
    t = pl.program_id(0)
    j = pl.program_id(1)
    pos = jnp.minimum(t, n_blocks - 1) % blocks_per_seq

    def norm(x):
        gs = g_ref[...] * (1.0 + sc_ref[0])
        ms = jnp.mean(x * x, axis=-1, keepdims=True)
        return x * lax.rsqrt(ms + EPS) * gs + sh_ref[0]

    def build(h_new):
        zero = jnp.zeros((HALO, k), F32)
        h_new[0:HALO, :] = jnp.where(pos == 0, zero, norm(xp_ref[...])).astype(BF16)
        for r0 in range(0, bm, ROW_CHUNK):
            h_new[HALO + r0:HALO + r0 + ROW_CHUNK, :] = norm(
                x_ref[r0:r0 + ROW_CHUNK, :]).astype(BF16)
        h_new[HALO + bm:, :] = jnp.where(pos == blocks_per_seq - 1, zero,
                                         norm(xn_ref[...])).astype(BF16)

    def multiply(h_cur):
        a = jnp.dot(h_cur[...], wa_ref[...], preferred_element_type=F32)
        b = jnp.dot(h_cur[HALO:HALO + bm, :], wb_ref[...], preferred_element_type=F32)
        n_ext = a.shape[0]
        w = cw_ref[...]
        prev = pltpu.roll(a, 1, axis=0)[HALO:HALO + bm]
        nxt = pltpu.roll(a, n_ext - 1, axis=0)[HALO:HALO + bm]
        ac = prev * w[0:1] + a[HALO:HALO + bm] * w[1:2] + nxt * w[2:3] + cb_ref[...]
        gelu = 0.5 * ac * (1.0 + lax.erf(ac * (1.0 / math.sqrt(2.0))))
        o_ref[...] = (gelu * b).astype(o_ref.dtype)

    even = t % 2 == 0

    @pl.when((t == 0) & (j == 0))
    def _():
        build(ha_ref)

    @pl.when((t > 0) & even & (j == 0))
    def _():
        build(ha_ref)
        multiply(hb_ref)

    @pl.when((t > 0) & even & (j > 0))
    def _():
        multiply(hb_ref)

    @pl.when(jnp.logical_not(even) & (j == 0))
    def _():
        build(hb_ref)
        multiply(ha_ref)

    @pl.when(jnp.logical_not(even) & (j > 0))
    def _():
        multiply(ha_ref)


def _mm_up_glu(x2d, g, sc, sh, w_up, conv_w, conv_b, seq_len, bm=1024, bn=512):
    m, k = x2d.shape
    d_ff = w_up.shape[1] // 2
    nbn = d_ff // bn
    nb = m // bm
    blocks_per_seq = seq_len // bm
    hb = bm // HALO
    last = m // HALO - 1
    new = lambda t: jnp.minimum(t, nb - 1)
    cur = lambda t: jnp.maximum(t - 1, 0)
    col = lambda t, j: jnp.where(t == 0, 0, j)
    return pl.pallas_call(
        functools.partial(_mm_up_glu_kernel, blocks_per_seq=blocks_per_seq, n_blocks=nb),
        grid=(nb + 1, nbn),
        in_specs=[pl.BlockSpec((bm, k), lambda t, j: (new(t), 0)),
                  pl.BlockSpec((HALO, k), lambda t, j: (jnp.maximum(new(t) * hb - 1, 0), 0)),
                  pl.BlockSpec((HALO, k),
                               lambda t, j: (jnp.minimum((new(t) + 1) * hb, last), 0)),
                  pl.BlockSpec((1, k), lambda t, j: (0, 0)),
                  pl.BlockSpec((1, 1, k), lambda t, j: (new(t) // blocks_per_seq, 0, 0)),
                  pl.BlockSpec((1, 1, k), lambda t, j: (new(t) // blocks_per_seq, 0, 0)),
                  pl.BlockSpec((k, bn), lambda t, j: (0, col(t, j))),
                  pl.BlockSpec((k, bn), lambda t, j: (0, nbn + col(t, j))),
                  pl.BlockSpec((3, bn), lambda t, j: (0, col(t, j))),
                  pl.BlockSpec((1, bn), lambda t, j: (0, col(t, j)))],
        out_specs=pl.BlockSpec((bm, bn), lambda t, j: (cur(t), col(t, j))),
        out_shape=jax.ShapeDtypeStruct((m, d_ff), BF16),
        scratch_shapes=[pltpu.VMEM((bm + 2 * HALO, k), BF16),
                        pltpu.VMEM((bm + 2 * HALO, k), BF16)],
        compiler_params=_cparams(("arbitrary", "arbitrary")),
        name="mm_up_glu",
    )(x2d, x2d, x2d, g, sc, sh, w_up, w_up, conv_w, conv_b)


def _mm_down_kernel(g_ref, w_ref, x_ref, gt_ref, gf_ref, o_ref):
    acc = jnp.dot(g_ref[...], w_ref[...], preferred_element_type=F32)
    x = x_ref[...] + gt_ref[0] * acc
    ms = jnp.mean(x * x, axis=-1, keepdims=True)
    o_ref[...] = x * lax.rsqrt(ms + EPS) * gf_ref[...]


def _mm_down(g2d, w, x2d, gt, g_final, seq_len, bm=256):
    m, k = g2d.shape
    n = w.shape[1]
    blocks_per_seq = seq_len // bm
    return pl.pallas_call(
        _mm_down_kernel,
        grid=(m // bm,),
        in_specs=[pl.BlockSpec((bm, k), lambda i: (i, 0)),
                  _resident((k, n)),
                  pl.BlockSpec((bm, n), lambda i: (i, 0)),
                  pl.BlockSpec((1, 1, n), lambda i: (i // blocks_per_seq, 0, 0)),
                  _resident((1, n))],
        out_specs=pl.BlockSpec((bm, n), lambda i: (i, 0)),
        out_shape=jax.ShapeDtypeStruct((m, n), F32),
        compiler_params=_cparams(("parallel",)),
        name="mm_down",
    )(g2d, w, x2d, gt, g_final)


def _position_features(seq_len, kpad):
    t = np.linspace(0.0, 1.0, seq_len)[:, None]
    bands = (FILTER_EMB - 1) // 2
    w = 2.0 * np.pi * np.arange(seq_len)[:, None] / seq_len
    fr = np.linspace(1e-4, bands - 1, bands)[None, :]
    z = np.concatenate([t, np.cos(fr * w), -np.sin(fr * w)], axis=-1)
    z = np.pad(z, ((0, 0), (0, kpad - z.shape[1])))
    return z.astype(np.float32)


def kernel(x, c, w_ada, b_ada, g_mix, w_in, na_rpb, hy_short_w, hy_short_b,
           hy_filt_w1, hy_filt_b1, hy_filt_w2, hy_filt_b2, hy_filt_w3, hy_filt_b3,
           hy_filt_w4, hy_filt_freq, hy_bias, beta_na, beta_hy, w_out, g_ffn,
           w_up, ffn_conv_w, ffn_conv_b, w_down, g_final):
    b, l, d = x.shape
    depth = w_ada.shape[0]
    d_hy = d - D_NA
    d_ff = w_down.shape[1]
    rows = l // GRID_W
    m = b * l
    assert depth == 1 and b % 2 == 0 and 2 * l == FFT_R * FFT_S

    c_pad = jnp.pad(c, ((0, V7X_SUBLANES - b), (0, 0)))
    b_ada2 = b_ada[0][None, :]
    col_scale = np.ones((1, w_in.shape[2]), np.float32)
    col_scale[:, :D_NA] = NA_Q_SCALE
    mod_head, bias_tab, w_in_bf = _adaln_bias(c_pad, w_ada[0], b_ada2, na_rpb[0].reshape(-1),
                                              w_in[0], jnp.asarray(col_scale), 2 * d)
    sh1, sc1 = [t[:b, None, :] for t in jnp.split(mod_head, 2, axis=-1)]

    x2d = x.reshape(m, d)

    proj, mod_tail = _mm_norm(x2d, g_mix, sc1, sh1, w_in_bf, c_pad, w_ada[0], b_ada2,
                              2 * d, l)
    gt1, sh2, sc2, gt2 = [t[:b, None, :] for t in jnp.split(mod_tail, 4, axis=-1)]

    na, (w_out_bf, w_up_bf, w_down_bf) = _na_attention(
        proj.reshape(b, l, proj.shape[1]), bias_tab, (w_out[0], w_up[0], w_down[0]))
    na2d = na.reshape(m, D_NA)

    x0, vx = _hy_pre(proj.reshape(b, l, proj.shape[1]), hy_short_w[0], hy_short_b[0][None, :], d_hy)

    kpad = V7X_LANES
    z_np = _position_features(l, kpad)
    w1p = jnp.pad(hy_filt_w1[0], ((0, kpad - FILTER_EMB), (0, 0)))
    max_decay = math.log(DECAY_TARGET) / FAST_DECAY_PCT
    min_decay = math.log(DECAY_TARGET) / SLOW_DECAY_PCT
    deltas_abs = np.abs(np.linspace(min_decay, max_decay, d_hy))[None, :].astype(np.float32)
    taps = _filter_taps(z_np, w1p, hy_filt_b1,
                        hy_filt_w2[0], hy_filt_b2, hy_filt_w3[0], hy_filt_b3,
                        hy_filt_freq, hy_filt_w4[0], jnp.asarray(deltas_abs), hy_bias, d_hy)

    w1_data, w1_filt, w3, e_fwd, e_inv = _fft_constants(l)
    a_filt = _fft_slow(w1_filt, taps.reshape(2, FFT_S // 2, FFT_R, d_hy), BF16)
    a_data = _fft_slow(w1_data, vx.reshape(b // 2, FFT_S, FFT_R, d_hy), BF16)
    g_spec = _fft_spec(e_fwd, e_inv, a_filt.reshape(2, 2, FFT_S, FFT_R, d_hy),
                       a_data.reshape(b // 2, 2, FFT_S, FFT_R, d_hy))
    hyn = _fft_slow_gate(w3, g_spec.reshape(b // 2, 2 * FFT_S, FFT_R, d_hy),
                         x0.reshape(b // 2, FFT_S, FFT_R, d_hy), beta_hy)

    x1 = _mm_out(na2d, hyn.reshape(m, d_hy), beta_na, w_out_bf, x2d, gt1, l)

    gl = _mm_up_glu(x1, g_ffn, sc2, sh2, w_up_bf, ffn_conv_w[0],
                    ffn_conv_b[0][None, :], l)
    out = _mm_down(gl, w_down_bf, x1, gt2, g_final[None, :], l)
    return out.reshape(b, l, d)
```

```python
import functools
import math

import numpy as np
import jax
import jax.numpy as jnp
from jax import lax
from jax.experimental import pallas as pl
from jax.experimental.pallas import tpu as pltpu

F32 = jnp.float32
BF16 = jnp.bfloat16

GRID_W = 64
NA_HEADS = 16
NA_HEAD_DIM = 64
D_NA = NA_HEADS * NA_HEAD_DIM
NA_WIN_ROWS = 8
NA_WIN_COLS = 16
NA_GROUP = 4
FILTER_EMB = 33
DECAY_TARGET = 1e-2
FAST_DECAY_PCT = 0.3
SLOW_DECAY_PCT = 1.5
EPS = 1e-6
NEG_BIG = -1e30
LOG2_E = math.log2(math.e)
NA_Q_SCALE = NA_HEAD_DIM ** -0.5 * LOG2_E

V7X_LANES = 128
V7X_VMEM_BYTES = 64 * 1024 * 1024
VMEM_LIMIT = 56 * 1024 * 1024

FFT_R = 128
FFT_S = 64

ROW_CHUNK = 64


def _cparams(sem, vmem=VMEM_LIMIT):
    return pltpu.CompilerParams(dimension_semantics=sem, vmem_limit_bytes=vmem)


def _for_row_chunks(n_rows, chunk, fn):
    def body(i, carry):
        fn(pl.ds(pl.multiple_of(i * chunk, chunk), chunk))
        return carry
    lax.fori_loop(0, n_rows // chunk, body, 0)


def _adaln_block(c_ref, w_ref, b_ref):
    c = c_ref[...]
    cond = c / (1.0 + jnp.exp(-c))
    return jnp.dot(cond.astype(BF16), w_ref[...].astype(BF16),
                   preferred_element_type=F32) + b_ref[...]


def _adaln_bias_kernel(c_ref, w_ref, b_ref, rpb_ref, win_ref, cs_ref, o_ref, bias_ref,
                       winb_ref):
    o_ref[...] = _adaln_block(c_ref, w_ref, b_ref)
    winb_ref[...] = (win_ref[...] * cs_ref[...]).astype(winb_ref.dtype)
    heads = bias_ref.shape[0]
    for hh in range(heads):
        _bias_head(rpb_ref, bias_ref, hh, pl.program_id(0) * heads + hh)


def _adaln_bias(c_pad, w_ada, b_ada, rpb_flat, w_in, col_scale, n_cols, bn=512):
    rows, d = c_pad.shape
    steps = n_cols // bn
    heads = NA_HEADS // steps
    kin, nin = w_in.shape
    rin = kin // steps
    assert heads * steps == NA_HEADS and rin * steps == kin and rin % (2 * V7X_SUBLANES) == 0
    n_keys = NA_WIN_ROWS * GRID_W
    return pl.pallas_call(
        _adaln_bias_kernel,
        grid=(steps,),
        in_specs=[pl.BlockSpec((rows, d), lambda j: (0, 0)),
                  pl.BlockSpec((d, bn), lambda j: (0, j)),
                  pl.BlockSpec((1, bn), lambda j: (0, j)),
                  pl.BlockSpec(memory_space=pltpu.SMEM),
                  pl.BlockSpec((rin, nin), lambda j: (j, 0)),
                  pl.BlockSpec((1, nin), lambda j: (0, 0))],
        out_specs=[pl.BlockSpec((rows, bn), lambda j: (0, j)),
                   pl.BlockSpec((heads, NA_WIN_ROWS, GRID_W, n_keys), lambda j: (j, 0, 0, 0)),
                   pl.BlockSpec((rin, nin), lambda j: (j, 0))],
        out_shape=[jax.ShapeDtypeStruct((rows, n_cols), F32),
                   jax.ShapeDtypeStruct((NA_HEADS, NA_WIN_ROWS, GRID_W, n_keys), F32),
                   jax.ShapeDtypeStruct((kin, nin), BF16)],
        compiler_params=_cparams(("parallel",)),
        name="adaln_bias",
    )(c_pad, w_ada, b_ada, rpb_flat, w_in, col_scale)


def _skewed_steps(ha_ref, hb_ref, step):
    t = pl.program_id(0)

    @pl.when(t == 0)
    def _():
        hb_ref[...] = jnp.zeros_like(hb_ref)

    @pl.when(t % 2 == 0)
    def _():
        step(ha_ref, hb_ref)

    @pl.when(t % 2 == 1)
    def _():
        step(hb_ref, ha_ref)


def _resident(shape):
    return pl.BlockSpec(shape, lambda t: tuple(0 for _ in shape), pipeline_mode=pl.Buffered(1))


def _mm_norm_kernel(x_ref, g_ref, sc_ref, sh_ref, w_ref, c_ref, wt_ref, bt_ref,
                    o_ref, mt_ref, ha_ref, hb_ref):
    bm = x_ref.shape[0]

    def step(h_new, h_cur):
        gs = g_ref[...] * (1.0 + sc_ref[0])
        sh = sh_ref[0]
        for r0 in range(0, bm, ROW_CHUNK):
            x = x_ref[r0:r0 + ROW_CHUNK, :]
            ms = jnp.mean(x * x, axis=-1, keepdims=True)
            h_new[r0:r0 + ROW_CHUNK, :] = (x * lax.rsqrt(ms + EPS) * gs + sh).astype(BF16)
        o_ref[...] = jnp.dot(h_cur[...], w_ref[...],
                             preferred_element_type=F32).astype(o_ref.dtype)
        mt_ref[...] = _adaln_block(c_ref, wt_ref, bt_ref)

    _skewed_steps(ha_ref, hb_ref, step)


def _mm_norm(x2d, g, sc, sh, w, c_pad, w_ada, b_ada, tail_col0, seq_len, bm=256):
    m, k = x2d.shape
    n = w.shape[1]
    nb = m // bm
    blocks_per_seq = seq_len // bm
    rows = c_pad.shape[0]
    n_tail = w_ada.shape[1] - tail_col0
    bt = n_tail // nb
    assert bt * nb == n_tail and bt % V7X_LANES == 0 and tail_col0 % bt == 0
    new = lambda t: jnp.minimum(t, nb - 1)
    cur = lambda t: jnp.maximum(t - 1, 0)
    tail = lambda t: (0, tail_col0 // bt + new(t))
    return pl.pallas_call(
        _mm_norm_kernel,
        grid=(nb + 1,),
        in_specs=[pl.BlockSpec((bm, k), lambda t: (new(t), 0)),
                  _resident((1, k)),
                  pl.BlockSpec((1, 1, k), lambda t: (new(t) // blocks_per_seq, 0, 0)),
                  pl.BlockSpec((1, 1, k), lambda t: (new(t) // blocks_per_seq, 0, 0)),
                  _resident((k, n)),
                  _resident(c_pad.shape),
                  pl.BlockSpec((k, bt), tail),
                  pl.BlockSpec((1, bt), tail)],
        out_specs=[pl.BlockSpec((bm, n), lambda t: (cur(t), 0)),
                   pl.BlockSpec((rows, bt), lambda t: (0, new(t)))],
        out_shape=[jax.ShapeDtypeStruct((m, n), BF16),
                   jax.ShapeDtypeStruct((rows, n_tail), F32)],
        scratch_shapes=[pltpu.VMEM((bm, k), BF16), pltpu.VMEM((bm, k), BF16)],
        compiler_params=_cparams(("arbitrary",)),
        name="mm_norm",
    )(x2d, g, sc, sh, w, c_pad, w_ada, b_ada)


def _bias_head(rpb_ref, o_ref, slot, h):
    n_rows = 2 * NA_WIN_ROWS - 1
    n_cols = 2 * NA_WIN_COLS - 1
    shape = (GRID_W, 2 * GRID_W)
    lane = lax.broadcasted_iota(jnp.int32, shape, 1)
    cq = lax.broadcasted_iota(jnp.int32, shape, 0)
    ck = lane & (GRID_W - 1)
    first = lane < GRID_W
    cs = jnp.clip(cq - NA_WIN_COLS // 2, 0, GRID_W - NA_WIN_COLS)
    valid = (ck >= cs) & (ck < cs + NA_WIN_COLS)
    d = jnp.clip(ck - cq, -(NA_WIN_COLS - 1), NA_WIN_COLS - 1) + (NA_WIN_COLS - 1)
    pair = []
    for j in range(n_rows - 1):
        base0 = (h * n_rows + j) * n_cols
        base1 = base0 + n_cols
        acc = jnp.zeros(shape, F32)
        for dd in range(n_cols):
            val = jnp.where(first, rpb_ref[base0 + dd], rpb_ref[base1 + dd])
            acc = jnp.where(d == dd, val, acc)
        pair.append(jnp.where(valid, acc * LOG2_E, NEG_BIG))
    for w in range(NA_WIN_ROWS):
        for ip in range(NA_WIN_ROWS // 2):
            o_ref[slot, w, :, ip * 2 * GRID_W:(ip + 1) * 2 * GRID_W] = pair[w + 2 * ip]


NA_ROWS_PER_STEP = 8


def _na_row_start(r, rows):
    return jnp.clip(r - NA_WIN_ROWS // 2, 0, rows - NA_WIN_ROWS)


def _na_window_start(r0, rows):
    span = NA_WIN_ROWS + NA_ROWS_PER_STEP - 1
    return jnp.clip(r0 - NA_WIN_ROWS // 2, 0, rows - span)


def _cast_block(shape, n_steps):
    r, c = shape
    for f in (1, 2, 4, 8):
        row_blocks = n_steps // f
        if (n_steps % f == 0 and r % (2 * V7X_SUBLANES * row_blocks) == 0
                and c % (V7X_LANES * f) == 0):
            return (r // row_blocks, c // f), (lambda s, f=f: (s // f, s % f))
    raise ValueError(f"cannot walk {shape} in {n_steps} blocks")


def _na_kernel(q_ref, k_ref, v_ref, bias_ref, *rest, rows, n_cast):
    casts_in, o_ref, casts_out = rest[:n_cast], rest[n_cast], rest[n_cast + 1:]
    for src, dst in zip(casts_in, casts_out):
        dst[...] = src[...].astype(dst.dtype)
    _na_body(q_ref, k_ref, v_ref, bias_ref, o_ref, rows=rows)


def _na_body(q_ref, k_ref, v_ref, bias_ref, o_ref, *, rows):
    n_keys = NA_WIN_ROWS * GRID_W
    gw = NA_GROUP * NA_HEAD_DIM
    lane_head = lax.broadcasted_iota(jnp.int32, (GRID_W, gw), 1) // NA_HEAD_DIM
    r0 = pl.program_id(1) * NA_ROWS_PER_STEP
    win0 = _na_window_start(r0, rows)
    for j in range(NA_ROWS_PER_STEP):
        r = r0 + j
        rs = _na_row_start(r, rows)
        ks = pl.ds(pl.multiple_of((rs - win0) * GRID_W, GRID_W), n_keys)
        w = rs - r + (NA_WIN_ROWS - 1)
        qs = slice(j * GRID_W, (j + 1) * GRID_W)
        for g in range(NA_HEADS // NA_GROUP):
            cs = slice(g * gw, (g + 1) * gw)
            qg = q_ref[qs, cs]
            kg = k_ref[ks, cs]
            vg = v_ref[ks, cs]
            zero = jnp.zeros_like(qg)
            q4 = jnp.concatenate(
                [jnp.where(lane_head == h, qg, zero) for h in range(NA_GROUP)], axis=0)
            s = lax.dot_general(q4, kg, (((1,), (1,)), ((), ())), preferred_element_type=F32)
            bias = bias_ref[NA_GROUP * g:NA_GROUP * (g + 1), w]
            s = s + bias.reshape(NA_GROUP * GRID_W, n_keys)
            m = jnp.max(s, axis=-1, keepdims=True)
            p = jnp.exp2(s - m)
            l = jnp.sum(p, axis=-1, keepdims=True)
            o4 = jnp.dot(p.astype(BF16), vg, preferred_element_type=F32) / l
            o = o4[0:GRID_W]
            for h in range(1, NA_GROUP):
                o = jnp.where(lane_head == h, o4[h * GRID_W:(h + 1) * GRID_W], o)
            o_ref[qs, cs] = o.astype(o_ref.dtype)


def _na_attention(proj3, bias_tab, weights_f32):
    b, l, _ = proj3.shape
    rows = l // GRID_W
    rb = NA_ROWS_PER_STEP
    span = NA_WIN_ROWS + rb - 1
    steps_per_batch = rows // rb
    n_steps = b * steps_per_batch
    step = lambda bi, t: bi * steps_per_batch + t

    def kv_spec(col0):
        return pl.BlockSpec((None, pl.Element(span * GRID_W), pl.Element(D_NA)),
                            lambda bi, t: (bi, _na_window_start(t * rb, rows) * GRID_W, col0))

    cast_specs = []
    for w in weights_f32:
        blk, walk = _cast_block(w.shape, n_steps)
        cast_specs.append(pl.BlockSpec(blk, lambda bi, t, walk=walk: walk(step(bi, t))))

    outs = pl.pallas_call(
        functools.partial(_na_kernel, rows=rows, n_cast=len(weights_f32)),
        grid=(b, steps_per_batch),
        in_specs=[
            pl.BlockSpec((None, rb * GRID_W, D_NA), lambda bi, t: (bi, t, 0)),
            kv_spec(D_NA),
            kv_spec(2 * D_NA),
            pl.BlockSpec(bias_tab.shape, lambda bi, t: (0, 0, 0, 0),
                         pipeline_mode=pl.Buffered(1)),
        ] + cast_specs,
        out_specs=[pl.BlockSpec((None, rb * GRID_W, D_NA), lambda bi, t: (bi, t, 0))]
        + cast_specs,
        out_shape=[jax.ShapeDtypeStruct((b, l, D_NA), BF16)]
        + [jax.ShapeDtypeStruct(w.shape, BF16) for w in weights_f32],
        compiler_params=_cparams(("arbitrary", "arbitrary")),
        name="na_attn",
    )(proj3, proj3, proj3, bias_tab, *weights_f32)
    return outs[0], outs[1:]


HY_ROWS = 256


def _shift_matrix(n):
    s = np.zeros((2 * n, n), np.float32)
    s[np.arange(1, n), np.arange(0, n - 1)] = 1.0
    s[n + np.arange(0, n - 1), np.arange(1, n)] = 1.0
    return s


def _hy_pre_kernel(u0_ref, u1_ref, u2_ref, w0_ref, w1_ref, w2_ref,
                   b0_ref, b1_ref, b2_ref, sh_ref, x0_ref, vx_ref):
    n, c = u0_ref.shape
    t8 = V7X_SUBLANES
    halo = 2 * t8
    row8 = lax.broadcasted_iota(jnp.int32, (t8, c), 0)
    shift = sh_ref[...]

    def conv_block(u_ref, w_ref, b_ref, r0):
        ub = u_ref[r0:r0 + HY_ROWS, :]
        pn = jnp.dot(shift, ub, preferred_element_type=F32)
        prev, nxt = pn[0:HY_ROWS], pn[HY_ROWS:]
        if r0 > 0:
            up = u_ref[r0 - halo:r0, :].astype(F32)[t8:]
            first = jnp.where(row8 == 0, pltpu.roll(up, 1, axis=0), prev[0:t8])
            prev = jnp.concatenate([first, prev[t8:]], axis=0)
        if r0 + HY_ROWS < n:
            dn = u_ref[r0 + HY_ROWS:r0 + HY_ROWS + halo, :].astype(F32)[0:t8]
            last = jnp.where(row8 == t8 - 1, pltpu.roll(dn, t8 - 1, axis=0),
                             nxt[HY_ROWS - t8:])
            nxt = jnp.concatenate([nxt[0:HY_ROWS - t8], last], axis=0)
        w = w_ref[...]
        return prev * w[0:1] + ub.astype(F32) * w[1:2] + nxt * w[2:3] + b_ref[...]

    for r0 in range(0, n, HY_ROWS):
        rs = slice(r0, r0 + HY_ROWS)
        x0 = conv_block(u0_ref, w0_ref, b0_ref, r0)
        x1 = conv_block(u1_ref, w1_ref, b1_ref, r0)
        v = conv_block(u2_ref, w2_ref, b2_ref, r0)
        x0_ref[rs, :] = x0.astype(x0_ref.dtype)
        vx_ref[rs, :] = (v * x1).astype(vx_ref.dtype)


def _hy_pre(proj3, short_w, short_b, d_hy, cb=256):
    b, l, _ = proj3.shape
    base = 3 * D_NA // cb
    per = d_hy // cb
    shift = jnp.asarray(_shift_matrix(HY_ROWS)).astype(BF16)

    def u_spec(g):
        return pl.BlockSpec((None, l, cb), lambda bi, j: (bi, 0, base + g * per + j))

    def w_spec(g):
        return pl.BlockSpec((3, cb), lambda bi, j: (0, g * per + j))

    def b_spec(g):
        return pl.BlockSpec((1, cb), lambda bi, j: (0, g * per + j))

    out_spec = pl.BlockSpec((None, l, cb), lambda bi, j: (bi, 0, j))
    return pl.pallas_call(
        _hy_pre_kernel,
        grid=(b, per),
        in_specs=[u_spec(0), u_spec(1), u_spec(2), w_spec(0), w_spec(1), w_spec(2),
                  b_spec(0), b_spec(1), b_spec(2),
                  pl.BlockSpec(shift.shape, lambda bi, j: (0, 0))],
        out_specs=[out_spec, out_spec],
        out_shape=[jax.ShapeDtypeStruct((b, l, d_hy), BF16),
                   jax.ShapeDtypeStruct((b, l, d_hy), BF16)],
        compiler_params=_cparams(("parallel", "parallel")),
        name="hy_pre",
    )(proj3, proj3, proj3, short_w, short_w, short_w, short_b, short_b, short_b, shift)


def _filter_kernel(z_ref, w1_ref, b1_ref, w2_ref, b2_ref, w3_ref, b3_ref,
                   fq_ref, w4f_ref, w4b_ref, dl_ref, db_ref, o_ref, h_ref):
    hi = lax.Precision.HIGHEST
    half, kp2 = z_ref.shape
    seq = 2 * half
    kp = kp2 // 2
    order = h_ref.shape[1]

    @pl.when(pl.program_id(0) == 0)
    def _():
        fq = fq_ref[...]

        def mlp(z):
            h = jnp.sin(fq * (jnp.dot(z, w1_ref[...], precision=hi,
                                      preferred_element_type=F32) + b1_ref[...]))
            h = jnp.sin(fq * (jnp.dot(h, w2_ref[...], precision=hi,
                                      preferred_element_type=F32) + b2_ref[...]))
            h = jnp.sin(fq * (jnp.dot(h, w3_ref[...], precision=hi,
                                      preferred_element_type=F32) + b3_ref[...]))
            return h

        def chunk(rs):
            h = mlp(z_ref[rs, :]).astype(h_ref.dtype)
            h_ref[rs, :] = h[:, 0:order]
            h_ref[pl.ds(half + rs.start, rs.size), :] = h[:, order:]

        _for_row_chunks(half, 4 * ROW_CHUNK, chunk)

    t_pos = jnp.concatenate([z_ref[:, 0:1], z_ref[:, kp:kp + 1]], axis=0)
    decay = jnp.exp(-t_pos * dl_ref[...])
    h3 = h_ref[...]
    fwd = jnp.dot(h3, w4f_ref[...].astype(BF16), preferred_element_type=F32) * decay
    bwd = jnp.dot(h3, w4b_ref[...].astype(BF16), preferred_element_type=F32) * decay
    row = lax.broadcasted_iota(jnp.int32, bwd.shape, 0)
    bwd = jnp.where(row == 0, 0.0, bwd)
    fwd = jnp.where(row == 0, fwd + db_ref[...], fwd)
    o_ref[0:seq, :] = fwd.astype(o_ref.dtype)
    o_ref[seq:2 * seq, :] = bwd.astype(o_ref.dtype)


def _block_diag2(w):
    z = jnp.zeros_like(w)
    return jnp.concatenate([jnp.concatenate([w, z], axis=1),
                            jnp.concatenate([z, w], axis=1)], axis=0)


def _filter_taps(z_np, w1p, b1, w2, b2, w3, b3, freq, w4, deltas_abs, d_bias, d_hy, cb=256):
    seq, kpad = z_np.shape
    order = w2.shape[0]
    per = d_hy // cb
    z = jnp.asarray(np.concatenate([z_np[:seq // 2], z_np[seq // 2:]], axis=1))
    w1p, w2, w3 = _block_diag2(w1p), _block_diag2(w2), _block_diag2(w3)
    b1, b2, b3, freq = [jnp.concatenate([a, a], axis=1) for a in (b1, b2, b3, freq)]
    full = lambda shape: pl.BlockSpec(shape, lambda j: tuple(0 for _ in shape))
    return pl.pallas_call(
        _filter_kernel,
        grid=(per,),
        in_specs=[full(z.shape),
                  full(w1p.shape), full(b1.shape),
                  full(w2.shape), full(b2.shape),
                  full(w3.shape), full(b3.shape),
                  full(freq.shape),
                  pl.BlockSpec((order, cb), lambda j: (0, j)),
                  pl.BlockSpec((order, cb), lambda j: (0, per + j)),
                  pl.BlockSpec((1, cb), lambda j: (0, j)),
                  pl.BlockSpec((1, cb), lambda j: (0, j))],
        out_specs=pl.BlockSpec((2 * seq, cb), lambda j: (0, j)),
        out_shape=jax.ShapeDtypeStruct((2 * seq, d_hy), BF16),
        scratch_shapes=[pltpu.VMEM((seq, order), BF16)],
        compiler_params=_cparams(("arbitrary",)),
        name="hy_filter",
    )(z, w1p, b1, w2, b2, w3, b3, freq, w4, w4, deltas_abs, d_bias)


V7X_SUBLANES = 8


def _fft_slow_kernel(wk_ref, x_ref, o_ref):
    k, rb, c = x_ref.shape
    m = o_ref.shape[0]
    x = x_ref[...].astype(F32)
    outs = []
    for t in range(rb // V7X_SUBLANES):
        xs = x[:, t * V7X_SUBLANES:(t + 1) * V7X_SUBLANES, :]
        xs = xs.reshape(k * V7X_SUBLANES, c).astype(BF16)
        o = jnp.dot(wk_ref[...], xs, preferred_element_type=F32)
        outs.append(o.reshape(m, V7X_SUBLANES, c))
    o_ref[...] = jnp.concatenate(outs, axis=1).astype(o_ref.dtype)


def _fft_slow_gate_kernel(wk_ref, x_ref, x0_ref, beta_ref, o_ref):
    k, rb, c = x_ref.shape
    m = o_ref.shape[0]
    x = x_ref[...].astype(F32)
    outs = []
    for t in range(rb // V7X_SUBLANES):
        xs = x[:, t * V7X_SUBLANES:(t + 1) * V7X_SUBLANES, :]
        xs = xs.reshape(k * V7X_SUBLANES, c).astype(BF16)
        o = jnp.dot(wk_ref[...], xs, preferred_element_type=F32)
        outs.append(o.reshape(m, V7X_SUBLANES, c))
    hy = jnp.concatenate(outs, axis=1) * x0_ref[...].astype(F32)
    ms = jnp.mean(hy * hy, axis=-1, keepdims=True)
    o_ref[...] = (hy * lax.rsqrt(ms + EPS) * beta_ref[...]).astype(o_ref.dtype)


def _fft_slow_gate(wk, x4, x04, beta, rb=16):
    g, k, r, c = x4.shape
    m = wk.shape[0] // V7X_SUBLANES
    assert wk.shape[1] == k * V7X_SUBLANES and x04.shape == (g, m, r, c)
    return pl.pallas_call(
        _fft_slow_gate_kernel,
        grid=(g, r // rb),
        in_specs=[pl.BlockSpec(wk.shape, lambda gi, j: (0, 0)),
                  pl.BlockSpec((None, k, rb, c), lambda gi, j: (gi, 0, j, 0)),
                  pl.BlockSpec((None, m, rb, c), lambda gi, j: (gi, 0, j, 0)),
                  pl.BlockSpec((1, c), lambda gi, j: (0, 0))],
        out_specs=pl.BlockSpec((None, m, rb, c), lambda gi, j: (gi, 0, j, 0)),
        out_shape=jax.ShapeDtypeStruct((g, m, r, c), BF16),
        compiler_params=_cparams(("parallel", "parallel")),
        name="fft_slow_gate",
    )(wk, x4, x04, beta)


def _fft_slow(wk, x4, out_dtype, rb=16):
    g, k, r, c = x4.shape
    m = wk.shape[0] // V7X_SUBLANES
    assert wk.shape[1] == k * V7X_SUBLANES
    return pl.pallas_call(
        _fft_slow_kernel,
        grid=(g, r // rb),
        in_specs=[pl.BlockSpec(wk.shape, lambda gi, j: (0, 0)),
                  pl.BlockSpec((None, k, rb, c), lambda gi, j: (gi, 0, j, 0))],
        out_specs=pl.BlockSpec((None, m, rb, c), lambda gi, j: (gi, 0, j, 0)),
        out_shape=jax.ShapeDtypeStruct((g, m, r, c), out_dtype),
        compiler_params=_cparams(("parallel", "parallel")),
        name="fft_slow",
    )(wk, x4)


SPEC_SLABS = 4


def _fft_spec_kernel(e_ref, ei_ref, af_ref, a_ref, o_ref):
    pairs, two, slabs, r, c = a_ref.shape
    for q in range(slabs):
        e = e_ref[q]
        hf = jnp.dot(e, af_ref[0, :, q].reshape(two * r, c), preferred_element_type=F32)
        hb = jnp.dot(e, af_ref[1, :, q].reshape(two * r, c), preferred_element_type=F32)
        kr = hf[0:r] + hb[0:r]
        ki = hf[r:2 * r] - hb[r:2 * r]
        for p in range(pairs):
            b2 = jnp.dot(e, a_ref[p, :, q].reshape(two * r, c), preferred_element_type=F32)
            br, bi = b2[0:r], b2[r:2 * r]
            y2 = jnp.concatenate([br * kr - bi * ki, br * ki + bi * kr], axis=0).astype(BF16)
            g2 = jnp.dot(ei_ref[q], y2, preferred_element_type=F32)
            o_ref[p, :, q] = g2.reshape(two, r, c).astype(o_ref.dtype)


def _fft_spec(e_fwd, e_inv, a_filt, a_data):
    pairs, two, s, r, c = a_data.shape
    q = SPEC_SLABS
    return pl.pallas_call(
        _fft_spec_kernel,
        grid=(s // q,),
        in_specs=[pl.BlockSpec((q, two * r, two * r), lambda i: (i, 0, 0)),
                  pl.BlockSpec((q, two * r, two * r), lambda i: (i, 0, 0)),
                  pl.BlockSpec((2, two, q, r, c), lambda i: (0, 0, i, 0, 0)),
                  pl.BlockSpec((pairs, two, q, r, c), lambda i: (0, 0, i, 0, 0))],
        out_specs=pl.BlockSpec((pairs, two, q, r, c), lambda i: (0, 0, i, 0, 0)),
        out_shape=jax.ShapeDtypeStruct((pairs, two, s, r, c), BF16),
        compiler_params=_cparams(("parallel",)),
        name="fft_spec",
    )(e_fwd, e_inv, a_filt, a_data)


def _fft_constants(seq_len):
    n = 2 * seq_len
    r_, s_ = FFT_R, FFT_S
    assert r_ * s_ == n
    half = s_ // 2
    ks = np.arange(s_)[:, None]
    s = np.arange(s_)[None, :]
    ang = 2.0 * np.pi * ((ks * s) % s_) / s_
    fr, fi = np.cos(ang), -np.sin(ang)
    w1_data = np.block([[fr[:, :half], -fi[:, :half]], [fi[:, :half], fr[:, :half]]])
    w1_filt = np.block([[fr[:, :half]], [fi[:, :half]]])
    so = np.arange(half)[:, None]
    ko = np.arange(s_)[None, :]
    ang_i = 2.0 * np.pi * ((so * ko) % s_) / s_
    cr, ci = np.cos(ang_i), np.sin(ang_i)
    w3 = np.block([[cr, -ci], [ci, cr]]) / n
    ksv = np.arange(s_)[:, None, None]
    kr = np.arange(r_)[None, :, None]
    r = np.arange(r_)[None, None, :]
    ang_e = 2.0 * np.pi * ((r * (ksv + s_ * kr)) % n) / n
    er, ei = np.cos(ang_e), -np.sin(ang_e)
    e_fwd = np.concatenate([np.concatenate([er, -ei], axis=2),
                            np.concatenate([ei, er], axis=2)], axis=1)
    e_inv = np.transpose(e_fwd, (0, 2, 1))
    as_bf16 = lambda a: jnp.asarray(a.astype(np.float32)).astype(BF16)
    eye = np.eye(V7X_SUBLANES)
    slow = [as_bf16(np.kron(w, eye)) for w in (w1_data, w1_filt, w3)]
    return slow[0], slow[1], slow[2], as_bf16(e_fwd), as_bf16(e_inv)


def _mm_out_kernel(na_ref, hyn_ref, bna_ref, w_ref, x_ref, gt_ref, o_ref, ha_ref, hb_ref):
    bm, d_na = na_ref.shape

    def step(h_new, h_cur):
        for r0 in range(0, bm, ROW_CHUNK):
            rs = slice(r0, r0 + ROW_CHUNK)
            na = na_ref[rs, :].astype(F32)
            na_n = na * lax.rsqrt(jnp.mean(na * na, axis=-1, keepdims=True) + EPS) * bna_ref[...]
            h_new[rs, 0:d_na] = na_n.astype(BF16)
            h_new[rs, d_na:] = hyn_ref[rs, :]
        acc = jnp.dot(h_cur[...], w_ref[...], preferred_element_type=F32)
        o_ref[...] = x_ref[...] + gt_ref[0] * acc

    _skewed_steps(ha_ref, hb_ref, step)


def _mm_out(na2d, hyn2d, beta_na, w, x2d, gt, seq_len, bm=512):
    m, d_na = na2d.shape
    d_hy = hyn2d.shape[1]
    k, n = w.shape
    nb = m // bm
    blocks_per_seq = seq_len // bm
    new = lambda t: jnp.minimum(t, nb - 1)
    cur = lambda t: jnp.maximum(t - 1, 0)
    return pl.pallas_call(
        _mm_out_kernel,
        grid=(nb + 1,),
        in_specs=[pl.BlockSpec((bm, d_na), lambda t: (new(t), 0)),
                  pl.BlockSpec((bm, d_hy), lambda t: (new(t), 0)),
                  _resident((1, d_na)),
                  _resident((k, n)),
                  pl.BlockSpec((bm, n), lambda t: (cur(t), 0)),
                  pl.BlockSpec((1, 1, n), lambda t: (cur(t) // blocks_per_seq, 0, 0))],
        out_specs=pl.BlockSpec((bm, n), lambda t: (cur(t), 0)),
        out_shape=jax.ShapeDtypeStruct((m, n), F32),
        scratch_shapes=[pltpu.VMEM((bm, k), BF16), pltpu.VMEM((bm, k), BF16)],
        compiler_params=_cparams(("arbitrary",)),
        name="mm_out",
    )(na2d, hyn2d, beta_na, w, x2d, gt)


HALO = 16


def _mm_up_glu_kernel(x_ref, xp_ref, xn_ref, g_ref, sc_ref, sh_ref, wa_ref, wb_ref,
                      cw_ref, cb_ref, o_ref, ha_ref, hb_ref, *, blocks_per_seq, n_blocks):
    bm, k = x_ref.shape
    t = pl.program_id(0)
    j = pl.program_id(1)
    pos = jnp.minimum(t, n_blocks - 1) % blocks_per_seq

    def norm(x):
        gs = g_ref[...] * (1.0 + sc_ref[0])
        ms = jnp.mean(x * x, axis=-1, keepdims=True)
        return x * lax.rsqrt(ms + EPS) * gs + sh_ref[0]

    def build(h_new):
        zero = jnp.zeros((HALO, k), F32)
        h_new[0:HALO, :] = jnp.where(pos == 0, zero, norm(xp_ref[...])).astype(BF16)
        for r0 in range(0, bm, ROW_CHUNK):
            h_new[HALO + r0:HALO + r0 + ROW_CHUNK, :] = norm(
                x_ref[r0:r0 + ROW_CHUNK, :]).astype(BF16)
        h_new[HALO + bm:, :] = jnp.where(pos == blocks_per_seq - 1, zero,
                                         norm(xn_ref[...])).astype(BF16)

    def multiply(h_cur):
        a = jnp.dot(h_cur[...], wa_ref[...], preferred_element_type=F32)
        b = jnp.dot(h_cur[HALO:HALO + bm, :], wb_ref[...], preferred_element_type=F32)
        n_ext = a.shape[0]
        w = cw_ref[...]
        prev = pltpu.roll(a, 1, axis=0)[HALO:HALO + bm]
        nxt = pltpu.roll(a, n_ext - 1, axis=0)[HALO:HALO + bm]
        ac = prev * w[0:1] + a[HALO:HALO + bm] * w[1:2] + nxt * w[2:3] + cb_ref[...]
        gelu = 0.5 * ac * (1.0 + lax.erf(ac * (1.0 / math.sqrt(2.0))))
        o_ref[...] = (gelu * b).astype(o_ref.dtype)

    even = t % 2 == 0

    @pl.when((t == 0) & (j == 0))
    def _():
        build(ha_ref)

    @pl.when((t > 0) & even & (j == 0))
    def _():
        build(ha_ref)
        multiply(hb_ref)

    @pl.when((t > 0) & even & (j > 0))
    def _():
        multiply(hb_ref)

    @pl.when(jnp.logical_not(even) & (j == 0))
    def _():
        build(hb_ref)
        multiply(ha_ref)

    @pl.when(jnp.logical_not(even) & (j > 0))
    def _():
        multiply(ha_ref)


def _mm_up_glu(x2d, g, sc, sh, w_up, conv_w, conv_b, seq_len, bm=1024, bn=512):
    m, k = x2d.shape
    d_ff = w_up.shape[1] // 2
    nbn = d_ff // bn
    nb = m // bm
    blocks_per_seq = seq_len // bm
    hb = bm // HALO
    last = m // HALO - 1
    new = lambda t: jnp.minimum(t, nb - 1)
    cur = lambda t: jnp.maximum(t - 1, 0)
    col = lambda t, j: jnp.where(t == 0, 0, j)
    return pl.pallas_call(
        functools.partial(_mm_up_glu_kernel, blocks_per_seq=blocks_per_seq, n_blocks=nb),
        grid=(nb + 1, nbn),
        in_specs=[pl.BlockSpec((bm, k), lambda t, j: (new(t), 0)),
                  pl.BlockSpec((HALO, k), lambda t, j: (jnp.maximum(new(t) * hb - 1, 0), 0)),
                  pl.BlockSpec((HALO, k),
                               lambda t, j: (jnp.minimum((new(t) + 1) * hb, last), 0)),
                  pl.BlockSpec((1, k), lambda t, j: (0, 0)),
                  pl.BlockSpec((1, 1, k), lambda t, j: (new(t) // blocks_per_seq, 0, 0)),
                  pl.BlockSpec((1, 1, k), lambda t, j: (new(t) // blocks_per_seq, 0, 0)),
                  pl.BlockSpec((k, bn), lambda t, j: (0, col(t, j))),
                  pl.BlockSpec((k, bn), lambda t, j: (0, nbn + col(t, j))),
                  pl.BlockSpec((3, bn), lambda t, j: (0, col(t, j))),
                  pl.BlockSpec((1, bn), lambda t, j: (0, col(t, j)))],
        out_specs=pl.BlockSpec((bm, bn), lambda t, j: (cur(t), col(t, j))),
        out_shape=jax.ShapeDtypeStruct((m, d_ff), BF16),
        scratch_shapes=[pltpu.VMEM((bm + 2 * HALO, k), BF16),
                        pltpu.VMEM((bm + 2 * HALO, k), BF16)],
        compiler_params=_cparams(("arbitrary", "arbitrary")),
        name="mm_up_glu",
    )(x2d, x2d, x2d, g, sc, sh, w_up, w_up, conv_w, conv_b)


def _mm_down_kernel(g_ref, w_ref, x_ref, gt_ref, gf_ref, o_ref):
    acc = jnp.dot(g_ref[...], w_ref[...], preferred_element_type=F32)
    x = x_ref[...] + gt_ref[0] * acc
    ms = jnp.mean(x * x, axis=-1, keepdims=True)
    o_ref[...] = x * lax.rsqrt(ms + EPS) * gf_ref[...]


def _mm_down(g2d, w, x2d, gt, g_final, seq_len, bm=256):
    m, k = g2d.shape
    n = w.shape[1]
    blocks_per_seq = seq_len // bm
    return pl.pallas_call(
        _mm_down_kernel,
        grid=(m // bm,),
        in_specs=[pl.BlockSpec((bm, k), lambda i: (i, 0)),
                  _resident((k, n)),
                  pl.BlockSpec((bm, n), lambda i: (i, 0)),
                  pl.BlockSpec((1, 1, n), lambda i: (i // blocks_per_seq, 0, 0)),
                  _resident((1, n))],
        out_specs=pl.BlockSpec((bm, n), lambda i: (i, 0)),
        out_shape=jax.ShapeDtypeStruct((m, n), F32),
        compiler_params=_cparams(("parallel",)),
        name="mm_down",
    )(g2d, w, x2d, gt, g_final)


def _position_features(seq_len, kpad):
    t = np.linspace(0.0, 1.0, seq_len)[:, None]
    bands = (FILTER_EMB - 1) // 2
    w = 2.0 * np.pi * np.arange(seq_len)[:, None] / seq_len
    fr = np.linspace(1e-4, bands - 1, bands)[None, :]
    z = np.concatenate([t, np.cos(fr * w), -np.sin(fr * w)], axis=-1)
    z = np.pad(z, ((0, 0), (0, kpad - z.shape[1])))
    return z.astype(np.float32)


def kernel(x, c, w_ada, b_ada, g_mix, w_in, na_rpb, hy_short_w, hy_short_b,
           hy_filt_w1, hy_filt_b1, hy_filt_w2, hy_filt_b2, hy_filt_w3, hy_filt_b3,
           hy_filt_w4, hy_filt_freq, hy_bias, beta_na, beta_hy, w_out, g_ffn,
           w_up, ffn_conv_w, ffn_conv_b, w_down, g_final):
    b, l, d = x.shape
    depth = w_ada.shape[0]
    d_hy = d - D_NA
    d_ff = w_down.shape[1]
    rows = l // GRID_W
    m = b * l
    assert depth == 1 and b % 2 == 0 and 2 * l == FFT_R * FFT_S

    c_pad = jnp.pad(c, ((0, V7X_SUBLANES - b), (0, 0)))
    b_ada2 = b_ada[0][None, :]
    col_scale = np.ones((1, w_in.shape[2]), np.float32)
    col_scale[:, :D_NA] = NA_Q_SCALE
    mod_head, bias_tab, w_in_bf = _adaln_bias(c_pad, w_ada[0], b_ada2, na_rpb[0].reshape(-1),
                                              w_in[0], jnp.asarray(col_scale), 2 * d)
    sh1, sc1 = [t[:b, None, :] for t in jnp.split(mod_head, 2, axis=-1)]

    x2d = x.reshape(m, d)

    proj, mod_tail = _mm_norm(x2d, g_mix, sc1, sh1, w_in_bf, c_pad, w_ada[0], b_ada2,
                              2 * d, l)
    gt1, sh2, sc2, gt2 = [t[:b, None, :] for t in jnp.split(mod_tail, 4, axis=-1)]

    na, (w_out_bf, w_up_bf, w_down_bf) = _na_attention(
        proj.reshape(b, l, proj.shape[1]), bias_tab, (w_out[0], w_up[0], w_down[0]))
    na2d = na.reshape(m, D_NA)

    x0, vx = _hy_pre(proj.reshape(b, l, proj.shape[1]), hy_short_w[0], hy_short_b[0][None, :], d_hy)

    kpad = V7X_LANES
    z_np = _position_features(l, kpad)
    w1p = jnp.pad(hy_filt_w1[0], ((0, kpad - FILTER_EMB), (0, 0)))
    max_decay = math.log(DECAY_TARGET) / FAST_DECAY_PCT
    min_decay = math.log(DECAY_TARGET) / SLOW_DECAY_PCT
    deltas_abs = np.abs(np.linspace(min_decay, max_decay, d_hy))[None, :].astype(np.float32)
    taps = _filter_taps(z_np, w1p, hy_filt_b1,
                        hy_filt_w2[0], hy_filt_b2, hy_filt_w3[0], hy_filt_b3,
                        hy_filt_freq, hy_filt_w4[0], jnp.asarray(deltas_abs), hy_bias, d_hy)

    w1_data, w1_filt, w3, e_fwd, e_inv = _fft_constants(l)
    a_filt = _fft_slow(w1_filt, taps.reshape(2, FFT_S // 2, FFT_R, d_hy), BF16)
    a_data = _fft_slow(w1_data, vx.reshape(b // 2, FFT_S, FFT_R, d_hy), BF16)
    g_spec = _fft_spec(e_fwd, e_inv, a_filt.reshape(2, 2, FFT_S, FFT_R, d_hy),
                       a_data.reshape(b // 2, 2, FFT_S, FFT_R, d_hy))
    hyn = _fft_slow_gate(w3, g_spec.reshape(b // 2, 2 * FFT_S, FFT_R, d_hy),
                         x0.reshape(b // 2, FFT_S, FFT_R, d_hy), beta_hy)

    x1 = _mm_out(na2d, hyn.reshape(m, d_hy), beta_na, w_out_bf, x2d, gt1, l)

    gl = _mm_up_glu(x1, g_ffn, sc2, sh2, w_up_bf, ffn_conv_w[0],
                    ffn_conv_b[0][None, :], l)
    out = _mm_down(gl, w_down_bf, x1, gt2, g_final[None, :], l)
    return out.reshape(b, l, d)
```
